```python
import jax, jax.numpy as jnp
from jax import lax
import numpy as np

D_MODEL = 4096
BATCH = 1
SEQ = 16384
DEPTH = 2

GRID_W = 64
CTX_LEN = 256
ALPHA = (2 * DEPTH) ** 0.25
BETA = (8 * DEPTH) ** -0.25
LN_EPS = 1e-6
ROPE_THETA = 10000.0
MIX_W = D_MODEL
A_WIDTH = MIX_W // 2
A_HEAD = 64
A_HEADS = A_WIDTH // A_HEAD
A_DECAY_LORA = 96
A_AAA_LORA = 96
A_GATE_LORA = 256
A_IN = 3 * A_WIDTH + A_GATE_LORA + 2 * A_DECAY_LORA + 2 * A_AAA_LORA
A_GN_EPS = 64e-5
B_WIDTH = MIX_W - A_WIDTH
B_HEAD = 256
B_HEADS = B_WIDTH // B_HEAD
B_IN = 4 * B_WIDTH
IN0 = A_IN + B_IN
RET_CHUNK = 128
B_GN_EPS = 1e-5
C_HEAD = 128
C_Q_HEADS = D_MODEL // C_HEAD
C_KV_HEADS = 8
C_GROUP = C_Q_HEADS // C_KV_HEADS
C_IN = (C_Q_HEADS + 2 * C_KV_HEADS) * C_HEAD
Q_BLOCK = 128
QK_EPS = 1e-6
D_FF = 11008
N_EXPERTS = 8
TOP_K = 2
D_FF_EXPERT = 4096

kernel_name = "hybrid_rwkv7_retention_gqa_moe_dit"


def layer_norm(x, g, b):
    xf = x.astype(jnp.float32)
    mu = xf.mean(-1, keepdims=True)
    var = jnp.square(xf - mu).mean(-1, keepdims=True)
    return ((xf - mu) * lax.rsqrt(var + LN_EPS) * g + b).astype(x.dtype)


def rms_norm(x, g):
    xf = x.astype(jnp.float32)
    return (xf * lax.rsqrt(jnp.square(xf).mean(-1, keepdims=True) + QK_EPS) * g).astype(x.dtype)


def head_norm(y, eps, g, b):
    yf = y.astype(jnp.float32)
    mu = yf.mean(-1, keepdims=True)
    var = jnp.square(yf - mu).mean(-1, keepdims=True)
    yn = ((yf - mu) * lax.rsqrt(var + eps)).reshape(y.shape[0], y.shape[1], -1)
    return yn * g + b


def modulate(x, shift, scale):
    return x * (1 + scale) + shift


def swiglu(u, w1, w3, w2):
    return (jax.nn.silu(u @ w1) * (u @ w3)) @ w2


def grid_positions(n_tokens):
    rows = n_tokens // GRID_W
    row = jnp.repeat(jnp.arange(rows, dtype=jnp.int32), GRID_W)
    col = jnp.tile(jnp.arange(GRID_W, dtype=jnp.int32), rows)
    return row, col


def axial_rope(x, row, col):
    half = x.shape[-1] // 2

    def rot(xp, pos):
        n = xp.shape[-1] // 2
        inv = ROPE_THETA ** (-jnp.arange(n, dtype=jnp.float32) / n)
        ang = pos.astype(jnp.float32)[:, None] * inv[None, :]
        cos, sin = jnp.cos(ang), jnp.sin(ang)
        x1, x2 = xp[..., :n], xp[..., n:]
        return jnp.concatenate([x1 * cos - x2 * sin, x1 * sin + x2 * cos], -1)

    return jnp.concatenate([rot(x[..., :half], row), rot(x[..., half:], col)], -1).astype(x.dtype)


def centred_shift(p):
    prev = jnp.pad(p[:, :-1], ((0, 0), (1, 0), (0, 0)))
    nxt = jnp.pad(p[:, 1:], ((0, 0), (0, 1), (0, 0)))
    return 0.5 * (prev + nxt)


def a_heads(t):
    return t.reshape(t.shape[0], t.shape[1], A_HEADS, A_HEAD)


def rwkv7_prepare(p, mu, w0, w2, a0, a2, g2, k_k, k_a):
    p = p + mu * (centred_shift(p) - p)
    offs = [A_WIDTH, 2 * A_WIDTH, 3 * A_WIDTH, 3 * A_WIDTH + A_GATE_LORA,
            3 * A_WIDTH + A_GATE_LORA + 2 * A_DECAY_LORA]
    r, k, v, gl, wl, al = jnp.split(p, offs, -1)
    B, T = p.shape[0], p.shape[1]
    wl = wl.reshape(B, T, 2, A_DECAY_LORA)
    al = al.reshape(B, T, 2, A_AAA_LORA)
    g = jax.nn.sigmoid(gl) @ g2
    wlog = -jax.nn.softplus(-(w0 + jnp.einsum('btdr,drc->btdc', jnp.tanh(wl), w2))) - 0.5
    decay = jnp.exp(-jnp.exp(wlog.astype(jnp.float32)))
    a = jax.nn.sigmoid(a0 + jnp.einsum('btdr,drc->btdc', al, a2))
    kk = a_heads(k * k_k).astype(jnp.float32)
    kk = kk / jnp.maximum(jnp.sqrt(jnp.sum(kk * kk, -1, keepdims=True)), 1e-12)
    kd = k[:, :, None, :] * (1 + (a - 1) * k_a)
    return r, v, kk, g, decay, a, kd


def rwkv7_scan(r, w, kk, a, k, v, S0, reverse):
    xs = tuple(jnp.moveaxis(t.astype(jnp.float32), 1, 0) for t in (r, w, kk, a, k, v))

    def step(S, inp):
        r_t, w_t, kk_t, a_t, k_t, v_t = inp
        sa = jnp.einsum('bhvk,bhk->bhv', S, -kk_t)
        S = (S * w_t[:, :, None, :] + sa[..., None] * (kk_t * a_t)[:, :, None, :]
             + v_t[..., None] * k_t[:, :, None, :])
        return S, jnp.einsum('bhvk,bhk->bhv', S, r_t)

    S, y = lax.scan(step, S0, xs, reverse=reverse)
    return jnp.moveaxis(y, 0, 1), S


def rwkv7_mixer(p_lat, p_ctx, mu, w0, w2, a0, a2, g2, k_k, k_a, r_k, lnx_g, lnx_b):
    lat = rwkv7_prepare(p_lat, mu, w0, w2, a0, a2, g2, k_k, k_a)
    ctx = rwkv7_prepare(p_ctx, mu, w0, w2, a0, a2, g2, k_k, k_a)
    S0 = jnp.zeros((p_lat.shape[0], A_HEADS, A_HEAD, A_HEAD), jnp.float32)

    def run(prep, d, S_init, reverse):
        r, v, kk, g, decay, a, kd = prep
        y, S = rwkv7_scan(a_heads(r), a_heads(decay[:, :, d]), kk, a_heads(a[:, :, d]),
                          a_heads(kd[:, :, d]), a_heads(v), S_init, reverse)
        bonus = jnp.sum(a_heads(r) * a_heads(kd[:, :, d]) * r_k, -1, keepdims=True) * a_heads(v)
        return y, bonus, S

    outs = []
    for prep_is_lat in (False, True):
        outs.append([])
    y_lat, b_lat, y_ctx, b_ctx = 0.0, 0.0, 0.0, 0.0
    for d, reverse in ((0, False), (1, True)):
        yc, bc, Sc = run(ctx, d, S0, reverse)
        yl, bl, _ = run(lat, d, Sc, reverse)
        y_ctx, b_ctx = y_ctx + yc, b_ctx + bc
        y_lat, b_lat = y_lat + yl, b_lat + bl

    def finish(y, bonus, g):
        o = head_norm(y, A_GN_EPS, lnx_g, lnx_b) + bonus.reshape(y.shape[0], y.shape[1], -1)
        return o * g

    return finish(y_lat, b_lat, lat[3]), finish(y_ctx, b_ctx, ctx[3])


def retention_chunks(q, k, v, log_g, S0, strict):
    B, H, T, d = q.shape
    n = T // RET_CHUNK
    idx = jnp.arange(RET_CHUNK, dtype=jnp.float32)
    diff = idx[:, None] - idx[None, :]
    mask = diff > 0 if strict else diff >= 0
    decay_mat = jnp.where(mask, jnp.exp(log_g[:, None, None] * jnp.maximum(diff, 0.0)), 0.0)
    q_dec = jnp.exp(log_g[:, None] * (idx + 1.0))[..., None]
    k_dec = jnp.exp(log_g[:, None] * (RET_CHUNK - 1.0 - idx))[..., None]
    chunk_dec = jnp.exp(log_g * RET_CHUNK)[:, None, None]

    def to_chunks(t):
        return jnp.moveaxis(t.astype(jnp.float32).reshape(B, H, n, RET_CHUNK, -1), 2, 0)

    def step(S, qkv):
        qi, ki, vi = qkv
        inner = jnp.einsum('bhqd,bhkd->bhqk', qi, ki) * decay_mat
        y = jnp.einsum('bhqk,bhkv->bhqv', inner, vi) + jnp.einsum('bhqd,bhdv->bhqv', qi * q_dec, S)
        S = S * chunk_dec + jnp.einsum('bhkd,bhkv->bhdv', ki * k_dec, vi)
        return S, y

    S, y = lax.scan(step, S0, (to_chunks(q), to_chunks(k), to_chunks(v)))
    return jnp.moveaxis(y, 0, 2).reshape(B, H, T, -1), S


def retention_mixer(p_lat, p_ctx, gn_g, gn_b, row, col):
    log_g_f = jnp.log1p(-jnp.exp2(-5.0 - jnp.arange(B_HEADS, dtype=jnp.float32)))
    log_g_b = log_g_f[::-1]

    def split(p, rope):
        q, k, v, gate = jnp.split(p, 4, -1)
        B, T = p.shape[0], p.shape[1]
        hd = lambda t: t.reshape(B, T, B_HEADS, B_HEAD).transpose(0, 2, 1, 3)
        q, k, v = hd(q), hd(k) * (B_HEAD ** -0.5), hd(v)
        if rope:
            q, k = axial_rope(q, row, col), axial_rope(k, row, col)
        return q, k, v, gate

    q, k, v, gate = split(p_lat, True)
    qc, kc, vc, gate_c = split(p_ctx, False)
    flip = lambda t: jnp.flip(t, axis=2)
    S0 = jnp.zeros((p_lat.shape[0], B_HEADS, B_HEAD, B_HEAD), jnp.float32)
    yc_f, S_f = retention_chunks(qc, kc, vc, log_g_f, S0, False)
    yc_b, S_b = retention_chunks(flip(qc), flip(kc), flip(vc), log_g_b, S0, True)
    yl_f, _ = retention_chunks(q, k, v, log_g_f, S_f, False)
    yl_b, _ = retention_chunks(flip(q), flip(k), flip(v), log_g_b, S_b, True)

    def finish(y, gt):
        return head_norm(y.transpose(0, 2, 1, 3), B_GN_EPS, gn_g, gn_b) * jax.nn.silu(gt)

    return finish(yl_f + flip(yl_b), gate), finish(yc_f + flip(yc_b), gate_c)


def attend(q, k, v):
    s = jnp.einsum('bgrqd,bgkd->bgrqk', q, k).astype(jnp.float32) * (C_HEAD ** -0.5)
    p = jax.nn.softmax(s, axis=-1)
    return jnp.einsum('bgrqk,bgkd->bgrqd', p.astype(v.dtype), v)


def gqa_mixer(u, uc, w_qkv, qn_g, kn_g, row, col, need_ctx):
    def proj(x, rope):
        B, T = x.shape[0], x.shape[1]
        q, k, v = jnp.split(x @ w_qkv, [C_Q_HEADS * C_HEAD, (C_Q_HEADS + C_KV_HEADS) * C_HEAD], -1)
        q = rms_norm(q.reshape(B, T, C_Q_HEADS, C_HEAD), qn_g).transpose(0, 2, 1, 3)
        k = rms_norm(k.reshape(B, T, C_KV_HEADS, C_HEAD), kn_g).transpose(0, 2, 1, 3)
        v = v.reshape(B, T, C_KV_HEADS, C_HEAD).transpose(0, 2, 1, 3)
        if rope:
            q, k = axial_rope(q, row, col), axial_rope(k, row, col)
        return q.reshape(B, C_KV_HEADS, C_GROUP, T, C_HEAD), k, v

    q, k, v = proj(u, True)
    qc, kc, vc = proj(uc, False)
    B, T = u.shape[0], u.shape[1]
    k_all = jnp.concatenate([kc, k], axis=2)
    v_all = jnp.concatenate([vc, v], axis=2)
    nb = T // Q_BLOCK
    qb = jnp.moveaxis(q.reshape(B, C_KV_HEADS, C_GROUP, nb, Q_BLOCK, C_HEAD), 3, 0)
    o = lax.map(lambda blk: attend(blk, k_all, v_all), qb)
    o = jnp.moveaxis(o, 0, 3).reshape(B, C_KV_HEADS, C_GROUP, T, C_HEAD)
    o = o.transpose(0, 3, 1, 2, 4).reshape(B, T, C_Q_HEADS * C_HEAD)
    oc = None
    if need_ctx:
        L = uc.shape[1]
        oc = attend(qc, kc, vc).transpose(0, 3, 1, 2, 4).reshape(B, L, C_Q_HEADS * C_HEAD)
    return o, oc


def moe_swiglu(u, router, w1, w3, w2):
    logits = (u @ router).astype(jnp.float32)
    top_v, top_i = lax.top_k(logits, TOP_K)
    gates = jax.nn.softmax(top_v, axis=-1)
    combine = jnp.einsum('btk,btke->bte', gates,
                         jax.nn.one_hot(top_i, N_EXPERTS, dtype=jnp.float32)).astype(u.dtype)
    out = jnp.zeros_like(u)
    for e in range(N_EXPERTS):
        out = out + combine[..., e:e + 1] * swiglu(u, w1[e], w3[e], w2[e])
    return out


def even_layer(h, hc, mod, modc, w_in, a_mu, a_w0, a_w2, a_a0, a_a2, a_g2, a_k_k, a_k_a, a_r_k,
               a_lnx_g, a_lnx_b, b_gn_g, b_gn_b, w_out, ln1_g, ln1_b, ffn_w1, ffn_w3, ffn_w2,
               ln2_g, ln2_b, row, col, need_ctx):
    sh1, sc1, gt1, sh2, sc2, gt2 = jnp.split(mod, 6, -1)
    csh1, csc1, cgt1, csh2, csc2, cgt2 = jnp.split(modc, 6, -1)
    p = modulate(h, sh1, sc1) @ w_in
    pc = modulate(hc, csh1, csc1) @ w_in
    ya, yac = rwkv7_mixer(p[..., :A_IN], pc[..., :A_IN], a_mu, a_w0, a_w2, a_a0, a_a2, a_g2,
                          a_k_k, a_k_a, a_r_k, a_lnx_g, a_lnx_b)
    yb, ybc = retention_mixer(p[..., A_IN:], pc[..., A_IN:], b_gn_g, b_gn_b, row, col)
    h = layer_norm(ALPHA * h + gt1 * (jnp.concatenate([ya, yb], -1) @ w_out), ln1_g, ln1_b)
    h = layer_norm(ALPHA * h + gt2 * swiglu(modulate(h, sh2, sc2), ffn_w1, ffn_w3, ffn_w2), ln2_g, ln2_b)
    if need_ctx:
        hc = layer_norm(ALPHA * hc + cgt1 * (jnp.concatenate([yac, ybc], -1) @ w_out), ln1_g, ln1_b)
        hc = layer_norm(ALPHA * hc + cgt2 * swiglu(modulate(hc, csh2, csc2), ffn_w1, ffn_w3, ffn_w2),
                        ln2_g, ln2_b)
    return h, hc


def odd_layer(h, hc, mod, modc, w_qkv, q_norm_g, k_norm_g, w_out, ln1_g, ln1_b, router,
              moe_w1, moe_w3, moe_w2, ln2_g, ln2_b, row, col, need_ctx):
    sh1, sc1, gt1, sh2, sc2, gt2 = jnp.split(mod, 6, -1)
    csh1, csc1, cgt1, csh2, csc2, cgt2 = jnp.split(modc, 6, -1)
    o, oc = gqa_mixer(modulate(h, sh1, sc1), modulate(hc, csh1, csc1), w_qkv, q_norm_g, k_norm_g,
                      row, col, need_ctx)
    h = layer_norm(ALPHA * h + gt1 * (o @ w_out), ln1_g, ln1_b)
    h = layer_norm(ALPHA * h + gt2 * moe_swiglu(modulate(h, sh2, sc2), router, moe_w1, moe_w3, moe_w2),
                   ln2_g, ln2_b)
    if need_ctx:
        hc = layer_norm(ALPHA * hc + cgt1 * (oc @ w_out), ln1_g, ln1_b)
        hc = layer_norm(ALPHA * hc + cgt2 * moe_swiglu(modulate(hc, csh2, csc2), router, moe_w1, moe_w3,
                                                      moe_w2), ln2_g, ln2_b)
    return h, hc


def setup_inputs(seed: int = 0) -> dict:
    key = jax.random.key(seed)
    keys = jax.random.split(key, 64)
    counter = [0]

    def nk():
        counter[0] += 1
        return keys[counter[0] - 1]

    def nrm(shape, scale):
        return jax.random.normal(nk(), shape, jnp.float32) * scale

    D = D_MODEL
    inp = {}
    inp['x'] = nrm((BATCH, SEQ, D), 1.0)
    inp['c'] = nrm((BATCH, D), 1.0)
    inp['ctx'] = nrm((BATCH, CTX_LEN, D), 1.0)
    inp['c_ctx'] = nrm((D,), 1.0)
    inp['l0_mod_w'] = nrm((D, 6 * D), 0.5 * D ** -0.5)
    inp['l0_mod_b'] = nrm((6 * D,), 0.02)
    inp['l0_w_in'] = nrm((D, IN0), D ** -0.5)
    inp['l0_a_mu'] = jax.random.uniform(nk(), (A_IN,), jnp.float32)
    ratio = jnp.arange(A_WIDTH, dtype=jnp.float32) / (A_WIDTH - 1)
    inp['l0_a_w0'] = (-6.0 + 5.0 * ratio ** 1.5)[None, :] + nrm((2, A_WIDTH), 0.1)
    inp['l0_a_w2'] = nrm((2, A_DECAY_LORA, A_WIDTH), 0.1 * A_DECAY_LORA ** -0.5)
    inp['l0_a_a0'] = nrm((2, A_WIDTH), 0.1)
    inp['l0_a_a2'] = nrm((2, A_AAA_LORA, A_WIDTH), 0.1 * A_AAA_LORA ** -0.5)
    inp['l0_a_g2'] = nrm((A_GATE_LORA, A_WIDTH), A_GATE_LORA ** -0.5)
    inp['l0_a_k_k'] = 0.85 + nrm((A_WIDTH,), 0.02)
    inp['l0_a_k_a'] = 1.0 + nrm((A_WIDTH,), 0.02)
    inp['l0_a_r_k'] = nrm((A_HEADS, A_HEAD), 0.1)
    inp['l0_a_lnx_g'] = 1.0 + nrm((A_WIDTH,), 0.02)
    inp['l0_a_lnx_b'] = nrm((A_WIDTH,), 0.02)
    inp['l0_b_gn_g'] = 1.0 + nrm((B_WIDTH,), 0.02)
    inp['l0_b_gn_b'] = nrm((B_WIDTH,), 0.02)
    inp['l0_w_out'] = nrm((MIX_W, D), BETA * MIX_W ** -0.5)
    inp['l0_ln1_g'] = 1.0 + nrm((D,), 0.02)
    inp['l0_ln1_b'] = nrm((D,), 0.02)
    inp['l0_ffn_w1'] = nrm((D, D_FF), D ** -0.5)
    inp['l0_ffn_w3'] = nrm((D, D_FF), D ** -0.5)
    inp['l0_ffn_w2'] = nrm((D_FF, D), BETA * D_FF ** -0.5)
    inp['l0_ln2_g'] = 1.0 + nrm((D,), 0.02)
    inp['l0_ln2_b'] = nrm((D,), 0.02)
    inp['l1_mod_w'] = nrm((D, 6 * D), 0.5 * D ** -0.5)
    inp['l1_mod_b'] = nrm((6 * D,), 0.02)
    inp['l1_w_qkv'] = nrm((D, C_IN), D ** -0.5)
    inp['l1_q_norm_g'] = 1.0 + nrm((C_HEAD,), 0.02)
    inp['l1_k_norm_g'] = 1.0 + nrm((C_HEAD,), 0.02)
    inp['l1_w_out'] = nrm((C_Q_HEADS * C_HEAD, D), BETA * (C_Q_HEADS * C_HEAD) ** -0.5)
    inp['l1_ln1_g'] = 1.0 + nrm((D,), 0.02)
    inp['l1_ln1_b'] = nrm((D,), 0.02)
    inp['l1_router'] = nrm((D, N_EXPERTS), D ** -0.5)
    inp['l1_moe_w1'] = nrm((N_EXPERTS, D, D_FF_EXPERT), D ** -0.5)
    inp['l1_moe_w3'] = nrm((N_EXPERTS, D, D_FF_EXPERT), D ** -0.5)
    inp['l1_moe_w2'] = nrm((N_EXPERTS, D_FF_EXPERT, D), BETA * D_FF_EXPERT ** -0.5)
    inp['l1_ln2_g'] = 1.0 + nrm((D,), 0.02)
    inp['l1_ln2_b'] = nrm((D,), 0.02)
    return inp


def reference(x, c, ctx, c_ctx,
              l0_mod_w, l0_mod_b, l0_w_in, l0_a_mu, l0_a_w0, l0_a_w2, l0_a_a0, l0_a_a2, l0_a_g2,
              l0_a_k_k, l0_a_k_a, l0_a_r_k, l0_a_lnx_g, l0_a_lnx_b, l0_b_gn_g, l0_b_gn_b, l0_w_out,
              l0_ln1_g, l0_ln1_b, l0_ffn_w1, l0_ffn_w3, l0_ffn_w2, l0_ln2_g, l0_ln2_b,
              l1_mod_w, l1_mod_b, l1_w_qkv, l1_q_norm_g, l1_k_norm_g, l1_w_out, l1_ln1_g, l1_ln1_b,
              l1_router, l1_moe_w1, l1_moe_w3, l1_moe_w2, l1_ln2_g, l1_ln2_b):
    layer_params = [
        (l0_mod_w, l0_mod_b, l0_w_in, l0_a_mu, l0_a_w0, l0_a_w2, l0_a_a0, l0_a_a2, l0_a_g2, l0_a_k_k,
         l0_a_k_a, l0_a_r_k, l0_a_lnx_g, l0_a_lnx_b, l0_b_gn_g, l0_b_gn_b, l0_w_out, l0_ln1_g, l0_ln1_b,
         l0_ffn_w1, l0_ffn_w3, l0_ffn_w2, l0_ln2_g, l0_ln2_b),
        (l1_mod_w, l1_mod_b, l1_w_qkv, l1_q_norm_g, l1_k_norm_g, l1_w_out, l1_ln1_g, l1_ln1_b,
         l1_router, l1_moe_w1, l1_moe_w3, l1_moe_w2, l1_ln2_g, l1_ln2_b),
    ]
    row, col = grid_positions(x.shape[1])
    silu_c = jax.nn.silu(c)
    silu_cc = jax.nn.silu(c_ctx)
    h, hc = x, ctx
    for i in range(DEPTH):
        p = layer_params[i]
        need_ctx = i < DEPTH - 1
        mod = (silu_c @ p[0] + p[1])[:, None, :]
        modc = (silu_cc @ p[0] + p[1])[None, None, :]
        if i % 2 == 0:
            h, hc = even_layer(h, hc, mod, modc, *p[2:], row=row, col=col, need_ctx=need_ctx)
        else:
            h, hc = odd_layer(h, hc, mod, modc, *p[2:], row=row, col=col, need_ctx=need_ctx)
    return h
```

```python
import functools

import jax
import jax.numpy as jnp
from jax import lax
from jax.experimental import pallas as pl
from jax.experimental.pallas import tpu as pltpu

F32 = jnp.float32
BF16 = jnp.bfloat16
HIGHEST = lax.Precision.HIGHEST

GRID_W = 64
DEPTH = 2
ALPHA = (2 * DEPTH) ** 0.25
LN_EPS = 1e-6
ROPE_THETA = 10000.0
A_HEAD = 64
A_DECAY_LORA = 96
A_AAA_LORA = 96
A_GATE_LORA = 256
A_GN_EPS = 64e-5
B_HEAD = 256
RET_CHUNK = 128
B_GN_EPS = 1e-5
C_HEAD = 128
C_GROUP = 4
QK_EPS = 1e-6
TOP_K = 2

LANES = 128
SUBLANES = 8
VMEM_LIMIT_BYTES = 60 * 1024 * 1024

RWKV_CHUNK = 64
PAIR = 2 * A_HEAD


def _cparams(*sem):
    return pltpu.CompilerParams(dimension_semantics=sem, vmem_limit_bytes=VMEM_LIMIT_BYTES)


def _pick(n, candidates):
    for c in candidates:
        if c <= n and n % c == 0:
            return c
    return n


def _round_up(n, m):
    return (n + m - 1) // m * m


def _silu(x):
    return x * (1.0 / (1.0 + jnp.exp(-x)))


def _sigmoid(x):
    return 1.0 / (1.0 + jnp.exp(-x))


def _dot(a, b):
    return jnp.dot(a, b, preferred_element_type=F32)


def _dot_nt(a, b, precision=None):
    return lax.dot_general(a, b, (((1,), (1,)), ((), ())), preferred_element_type=F32, precision=precision)


def _row_is_ctx(shape, row0, n_ctx):
    rows = row0 + lax.broadcasted_iota(jnp.int32, shape, 0)
    return rows < n_ctx


def _select_stream(mod_ref, shape, row0, n_ctx):
    lat = mod_ref[0:1, :]
    if n_ctx == 0:
        return jnp.broadcast_to(lat, shape)
    return jnp.where(_row_is_ctx(shape, row0, n_ctx), mod_ref[1:2, :], lat)


def _modvec_kernel(c_ref, w_ref, b_ref, o_ref):
    s = _silu(c_ref[...])
    o_ref[...] = jnp.dot(s, w_ref[...], preferred_element_type=F32, precision=HIGHEST) + b_ref[...]


def _mod_vectors(c_rows, w, b):
    d, n = w.shape
    tn = _pick(n, (512, 256, 128))
    return pl.pallas_call(
        _modvec_kernel,
        grid=(n // tn,),
        in_specs=[pl.BlockSpec((SUBLANES, d), lambda j: (0, 0)),
                  pl.BlockSpec((d, tn), lambda j: (0, j)),
                  pl.BlockSpec((1, tn), lambda j: (0, j))],
        out_specs=pl.BlockSpec((SUBLANES, tn), lambda j: (0, j)),
        out_shape=jax.ShapeDtypeStruct((SUBLANES, n), F32),
        compiler_params=_cparams("parallel"),
        name="mod_vectors",
    )(c_rows, w, b.reshape(1, n))


def _ln_mod_kernel(*refs, tb, n_ctx, do_ln, do_mod):
    refs = list(refs)
    r_ref = refs.pop(0)
    x = r_ref[...]
    if do_ln:
        g_ref, b_ref = refs.pop(0), refs.pop(0)
        mu = jnp.mean(x, axis=-1, keepdims=True)
        xc = x - mu
        var = jnp.mean(xc * xc, axis=-1, keepdims=True)
        x = xc * lax.rsqrt(var + LN_EPS) * g_ref[...] + b_ref[...]
    if do_mod:
        sh_ref, sc_ref = refs.pop(0), refs.pop(0)
    if do_ln:
        refs.pop(0)[...] = x
    if do_mod:
        row0 = pl.program_id(0) * tb
        sh = _select_stream(sh_ref, x.shape, row0, n_ctx)
        sc = _select_stream(sc_ref, x.shape, row0, n_ctx)
        refs.pop(0)[...] = (x * (1.0 + sc) + sh).astype(BF16)


def _ln_mod(r, ln, mod, k_shift, n_ctx):
    n, d = r.shape
    do_ln, do_mod = ln is not None, mod is not None
    tb = _pick(n, (256, 128, 64, 32, 16, 8))
    row = pl.BlockSpec((tb, d), lambda i: (i, 0))
    vec = pl.BlockSpec((1, d), lambda i: (0, 0))
    args, in_specs, out_specs, out_shape = [r], [row], [], []
    if do_ln:
        args += [ln[0].reshape(1, d), ln[1].reshape(1, d)]
        in_specs += [vec, vec]
        out_specs.append(row)
        out_shape.append(jax.ShapeDtypeStruct((n, d), F32))
    if do_mod:
        args += [mod, mod]
        in_specs += [pl.BlockSpec((SUBLANES, d), lambda i: (0, k_shift)),
                     pl.BlockSpec((SUBLANES, d), lambda i: (0, k_shift + 1))]
        out_specs.append(row)
        out_shape.append(jax.ShapeDtypeStruct((n, d), BF16))
    outs = pl.pallas_call(
        functools.partial(_ln_mod_kernel, tb=tb, n_ctx=n_ctx, do_ln=do_ln, do_mod=do_mod),
        grid=(n // tb,),
        in_specs=in_specs,
        out_specs=out_specs,
        out_shape=out_shape,
        compiler_params=_cparams("parallel"),
        name="ln_mod",
    )(*args)
    outs = list(outs)
    h = outs.pop(0) if do_ln else r
    u = outs.pop(0) if do_mod else None
    return h, u


def _mm_kernel(*refs, kind, tm, n_ctx, nk):
    if kind == "swiglu":
        x_ref, w1_ref, w3_ref, c_ref, o_ref, acc1, acc3 = refs
    elif kind == "res":
        x_ref, w_ref, h_ref, gt_ref, o_ref, acc1 = refs
    else:
        x_ref, w_ref, o_ref, acc1 = refs
    k = pl.program_id(2)

    @pl.when(k == 0)
    def _():
        acc1[...] = jnp.zeros_like(acc1)
        if kind == "swiglu":
            acc3[...] = jnp.zeros_like(acc3)

    x = x_ref[...]
    if kind == "swiglu":
        acc1[...] += _dot(x, w1_ref[...])
        acc3[...] += _dot(x, w3_ref[...])
    else:
        acc1[...] += _dot(x, w_ref[...])

    @pl.when(k == nk - 1)
    def _():
        if kind == "swiglu":
            tn = o_ref.shape[1]
            for j in range(tn // LANES):
                sl = slice(j * LANES, (j + 1) * LANES)
                y = _silu(acc1[:, sl]) * acc3[:, sl]
                if c_ref is not None:
                    y = y * c_ref[...]
                o_ref[:, sl] = y.astype(o_ref.dtype)
        elif kind == "res":
            row0 = pl.program_id(0) * tm
            gt = _select_stream(gt_ref, acc1.shape, row0, n_ctx)
            o_ref[...] = ALPHA * h_ref[...] + gt * acc1[...]
        else:
            o_ref[...] = acc1[...].astype(o_ref.dtype)


def _mm_tiles(m, n, k):
    tm = _pick(m, (1280, 1024, 512, 256, 128, 64, 32, 16, 8))
    tn = _pick(n, (1024, 1152, 512, 256, 128))
    tk = _pick(k, (2048, 1024, 512, 256, 128))
    return tm, tn, tk


def _mm_plain(x, w, out_dtype):
    m, kd = x.shape
    n = w.shape[1]
    tm, tn, tk = _mm_tiles(m, n, kd)
    nk = kd // tk
    return pl.pallas_call(
        functools.partial(_mm_kernel, kind="plain", tm=tm, n_ctx=0, nk=nk),
        grid=(m // tm, n // tn, nk),
        in_specs=[pl.BlockSpec((tm, tk), lambda i, j, k: (i, k)),
                  pl.BlockSpec((tk, tn), lambda i, j, k: (k, j))],
        out_specs=pl.BlockSpec((tm, tn), lambda i, j, k: (i, j)),
        out_shape=jax.ShapeDtypeStruct((m, n), out_dtype),
        scratch_shapes=[pltpu.VMEM((tm, tn), F32)],
        compiler_params=_cparams("parallel", "parallel", "arbitrary"),
        name="mm_plain",
    )(x, w)


def _mm_res(x, w, h, mod, k_gate, n_ctx):
    m, kd = x.shape
    n = w.shape[1]
    tm, tn, tk = _mm_tiles(m, n, kd)
    nk = kd // tk
    gate_block0 = k_gate * (n // tn)
    return pl.pallas_call(
        functools.partial(_mm_kernel, kind="res", tm=tm, n_ctx=n_ctx, nk=nk),
        grid=(m // tm, n // tn, nk),
        in_specs=[pl.BlockSpec((tm, tk), lambda i, j, k: (i, k)),
                  pl.BlockSpec((tk, tn), lambda i, j, k: (k, j)),
                  pl.BlockSpec((tm, tn), lambda i, j, k: (i, j)),
                  pl.BlockSpec((SUBLANES, tn), lambda i, j, k: (0, gate_block0 + j))],
        out_specs=pl.BlockSpec((tm, tn), lambda i, j, k: (i, j)),
        out_shape=jax.ShapeDtypeStruct((m, n), F32),
        scratch_shapes=[pltpu.VMEM((tm, tn), F32)],
        compiler_params=_cparams("parallel", "parallel", "arbitrary"),
        name="mm_res",
    )(x, w, h, mod)


def _mm_swiglu(x, w1, w3, cexp=None):
    m, kd = x.shape
    stacked = w1.ndim == 3
    f = w1.shape[-1]
    n_exp = w1.shape[0] if stacked else 1
    tm, tn, tk = _mm_tiles(m, f, kd)
    nk = kd // tk
    nf = f // tn
    if stacked:
        w_spec = pl.BlockSpec((None, tk, tn), lambda i, j, k: (j // nf, k, j % nf))
    else:
        w_spec = pl.BlockSpec((tk, tn), lambda i, j, k: (k, j))
    in_specs = [pl.BlockSpec((tm, tk), lambda i, j, k: (i, k)), w_spec, w_spec]
    args = [x, w1, w3]
    if cexp is not None:
        in_specs.append(pl.BlockSpec((None, tm, LANES), lambda i, j, k: (j // nf, i, 0)))
        args.append(cexp)

    def body(*refs):
        if cexp is None:
            refs = refs[:3] + (None,) + refs[3:]
        _mm_kernel(*refs, kind="swiglu", tm=tm, n_ctx=0, nk=nk)

    return pl.pallas_call(
        body,
        grid=(m // tm, n_exp * nf, nk),
        in_specs=in_specs,
        out_specs=pl.BlockSpec((tm, tn), lambda i, j, k: (i, j)),
        out_shape=jax.ShapeDtypeStruct((m, n_exp * f), BF16),
        scratch_shapes=[pltpu.VMEM((tm, tn), F32), pltpu.VMEM((tm, tn), F32)],
        compiler_params=_cparams("parallel", "parallel", "arbitrary"),
        name="mm_swiglu",
    )(*args)


def _split3(x):
    hi = x.astype(BF16)
    r1 = x - hi.astype(F32)
    mid = r1.astype(BF16)
    lo = (r1 - mid.astype(F32)).astype(BF16)
    return hi, mid, lo


def _head_seg_matrix(width):
    r = lax.broadcasted_iota(jnp.int32, (LANES, LANES), 0) // width
    c = lax.broadcasted_iota(jnp.int32, (LANES, LANES), 1) // width
    return (r == c).astype(BF16)


def _segsum_tile(x, seg):
    hi, mid, lo = _split3(x)
    return _dot(hi, seg) + _dot(mid, seg) + _dot(lo, seg)


def _rwkv_prep_kernel(p_ref, prev_ref, next_ref, mu_ref, w0_ref, w2_ref, a0_ref, a2_ref, g2_ref,
                      kk_ref, ka_ref, rk_ref,
                      r_out, v_out, kap_out, g_out, bonus_out, lw_out, b_out, kd_out,
                      *, tb, n_ctx, n_rows, width, win_w, win_a, o_gl):
    x = p_ref[...]
    row = pl.program_id(0) * tb + lax.broadcasted_iota(jnp.int32, (tb, 1), 0)
    local = lax.broadcasted_iota(jnp.int32, (tb, 1), 0)
    prev = jnp.where(local == 0, prev_ref[...], pltpu.roll(x, 1, 0))
    nxt = jnp.where(local == tb - 1, next_ref[...], pltpu.roll(x, tb - 1, 0))
    prev = jnp.where((row == 0) | (row == n_ctx), 0.0, prev)
    nxt = jnp.where((row == n_ctx - 1) | (row == n_rows - 1), 0.0, nxt)
    x = x + mu_ref[...] * (0.5 * (prev + nxt) - x)

    w = width
    r = x[:, 0:w]
    k = x[:, w:2 * w]
    v = x[:, 2 * w:3 * w]
    gl = x[:, o_gl:o_gl + A_GATE_LORA]
    wl = x[:, win_w[0]:win_w[0] + win_w[1]]
    al = x[:, win_a[0]:win_a[0] + win_a[1]]

    g_out[...] = _dot(_sigmoid(gl).astype(BF16), g2_ref[...])
    lw_lora = _dot(jnp.tanh(wl).astype(BF16), w2_ref[...])
    a_lora = _dot(al.astype(BF16), a2_ref[...])

    kk = k * kk_ref[...]
    seg = _head_seg_matrix(A_HEAD)
    rk = rk_ref[...]
    ka = ka_ref[...]
    kd_sum = jnp.zeros_like(k)
    kaps = []
    for j in range(w // LANES):
        sl = slice(j * LANES, (j + 1) * LANES)
        t = kk[:, sl]
        n2 = _segsum_tile(t * t, seg)
        kaps.append(t / jnp.maximum(jnp.sqrt(n2), 1e-12))
    kap = jnp.concatenate(kaps, axis=-1)
    for d in range(2):
        z = w0_ref[d:d + 1, :] + lw_lora[:, d * w:(d + 1) * w]
        softplus = jnp.maximum(-z, 0.0) + jnp.log(1.0 + jnp.exp(-jnp.abs(z)))
        wlog = -softplus - 0.5
        lw_out[d] = -jnp.exp(wlog)
        a = _sigmoid(a0_ref[d:d + 1, :] + a_lora[:, d * w:(d + 1) * w])
        b_out[d] = kap * a
        kd = k * (1.0 + (a - 1.0) * ka)
        kd_out[d] = kd
        kd_sum = kd_sum + kd
    bon = r * kd_sum * rk
    bons = []
    for j in range(w // LANES):
        sl = slice(j * LANES, (j + 1) * LANES)
        bons.append(_segsum_tile(bon[:, sl], seg))
    bonus_out[...] = jnp.concatenate(bons, axis=-1) * v
    r_out[...] = r
    v_out[...] = v
    kap_out[...] = kap


def _aligned_window(offset, width, limit):
    start = offset // LANES * LANES
    size = min(_round_up(offset + width - start, LANES), limit - start)
    return start, size


def _window_weight(blocks, offset, window, width):
    start, size = window
    lora = blocks.shape[1]
    wt = jnp.zeros((size, 2 * width), F32)
    for d in range(2):
        r0 = offset - start + d * lora
        wt = wt.at[r0:r0 + lora, d * width:(d + 1) * width].set(blocks[d])
    return wt.astype(BF16)


def _rwkv_prepare(pa, n_ctx, a_in, mu, w0, w2, a0, a2, g2, k_k, k_a, r_k):
    n = pa.shape[0]
    w = g2.shape[1]
    tb = _pick(n_ctx if n_ctx else n, (64, 32, 16, 8))
    nblk = n // tb
    o_gl = 3 * w
    o_wl = o_gl + A_GATE_LORA
    o_al = o_wl + 2 * A_DECAY_LORA
    win_w = _aligned_window(o_wl, 2 * A_DECAY_LORA, a_in)
    win_a = _aligned_window(o_al, 2 * A_AAA_LORA, a_in)
    w2w = _window_weight(w2, o_wl, win_w, w)
    a2w = _window_weight(a2, o_al, win_a, w)
    zero = jnp.zeros((1, a_in), F32)
    last_rows = pa[tb - 1::tb, :a_in]
    first_rows = pa[0::tb, :a_in]
    prev_rows = jnp.concatenate([zero, last_rows[:-1]], axis=0).reshape(nblk, 1, a_in)
    next_rows = jnp.concatenate([first_rows[1:], zero], axis=0).reshape(nblk, 1, a_in)

    tok = pl.BlockSpec((tb, w), lambda i: (i, 0))
    tok2 = pl.BlockSpec((2, tb, w), lambda i: (0, i, 0))
    halo = pl.BlockSpec((None, 1, a_in), lambda i: (i, 0, 0))

    def full(a):
        return pl.BlockSpec(a.shape, lambda i: (0,) * a.ndim)

    params = [mu.reshape(1, a_in), w0, w2w, a0, a2w, g2.astype(BF16), k_k.reshape(1, w), k_a.reshape(1, w),
              r_k.reshape(1, w)]
    one = jax.ShapeDtypeStruct((n, w), F32)
    two = jax.ShapeDtypeStruct((2, n, w), F32)
    return pl.pallas_call(
        functools.partial(_rwkv_prep_kernel, tb=tb, n_ctx=n_ctx, n_rows=n, width=w, win_w=win_w, win_a=win_a,
                          o_gl=o_gl),
        grid=(nblk,),
        in_specs=[pl.BlockSpec((tb, a_in), lambda i: (i, 0)), halo, halo] + [full(a) for a in params],
        out_specs=[tok, tok, tok, tok, tok, tok2, tok2, tok2],
        out_shape=[one, one, one, one, one, two, two, two],
        compiler_params=_cparams("parallel"),
        name="rwkv_prepare",
    )(pa, prev_rows, next_rows, *params)


def _chunk_order(d, c, n_ctx_chunks, n_chunks):
    back = jnp.where(c < n_ctx_chunks, n_ctx_chunks - 1 - c, n_chunks - 1 - (c - n_ctx_chunks))
    return jnp.where(d == 0, c, back)


def _rwkv_scan_kernel(r_ref, kap_ref, v_ref, lw_ref, b_ref, k_ref, y_ref, s_ref):
    c_len = r_ref.shape[0]
    d = pl.program_id(0)
    rev = d == 1

    @pl.when(pl.program_id(2) == 0)
    def _():
        s_ref[...] = jnp.zeros_like(s_ref)

    def mm(a, b):
        return jnp.dot(a, b, preferred_element_type=F32, precision=HIGHEST)

    r, kap, v, lw, b, k = (ref[...] for ref in (r_ref, kap_ref, v_ref, lw_ref, b_ref, k_ref))
    ti = lax.broadcasted_iota(jnp.int32, (c_len, c_len), 0)
    si = lax.broadcasted_iota(jnp.int32, (c_len, c_len), 1)
    sign = jnp.where(rev, -1, 1)
    upto = ((ti - si) * sign >= 0).astype(F32)
    cl_in = mm(upto, lw)
    cl_tot = jnp.sum(lw, axis=0, keepdims=True)
    g_in = jnp.exp(cl_in)
    g_ex = jnp.exp(cl_in - lw)
    g_inv = jnp.exp(-cl_in)
    g_end = jnp.exp(cl_tot - cl_in)

    lane = lax.broadcasted_iota(jnp.int32, (c_len, PAIR), 1)
    head0 = lane < A_HEAD

    def stack(x):
        return jnp.concatenate([jnp.where(head0, x, 0.0), jnp.where(head0, 0.0, x)], axis=0)

    def dup(x):
        return jnp.concatenate([x, x], axis=0)

    ks = stack(kap * g_ex)
    rs = stack(r * g_in)
    vs = stack(v)
    bhs = stack(b * g_end)
    khs = stack(k * g_end)
    bd_ = dup(b * g_inv)
    kd_ = dup(k * g_inv)

    n2 = 2 * c_len
    row = lax.broadcasted_iota(jnp.int32, (n2, n2), 0)
    col = lax.broadcasted_iota(jnp.int32, (n2, n2), 1)
    same = (row // c_len) == (col // c_len)
    t = row % c_len
    s = col % c_len
    strict = same & ((t - s) * sign > 0)
    incl = same & ((t - s) * sign >= 0)
    eye = row == col

    m_ = jnp.where(strict, _dot_nt(ks, bd_, HIGHEST), 0.0)
    n_ = jnp.where(strict, _dot_nt(ks, kd_, HIGHEST), 0.0)
    lb = jnp.where(incl, _dot_nt(rs, bd_, HIGHEST), 0.0)
    lk = jnp.where(incl, _dot_nt(rs, kd_, HIGHEST), 0.0)

    x = -m_
    tinv = jnp.where(eye, 1.0, 0.0) + x
    p = x
    span = 2
    while span < c_len:
        p = mm(p, p)
        tinv = tinv + mm(tinv, p)
        span *= 2

    kq = mm(tinv, ks)
    u = mm(tinv, mm(n_, vs))
    bhs_t = bhs.T
    a_ = jnp.where(eye, jnp.exp(cl_tot), 0.0) - mm(bhs_t, kq)
    g_ = mm(khs.T, vs) - mm(bhs_t, u)
    rq = rs - mm(lb, kq)
    yi = mm(lk, vs) - mm(lb, u)

    s0 = s_ref[...]
    ys = mm(rq, s0) + yi
    y_ref[...] = ys[:c_len] + ys[c_len:]
    s_ref[...] = mm(a_, s0) + g_


def _rwkv_scan(r, kap, v, lw, b, kd, n_ctx):
    n, w = r.shape
    c_len = RWKV_CHUNK
    n_chunks = n // c_len
    n_ctx_chunks = n_ctx // c_len

    def tok(d, p, c):
        return (_chunk_order(d, c, n_ctx_chunks, n_chunks), p)

    def tok2(d, p, c):
        return (d, _chunk_order(d, c, n_ctx_chunks, n_chunks), p)

    one = pl.BlockSpec((c_len, PAIR), tok)
    two = pl.BlockSpec((None, c_len, PAIR), tok2)
    return pl.pallas_call(
        _rwkv_scan_kernel,
        grid=(2, w // PAIR, n_chunks),
        in_specs=[one, one, one, two, two, two],
        out_specs=two,
        out_shape=jax.ShapeDtypeStruct((2, n, w), F32),
        scratch_shapes=[pltpu.VMEM((PAIR, PAIR), F32)],
        compiler_params=_cparams("parallel", "parallel", "arbitrary"),
        name="rwkv_scan",
    )(r, kap, v, lw, b, kd)


def _rope_tables(n_ctx, n_lat, half):
    nq = half // 2
    pos = jnp.arange(n_lat, dtype=jnp.int32)
    row = (pos // GRID_W).astype(F32)
    col = (pos % GRID_W).astype(F32)
    inv = ROPE_THETA ** (-jnp.arange(nq, dtype=F32) / nq)

    def part(p):
        ang = p[:, None] * inv[None, :]
        c, s = jnp.cos(ang), jnp.sin(ang)
        return jnp.concatenate([c, c], -1), jnp.concatenate([-s, s], -1)

    cr, sr = part(row)
    cc, sc = part(col)
    cos = jnp.concatenate([cr, cc], -1)
    sin = jnp.concatenate([sr, sc], -1)
    cos = jnp.concatenate([jnp.ones((n_ctx, 2 * half), F32), cos], 0)
    sin = jnp.concatenate([jnp.zeros((n_ctx, 2 * half), F32), sin], 0)
    return cos, sin


def _rope_tile(x, cos, sin, quarter):
    if 2 * quarter == LANES:
        partner = pltpu.roll(x, quarter, 1)
    else:
        lane = lax.broadcasted_iota(jnp.int32, x.shape, 1)
        lower = (lane % (2 * quarter)) < quarter
        partner = jnp.where(lower, pltpu.roll(x, LANES - quarter, 1), pltpu.roll(x, quarter, 1))
    return x * cos + partner * sin


def _retention_kernel(q_ref, k_ref, v_ref, cos_ref, sin_ref, lg_ref, y_ref, s_ref):
    c_len = q_ref.shape[0]
    d = pl.program_id(0)
    rev = d == 1

    @pl.when(pl.program_id(2) == 0)
    def _():
        s_ref[...] = jnp.zeros_like(s_ref)

    cos = cos_ref[...]
    sin = sin_ref[...]

    def rope(x):
        tiles = [_rope_tile(x[:, j * LANES:(j + 1) * LANES], cos[:, j * LANES:(j + 1) * LANES],
                            sin[:, j * LANES:(j + 1) * LANES], LANES // 2) for j in range(B_HEAD // LANES)]
        return jnp.concatenate(tiles, axis=-1)

    q = rope(q_ref[...])
    k = rope(k_ref[...]) * (B_HEAD ** -0.5)
    v = v_ref[...]
    lg = jnp.concatenate([lg_ref[0:1, :]] * (B_HEAD // LANES), axis=-1)
    lg_c = jnp.concatenate([lg_ref[0:1, :]] * (c_len // LANES), axis=-1)

    n_i = lax.broadcasted_iota(jnp.int32, (c_len, c_len), 0)
    m_i = lax.broadcasted_iota(jnp.int32, (c_len, c_len), 1)
    dist = (n_i - m_i) * jnp.where(rev, -1, 1)
    seen = dist >= jnp.where(rev, 1, 0)
    decay = jnp.where(seen, jnp.exp(lg_c * jnp.maximum(dist, 0).astype(F32)), 0.0)
    idx = lax.broadcasted_iota(jnp.int32, (c_len, B_HEAD), 0)
    order = jnp.where(rev, c_len - 1 - idx, idx).astype(F32)
    q_dec = jnp.exp(lg * (order + 1.0))
    k_dec = jnp.exp(lg * (c_len - 1.0 - order))
    chunk_dec = jnp.exp(lg * float(c_len))

    inner = _dot_nt(q.astype(BF16), k.astype(BF16)) * decay
    s0 = s_ref[...]
    y = _dot(inner.astype(BF16), v.astype(BF16)) + _dot((q * q_dec).astype(BF16), s0.astype(BF16))
    y_ref[...] = y
    s_ref[...] = s0 * chunk_dec + _dot((k * k_dec).T.astype(BF16), v.astype(BF16))


def _retention(pb, n_ctx, cos, sin):
    n = pb.shape[0]
    width = pb.shape[1] // 4
    heads = width // B_HEAD
    c_len = RET_CHUNK
    n_chunks = n // c_len
    n_ctx_chunks = n_ctx // c_len
    log_g = jnp.log1p(-jnp.exp2(-5.0 - jnp.arange(heads, dtype=F32)))
    lg = jnp.concatenate([log_g, log_g[::-1]])
    lg = jnp.broadcast_to(lg[:, None, None], (2 * heads, SUBLANES, LANES))

    def part(which):
        return pl.BlockSpec((c_len, B_HEAD),
                            lambda d, h, c: (_chunk_order(d, c, n_ctx_chunks, n_chunks), which * heads + h))

    tab = pl.BlockSpec((c_len, B_HEAD), lambda d, h, c: (_chunk_order(d, c, n_ctx_chunks, n_chunks), 0))
    return pl.pallas_call(
        _retention_kernel,
        grid=(2, heads, n_chunks),
        in_specs=[part(0), part(1), part(2), tab, tab,
                  pl.BlockSpec((None, SUBLANES, LANES), lambda d, h, c: (d * heads + h, 0, 0))],
        out_specs=pl.BlockSpec((None, c_len, B_HEAD),
                               lambda d, h, c: (d, _chunk_order(d, c, n_ctx_chunks, n_chunks), h)),
        out_shape=jax.ShapeDtypeStruct((2, n, width), F32),
        scratch_shapes=[pltpu.VMEM((B_HEAD, B_HEAD), F32)],
        compiler_params=_cparams("parallel", "parallel", "arbitrary"),
        name="retention",
    )(pb, pb, pb, cos, sin, lg)


def _mix_finish_kernel(ya_ref, bonus_ref, g_ref, yb_ref, gate_ref, lg_ref, lb_ref, gg_ref, gb_ref, o_ref):
    wa = bonus_ref.shape[1]
    ya = ya_ref[0] + ya_ref[1]
    seg = _head_seg_matrix(A_HEAD)
    inv = 1.0 / A_HEAD
    for j in range(wa // LANES):
        sl = slice(j * LANES, (j + 1) * LANES)
        t = ya[:, sl]
        mu = _segsum_tile(t, seg) * inv
        tc = t - mu
        var = _segsum_tile(tc * tc, seg) * inv
        o = tc * lax.rsqrt(var + A_GN_EPS) * lg_ref[:, sl] + lb_ref[:, sl] + bonus_ref[:, sl]
        o_ref[:, sl] = (o * g_ref[:, sl]).astype(o_ref.dtype)
    yb = yb_ref[0] + yb_ref[1]
    wb = yb.shape[1]
    for j in range(wb // B_HEAD):
        sl = slice(j * B_HEAD, (j + 1) * B_HEAD)
        t = yb[:, sl]
        mu = jnp.mean(t, axis=-1, keepdims=True)
        tc = t - mu
        var = jnp.mean(tc * tc, axis=-1, keepdims=True)
        o = tc * lax.rsqrt(var + B_GN_EPS) * gg_ref[:, sl] + gb_ref[:, sl]
        o_ref[:, wa + j * B_HEAD:wa + (j + 1) * B_HEAD] = (o * _silu(gate_ref[:, sl])).astype(o_ref.dtype)


def _mix_finish(ya, bonus, g, yb, pb, lnx_g, lnx_b, gn_g, gn_b):
    n, wa = bonus.shape
    wb = yb.shape[2]
    tb = _pick(n, (128, 64, 32, 16, 8))
    ra = pl.BlockSpec((tb, wa), lambda i: (i, 0))
    rb = pl.BlockSpec((tb, wb), lambda i: (i, 3))
    va = pl.BlockSpec((1, wa), lambda i: (0, 0))
    vb = pl.BlockSpec((1, wb), lambda i: (0, 0))
    return pl.pallas_call(
        _mix_finish_kernel,
        grid=(n // tb,),
        in_specs=[pl.BlockSpec((2, tb, wa), lambda i: (0, i, 0)), ra, ra,
                  pl.BlockSpec((2, tb, wb), lambda i: (0, i, 0)), rb, va, va, vb, vb],
        out_specs=pl.BlockSpec((tb, wa + wb), lambda i: (i, 0)),
        out_shape=jax.ShapeDtypeStruct((n, wa + wb), BF16),
        compiler_params=_cparams("parallel"),
        name="mix_finish",
    )(ya, bonus, g, yb, pb, lnx_g.reshape(1, wa), lnx_b.reshape(1, wa), gn_g.reshape(1, wb), gn_b.reshape(1, wb))


def _qkv_prep_kernel(x_ref, cos_ref, sin_ref, qg_ref, kg_ref, q_ref, k_ref, v_ref, *, q_heads, kv_heads):
    cos = cos_ref[...]
    sin = sin_ref[...]

    def norm_rope(t, g):
        t = t * lax.rsqrt(jnp.mean(t * t, axis=-1, keepdims=True) + QK_EPS) * g
        return _rope_tile(t, cos, sin, C_HEAD // 4)

    scale = C_HEAD ** -0.5
    for h in range(q_heads):
        sl = slice(h * C_HEAD, (h + 1) * C_HEAD)
        q_ref[:, sl] = (norm_rope(x_ref[:, sl], qg_ref[...]) * scale).astype(BF16)
    for h in range(kv_heads):
        sl = slice(h * C_HEAD, (h + 1) * C_HEAD)
        src = slice((q_heads + h) * C_HEAD, (q_heads + h + 1) * C_HEAD)
        k_ref[:, sl] = norm_rope(x_ref[:, src], kg_ref[...]).astype(BF16)
    v0 = (q_heads + kv_heads) * C_HEAD
    v_ref[...] = x_ref[:, v0:v0 + kv_heads * C_HEAD].astype(BF16)


def _qkv_prep(qkv, cos, sin, qn_g, kn_g, q_heads, kv_heads):
    n, c_in = qkv.shape
    tb = _pick(n, (128, 64, 32, 16, 8))
    kvw = kv_heads * C_HEAD
    return pl.pallas_call(
        functools.partial(_qkv_prep_kernel, q_heads=q_heads, kv_heads=kv_heads),
        grid=(n // tb,),
        in_specs=[pl.BlockSpec((tb, c_in), lambda i: (i, 0)),
                  pl.BlockSpec((tb, C_HEAD), lambda i: (i, 0)),
                  pl.BlockSpec((tb, C_HEAD), lambda i: (i, 0)),
                  pl.BlockSpec((1, C_HEAD), lambda i: (0, 0)),
                  pl.BlockSpec((1, C_HEAD), lambda i: (0, 0))],
        out_specs=[pl.BlockSpec((tb, q_heads * C_HEAD), lambda i: (i, 0)),
                   pl.BlockSpec((tb, kvw), lambda i: (i, 0)),
                   pl.BlockSpec((tb, kvw), lambda i: (i, 0))],
        out_shape=[jax.ShapeDtypeStruct((n, q_heads * C_HEAD), BF16),
                   jax.ShapeDtypeStruct((n, kvw), BF16),
                   jax.ShapeDtypeStruct((n, kvw), BF16)],
        compiler_params=_cparams("parallel"),
        name="qkv_prep",
    )(qkv, cos, sin, qn_g.reshape(1, C_HEAD), kn_g.reshape(1, C_HEAD))


def _flash_kernel(q_ref, k_ref, v_ref, o_ref, *, tq, tk, n_kv):
    q = jnp.concatenate([q_ref[:, h * C_HEAD:(h + 1) * C_HEAD] for h in range(C_GROUP)], axis=0)
    rows = C_GROUP * tq

    def step(j, carry):
        m, l, acc = carry
        kj = k_ref[pl.ds(pl.multiple_of(j * tk, tk), tk), :]
        vj = v_ref[pl.ds(pl.multiple_of(j * tk, tk), tk), :]
        s = _dot_nt(q, kj)
        m_new = jnp.maximum(m, jnp.max(s, axis=-1, keepdims=True))
        corr = jnp.exp(m - m_new)
        p = jnp.exp(s - m_new)
        l = corr * l + jnp.sum(p, axis=-1, keepdims=True)
        acc = corr * acc + _dot(p.astype(BF16), vj)
        return m_new, l, acc

    init = (jnp.full((rows, 1), -jnp.inf, F32), jnp.zeros((rows, 1), F32), jnp.zeros((rows, C_HEAD), F32))
    m, l, acc = lax.fori_loop(0, n_kv // tk, step, init)
    out = acc / l
    for h in range(C_GROUP):
        o_ref[:, h * C_HEAD:(h + 1) * C_HEAD] = out[h * tq:(h + 1) * tq].astype(o_ref.dtype)


def _flash(q, k, v, n_ctx):
    n = q.shape[0]
    kv_heads = k.shape[1] // C_HEAD
    n_lat = n - n_ctx
    tq = _pick(n_ctx, (256, 128, 64, 32, 16, 8))
    tk = _pick(n, (1280, 1024, 640, 512, 256, 128, 64, 32, 16, 8))
    gw = C_GROUP * C_HEAD
    q0 = n_ctx // tq
    return pl.pallas_call(
        functools.partial(_flash_kernel, tq=tq, tk=tk, n_kv=n),
        grid=(kv_heads, n_lat // tq),
        in_specs=[pl.BlockSpec((tq, gw), lambda g, i: (q0 + i, g)),
                  pl.BlockSpec((n, C_HEAD), lambda g, i: (0, g)),
                  pl.BlockSpec((n, C_HEAD), lambda g, i: (0, g))],
        out_specs=pl.BlockSpec((tq, gw), lambda g, i: (i, g)),
        out_shape=jax.ShapeDtypeStruct((n_lat, kv_heads * gw), BF16),
        compiler_params=_cparams("parallel", "parallel"),
        name="flash_gqa",
    )(q, k, v)


def _router_kernel(u_ref, w_ref, c_ref, *, n_exp):
    logits = jnp.dot(u_ref[...].astype(F32), w_ref[...], preferred_element_type=F32, precision=HIGHEST)
    lane = lax.broadcasted_iota(jnp.int32, logits.shape, 1)
    valid = lane < n_exp
    neg = jnp.float32(-jnp.inf)
    l1 = jnp.where(valid, logits, neg)
    m1 = jnp.max(l1, axis=-1, keepdims=True)
    i1 = jnp.min(jnp.where(l1 == m1, lane, LANES), axis=-1, keepdims=True)
    l2 = jnp.where(lane == i1, neg, l1)
    m2 = jnp.max(l2, axis=-1, keepdims=True)
    i2 = jnp.min(jnp.where(l2 == m2, lane, LANES), axis=-1, keepdims=True)
    e2 = jnp.exp(m2 - m1)
    g1 = 1.0 / (1.0 + e2)
    g2 = e2 / (1.0 + e2)
    c_ref[...] = jnp.where(lane == i1, g1, 0.0) + jnp.where(lane == i2, g2, 0.0)


def _router(h, mod, k_shift, router):
    m, d = h.shape
    n_exp = router.shape[1]
    w = jnp.zeros((d, LANES), F32).at[:, :n_exp].set(router)
    tb = _pick(m, (256, 128, 64, 32, 16, 8))

    def body(h_ref, sh_ref, sc_ref, w_ref, c_ref, u_scr):
        u_scr[...] = h_ref[...] * (1.0 + sc_ref[0:1, :]) + sh_ref[0:1, :]
        _router_kernel(u_scr, w_ref, c_ref, n_exp=n_exp)

    return pl.pallas_call(
        body,
        grid=(m // tb,),
        in_specs=[pl.BlockSpec((tb, d), lambda i: (i, 0)),
                  pl.BlockSpec((SUBLANES, d), lambda i: (0, k_shift)),
                  pl.BlockSpec((SUBLANES, d), lambda i: (0, k_shift + 1)),
                  pl.BlockSpec((d, LANES), lambda i: (0, 0))],
        out_specs=pl.BlockSpec((tb, LANES), lambda i: (i, 0)),
        out_shape=jax.ShapeDtypeStruct((m, LANES), F32),
        scratch_shapes=[pltpu.VMEM((tb, d), F32)],
        compiler_params=_cparams("parallel"),
        name="router_top2",
    )(h, mod, mod, w)


def _pad_cols(w, n):
    return jnp.pad(w, ((0, 0), (0, n - w.shape[1])))


def kernel(x, c, ctx, c_ctx, l0_mod_w, l0_mod_b, l0_w_in, l0_a_mu, l0_a_w0, l0_a_w2, l0_a_a0, l0_a_a2, l0_a_g2, l0_a_k_k, l0_a_k_a, l0_a_r_k, l0_a_lnx_g, l0_a_lnx_b, l0_b_gn_g, l0_b_gn_b, l0_w_out, l0_ln1_g, l0_ln1_b, l0_ffn_w1, l0_ffn_w3, l0_ffn_w2, l0_ln2_g, l0_ln2_b, l1_mod_w, l1_mod_b, l1_w_qkv, l1_q_norm_g, l1_k_norm_g, l1_w_out, l1_ln1_g, l1_ln1_b, l1_router, l1_moe_w1, l1_moe_w3, l1_moe_w2, l1_ln2_g, l1_ln2_b):
    batch, n_lat, d = x.shape
    n_ctx = ctx.shape[1]
    assert batch == 1 and c.shape[0] == 1
    a_width = l0_a_g2.shape[1]
    a_in = l0_a_mu.shape[0]
    b_width = l0_b_gn_g.shape[0]

    c_rows = jnp.zeros((SUBLANES, d), F32).at[0].set(c[0]).at[1].set(c_ctx)
    mod0 = _mod_vectors(c_rows, l0_mod_w, l0_mod_b)
    mod1 = _mod_vectors(c_rows, l1_mod_w, l1_mod_b)

    h = jnp.concatenate([ctx[0], x[0]], axis=0)

    _, u = _ln_mod(h, None, mod0, 0, n_ctx)
    a_pad = _round_up(a_in, 1024) if a_in > 1024 else a_in
    w_in_a = _pad_cols(l0_w_in[:, :a_in], a_pad).astype(BF16)
    w_in_b = l0_w_in[:, a_in:].astype(BF16)
    pa = _mm_plain(u, w_in_a, F32)
    pb = _mm_plain(u, w_in_b, F32)
    r, v, kap, g, bonus, lw, b, kd = _rwkv_prepare(pa, n_ctx, a_in, l0_a_mu, l0_a_w0, l0_a_w2, l0_a_a0, l0_a_a2,
                                                   l0_a_g2, l0_a_k_k, l0_a_k_a, l0_a_r_k)
    ya = _rwkv_scan(r, kap, v, lw, b, kd, n_ctx)
    cos_b, sin_b = _rope_tables(n_ctx, n_lat, B_HEAD // 2)
    yb = _retention(pb, n_ctx, cos_b, sin_b)
    mix = _mix_finish(ya, bonus, g, yb, pb, l0_a_lnx_g, l0_a_lnx_b, l0_b_gn_g, l0_b_gn_b)
    res = _mm_res(mix, l0_w_out.astype(BF16), h, mod0, 2, n_ctx)
    h, u = _ln_mod(res, (l0_ln1_g, l0_ln1_b), mod0, 3, n_ctx)
    d_ff = l0_ffn_w1.shape[1]
    ff_pad = _round_up(d_ff, 1024) if d_ff > 1024 else d_ff
    hid = _mm_swiglu(u, _pad_cols(l0_ffn_w1, ff_pad).astype(BF16), _pad_cols(l0_ffn_w3, ff_pad).astype(BF16))
    w2 = jnp.pad(l0_ffn_w2, ((0, ff_pad - d_ff), (0, 0))).astype(BF16)
    res = _mm_res(hid, w2, h, mod0, 5, n_ctx)
    h, u = _ln_mod(res, (l0_ln2_g, l0_ln2_b), mod1, 0, n_ctx)

    q_heads = l1_w_out.shape[0] // C_HEAD
    kv_heads = q_heads // C_GROUP
    qkv = _mm_plain(u, l1_w_qkv.astype(BF16), F32)
    cos_c, sin_c = _rope_tables(n_ctx, n_lat, C_HEAD // 2)
    q, k, vv = _qkv_prep(qkv, cos_c, sin_c, l1_q_norm_g, l1_k_norm_g, q_heads, kv_heads)
    att = _flash(q, k, vv, n_ctx)
    h_lat = h[n_ctx:]
    res = _mm_res(att, l1_w_out.astype(BF16), h_lat, mod1, 2, 0)
    h_lat, u = _ln_mod(res, (l1_ln1_g, l1_ln1_b), mod1, 3, 0)
    comb = _router(h_lat, mod1, 3, l1_router)
    n_exp = l1_router.shape[1]
    cexp = jnp.broadcast_to(comb[:, :n_exp].T[:, :, None], (n_exp, n_lat, LANES))
    hid = _mm_swiglu(u, l1_moe_w1.astype(BF16), l1_moe_w3.astype(BF16), cexp)
    w2 = l1_moe_w2.reshape(n_exp * l1_moe_w2.shape[1], d).astype(BF16)
    res = _mm_res(hid, w2, h_lat, mod1, 5, 0)
    out, _ = _ln_mod(res, (l1_ln2_g, l1_ln2_b), None, 0, 0)
    return out[None]
```

```python
import functools

import jax
import jax.numpy as jnp
from jax import lax
from jax.experimental import pallas as pl
from jax.experimental.pallas import tpu as pltpu

F32 = jnp.float32
BF16 = jnp.bfloat16
HIGHEST = lax.Precision.HIGHEST

GRID_W = 64
DEPTH = 2
ALPHA = (2 * DEPTH) ** 0.25
LN_EPS = 1e-6
ROPE_THETA = 10000.0
A_HEAD = 64
A_DECAY_LORA = 96
A_AAA_LORA = 96
A_GATE_LORA = 256
A_GN_EPS = 64e-5
B_HEAD = 256
RET_CHUNK = 128
B_GN_EPS = 1e-5
C_HEAD = 128
C_GROUP = 4
QK_EPS = 1e-6
TOP_K = 2
LOG2_E = 1.4426950408889634

LANES = 128
SUBLANES = 8
VMEM_LIMIT_BYTES = 60 * 1024 * 1024

RWKV_CHUNK = 64
PAIR = 2 * A_HEAD


def _cparams(*sem):
    return pltpu.CompilerParams(dimension_semantics=sem, vmem_limit_bytes=VMEM_LIMIT_BYTES)


def _pick(n, candidates):
    for c in candidates:
        if c <= n and n % c == 0:
            return c
    return n


def _round_up(n, m):
    return (n + m - 1) // m * m


def _silu(x):
    return x * (1.0 / (1.0 + jnp.exp(-x)))


def _sigmoid(x):
    return 1.0 / (1.0 + jnp.exp(-x))


def _dot(a, b):
    return jnp.dot(a, b, preferred_element_type=F32)


def _dot_nt(a, b, precision=None):
    return lax.dot_general(a, b, (((1,), (1,)), ((), ())), preferred_element_type=F32, precision=precision)


def _row_is_ctx(shape, row0, n_ctx):
    rows = row0 + lax.broadcasted_iota(jnp.int32, shape, 0)
    return rows < n_ctx


def _select_stream(mod_ref, shape, row0, n_ctx):
    lat = mod_ref[0:1, :]
    if n_ctx == 0:
        return jnp.broadcast_to(lat, shape)
    return jnp.where(_row_is_ctx(shape, row0, n_ctx), mod_ref[1:2, :], lat)


def _modvec_kernel(c_ref, w_ref, b_ref, o_ref):
    s = _silu(c_ref[...])
    o_ref[...] = jnp.dot(s, w_ref[...], preferred_element_type=F32, precision=HIGHEST) + b_ref[...]


def _mod_vectors(c_rows, w, b):
    d, n = w.shape
    tn = _pick(n, (512, 256, 128))
    return pl.pallas_call(
        _modvec_kernel,
        grid=(n // tn,),
        in_specs=[pl.BlockSpec((SUBLANES, d), lambda j: (0, 0)),
                  pl.BlockSpec((d, tn), lambda j: (0, j)),
                  pl.BlockSpec((1, tn), lambda j: (0, j))],
        out_specs=pl.BlockSpec((SUBLANES, tn), lambda j: (0, j)),
        out_shape=jax.ShapeDtypeStruct((SUBLANES, n), F32),
        compiler_params=_cparams("parallel"),
        name="mod_vectors",
    )(c_rows, w, b.reshape(1, n))


def _ln_mod_kernel(*refs, tb, n_ctx, do_ln, do_mod):
    refs = list(refs)
    r_ref = refs.pop(0)
    x = r_ref[...]
    if do_ln:
        g_ref, b_ref = refs.pop(0), refs.pop(0)
        mu = jnp.mean(x, axis=-1, keepdims=True)
        xc = x - mu
        var = jnp.mean(xc * xc, axis=-1, keepdims=True)
        x = xc * lax.rsqrt(var + LN_EPS) * g_ref[...] + b_ref[...]
    if do_mod:
        sh_ref, sc_ref = refs.pop(0), refs.pop(0)
    if do_ln:
        refs.pop(0)[...] = x
    if do_mod:
        row0 = pl.program_id(0) * tb
        sh = _select_stream(sh_ref, x.shape, row0, n_ctx)
        sc = _select_stream(sc_ref, x.shape, row0, n_ctx)
        refs.pop(0)[...] = (x * (1.0 + sc) + sh).astype(BF16)


def _ln_mod(r, ln, mod, k_shift, n_ctx):
    n, d = r.shape
    do_ln, do_mod = ln is not None, mod is not None
    tb = _pick(n, (256, 128, 64, 32, 16, 8))
    row = pl.BlockSpec((tb, d), lambda i: (i, 0))
    vec = pl.BlockSpec((1, d), lambda i: (0, 0))
    args, in_specs, out_specs, out_shape = [r], [row], [], []
    if do_ln:
        args += [ln[0].reshape(1, d), ln[1].reshape(1, d)]
        in_specs += [vec, vec]
        out_specs.append(row)
        out_shape.append(jax.ShapeDtypeStruct((n, d), F32))
    if do_mod:
        args += [mod, mod]
        in_specs += [pl.BlockSpec((SUBLANES, d), lambda i: (0, k_shift)),
                     pl.BlockSpec((SUBLANES, d), lambda i: (0, k_shift + 1))]
        out_specs.append(row)
        out_shape.append(jax.ShapeDtypeStruct((n, d), BF16))
    outs = pl.pallas_call(
        functools.partial(_ln_mod_kernel, tb=tb, n_ctx=n_ctx, do_ln=do_ln, do_mod=do_mod),
        grid=(n // tb,),
        in_specs=in_specs,
        out_specs=out_specs,
        out_shape=out_shape,
        compiler_params=_cparams("parallel"),
        name="ln_mod",
    )(*args)
    outs = list(outs)
    h = outs.pop(0) if do_ln else r
    u = outs.pop(0) if do_mod else None
    return h, u


def _mm_kernel(*refs, kind, tm, n_ctx, nk):
    if kind == "swiglu":
        x_ref, w1_ref, w3_ref, c_ref, o_ref, acc1, acc3 = refs
    elif kind == "res":
        x_ref, w_ref, h_ref, gt_ref, o_ref, acc1 = refs
    else:
        x_ref, w_ref, o_ref, acc1 = refs
    k = pl.program_id(2)

    @pl.when(k == 0)
    def _():
        acc1[...] = jnp.zeros_like(acc1)
        if kind == "swiglu":
            acc3[...] = jnp.zeros_like(acc3)

    x = x_ref[...]
    if kind == "swiglu":
        acc1[...] += _dot(x, w1_ref[...])
        acc3[...] += _dot(x, w3_ref[...])
    else:
        acc1[...] += _dot(x, w_ref[...])

    @pl.when(k == nk - 1)
    def _():
        if kind == "swiglu":
            tn = o_ref.shape[1]
            for j in range(tn // LANES):
                sl = slice(j * LANES, (j + 1) * LANES)
                y = _silu(acc1[:, sl]) * acc3[:, sl]
                if c_ref is not None:
                    y = y * c_ref[...]
                o_ref[:, sl] = y.astype(o_ref.dtype)
        elif kind == "res":
            row0 = pl.program_id(0) * tm
            gt = _select_stream(gt_ref, acc1.shape, row0, n_ctx)
            o_ref[...] = ALPHA * h_ref[...] + gt * acc1[...]
        else:
            o_ref[...] = acc1[...].astype(o_ref.dtype)


def _mm_tiles(m, n, k):
    tm = _pick(m, (1280, 1024, 512, 256, 128, 64, 32, 16, 8))
    tn = _pick(n, (1024, 1152, 512, 256, 128))
    tk = _pick(k, (2048, 1024, 512, 256, 128))
    return tm, tn, tk


def _mm_plain(x, w, out_dtype):
    m, kd = x.shape
    n = w.shape[1]
    tm, tn, tk = _mm_tiles(m, n, kd)
    nk = kd // tk
    return pl.pallas_call(
        functools.partial(_mm_kernel, kind="plain", tm=tm, n_ctx=0, nk=nk),
        grid=(m // tm, n // tn, nk),
        in_specs=[pl.BlockSpec((tm, tk), lambda i, j, k: (i, k)),
                  pl.BlockSpec((tk, tn), lambda i, j, k: (k, j))],
        out_specs=pl.BlockSpec((tm, tn), lambda i, j, k: (i, j)),
        out_shape=jax.ShapeDtypeStruct((m, n), out_dtype),
        scratch_shapes=[pltpu.VMEM((tm, tn), F32)],
        compiler_params=_cparams("parallel", "parallel", "arbitrary"),
        name="mm_plain",
    )(x, w)


def _mm_res(x, w, h, mod, k_gate, n_ctx):
    m, kd = x.shape
    n = w.shape[1]
    tm, tn, tk = _mm_tiles(m, n, kd)
    nk = kd // tk
    gate_block0 = k_gate * (n // tn)
    return pl.pallas_call(
        functools.partial(_mm_kernel, kind="res", tm=tm, n_ctx=n_ctx, nk=nk),
        grid=(m // tm, n // tn, nk),
        in_specs=[pl.BlockSpec((tm, tk), lambda i, j, k: (i, k)),
                  pl.BlockSpec((tk, tn), lambda i, j, k: (k, j)),
                  pl.BlockSpec((tm, tn), lambda i, j, k: (i, j)),
                  pl.BlockSpec((SUBLANES, tn), lambda i, j, k: (0, gate_block0 + j))],
        out_specs=pl.BlockSpec((tm, tn), lambda i, j, k: (i, j)),
        out_shape=jax.ShapeDtypeStruct((m, n), F32),
        scratch_shapes=[pltpu.VMEM((tm, tn), F32)],
        compiler_params=_cparams("parallel", "parallel", "arbitrary"),
        name="mm_res",
    )(x, w, h, mod)


def _mm_swiglu(x, w1, w3, cexp=None):
    m, kd = x.shape
    stacked = w1.ndim == 3
    f = w1.shape[-1]
    n_exp = w1.shape[0] if stacked else 1
    tm, tn, tk = _mm_tiles(m, f, kd)
    nk = kd // tk
    nf = f // tn
    if stacked:
        w_spec = pl.BlockSpec((None, tk, tn), lambda i, j, k: (j // nf, k, j % nf))
    else:
        w_spec = pl.BlockSpec((tk, tn), lambda i, j, k: (k, j))
    in_specs = [pl.BlockSpec((tm, tk), lambda i, j, k: (i, k)), w_spec, w_spec]
    args = [x, w1, w3]
    if cexp is not None:
        in_specs.append(pl.BlockSpec((None, tm, LANES), lambda i, j, k: (j // nf, i, 0)))
        args.append(cexp)

    def body(*refs):
        if cexp is None:
            refs = refs[:3] + (None,) + refs[3:]
        _mm_kernel(*refs, kind="swiglu", tm=tm, n_ctx=0, nk=nk)

    return pl.pallas_call(
        body,
        grid=(m // tm, n_exp * nf, nk),
        in_specs=in_specs,
        out_specs=pl.BlockSpec((tm, tn), lambda i, j, k: (i, j)),
        out_shape=jax.ShapeDtypeStruct((m, n_exp * f), BF16),
        scratch_shapes=[pltpu.VMEM((tm, tn), F32), pltpu.VMEM((tm, tn), F32)],
        compiler_params=_cparams("parallel", "parallel", "arbitrary"),
        name="mm_swiglu",
    )(*args)


def _split3(x):
    hi = x.astype(BF16)
    r1 = x - hi.astype(F32)
    mid = r1.astype(BF16)
    lo = (r1 - mid.astype(F32)).astype(BF16)
    return hi, mid, lo


def _head_seg_matrix(width):
    r = lax.broadcasted_iota(jnp.int32, (LANES, LANES), 0) // width
    c = lax.broadcasted_iota(jnp.int32, (LANES, LANES), 1) // width
    return (r == c).astype(BF16)


def _segsum_tile(x, seg):
    hi, mid, lo = _split3(x)
    return _dot(hi, seg) + _dot(mid, seg) + _dot(lo, seg)


def _rwkv_prep_kernel(p_ref, prev_ref, next_ref, mu_ref, w0_ref, w2_ref, a0_ref, a2_ref, g2_ref,
                      kk_ref, ka_ref, rk_ref,
                      r_out, v_out, kap_out, g_out, bonus_out, lw_out, b_out, kd_out,
                      *, tb, n_ctx, n_rows, width, win_w, win_a, o_gl):
    x = p_ref[...]
    row = pl.program_id(0) * tb + lax.broadcasted_iota(jnp.int32, (tb, 1), 0)
    local = lax.broadcasted_iota(jnp.int32, (tb, 1), 0)
    prev = jnp.where(local == 0, prev_ref[...], pltpu.roll(x, 1, 0))
    nxt = jnp.where(local == tb - 1, next_ref[...], pltpu.roll(x, tb - 1, 0))
    prev = jnp.where((row == 0) | (row == n_ctx), 0.0, prev)
    nxt = jnp.where((row == n_ctx - 1) | (row == n_rows - 1), 0.0, nxt)
    x = x + mu_ref[...] * (0.5 * (prev + nxt) - x)

    w = width
    r = x[:, 0:w]
    k = x[:, w:2 * w]
    v = x[:, 2 * w:3 * w]
    gl = x[:, o_gl:o_gl + A_GATE_LORA]
    wl = x[:, win_w[0]:win_w[0] + win_w[1]]
    al = x[:, win_a[0]:win_a[0] + win_a[1]]

    g_out[...] = _dot(_sigmoid(gl).astype(BF16), g2_ref[...])
    lw_lora = _dot(jnp.tanh(wl).astype(BF16), w2_ref[...])
    a_lora = _dot(al.astype(BF16), a2_ref[...])

    kk = k * kk_ref[...]
    seg = _head_seg_matrix(A_HEAD)
    rk = rk_ref[...]
    ka = ka_ref[...]
    kd_sum = jnp.zeros_like(k)
    kaps = []
    for j in range(w // LANES):
        sl = slice(j * LANES, (j + 1) * LANES)
        t = kk[:, sl]
        n2 = _segsum_tile(t * t, seg)
        kaps.append(t / jnp.maximum(jnp.sqrt(n2), 1e-12))
    kap = jnp.concatenate(kaps, axis=-1)
    for d in range(2):
        z = w0_ref[d:d + 1, :] + lw_lora[:, d * w:(d + 1) * w]
        softplus = jnp.maximum(-z, 0.0) + jnp.log(1.0 + jnp.exp(-jnp.abs(z)))
        wlog = -softplus - 0.5
        lw_out[d] = -jnp.exp(wlog)
        a = _sigmoid(a0_ref[d:d + 1, :] + a_lora[:, d * w:(d + 1) * w])
        b_out[d] = kap * a
        kd = k * (1.0 + (a - 1.0) * ka)
        kd_out[d] = kd
        kd_sum = kd_sum + kd
    bon = r * kd_sum * rk
    bons = []
    for j in range(w // LANES):
        sl = slice(j * LANES, (j + 1) * LANES)
        bons.append(_segsum_tile(bon[:, sl], seg))
    bonus_out[...] = jnp.concatenate(bons, axis=-1) * v
    r_out[...] = r
    v_out[...] = v
    kap_out[...] = kap


def _aligned_window(offset, width, limit):
    start = offset // LANES * LANES
    size = min(_round_up(offset + width - start, LANES), limit - start)
    return start, size


def _window_weight(blocks, offset, window, width):
    start, size = window
    lora = blocks.shape[1]
    wt = jnp.zeros((size, 2 * width), F32)
    for d in range(2):
        r0 = offset - start + d * lora
        wt = wt.at[r0:r0 + lora, d * width:(d + 1) * width].set(blocks[d])
    return wt.astype(BF16)


def _rwkv_prepare(pa, n_ctx, a_in, mu, w0, w2, a0, a2, g2, k_k, k_a, r_k):
    n = pa.shape[0]
    w = g2.shape[1]
    tb = _pick(n_ctx if n_ctx else n, (64, 32, 16, 8))
    nblk = n // tb
    o_gl = 3 * w
    o_wl = o_gl + A_GATE_LORA
    o_al = o_wl + 2 * A_DECAY_LORA
    win_w = _aligned_window(o_wl, 2 * A_DECAY_LORA, a_in)
    win_a = _aligned_window(o_al, 2 * A_AAA_LORA, a_in)
    w2w = _window_weight(w2, o_wl, win_w, w)
    a2w = _window_weight(a2, o_al, win_a, w)
    zero = jnp.zeros((1, a_in), F32)
    last_rows = pa[tb - 1::tb, :a_in]
    first_rows = pa[0::tb, :a_in]
    prev_rows = jnp.concatenate([zero, last_rows[:-1]], axis=0).reshape(nblk, 1, a_in)
    next_rows = jnp.concatenate([first_rows[1:], zero], axis=0).reshape(nblk, 1, a_in)

    tok = pl.BlockSpec((tb, w), lambda i: (i, 0))
    tok2 = pl.BlockSpec((2, tb, w), lambda i: (0, i, 0))
    halo = pl.BlockSpec((None, 1, a_in), lambda i: (i, 0, 0))

    def full(a):
        return pl.BlockSpec(a.shape, lambda i: (0,) * a.ndim)

    params = [mu.reshape(1, a_in), w0, w2w, a0, a2w, g2.astype(BF16), k_k.reshape(1, w), k_a.reshape(1, w),
              r_k.reshape(1, w)]
    one = jax.ShapeDtypeStruct((n, w), F32)
    two = jax.ShapeDtypeStruct((2, n, w), F32)
    return pl.pallas_call(
        functools.partial(_rwkv_prep_kernel, tb=tb, n_ctx=n_ctx, n_rows=n, width=w, win_w=win_w, win_a=win_a,
                          o_gl=o_gl),
        grid=(nblk,),
        in_specs=[pl.BlockSpec((tb, a_in), lambda i: (i, 0)), halo, halo] + [full(a) for a in params],
        out_specs=[tok, tok, tok, tok, tok, tok2, tok2, tok2],
        out_shape=[one, one, one, one, one, two, two, two],
        compiler_params=_cparams("parallel"),
        name="rwkv_prepare",
    )(pa, prev_rows, next_rows, *params)


def _chunk_order(d, c, n_ctx_chunks, n_chunks):
    back = jnp.where(c < n_ctx_chunks, n_ctx_chunks - 1 - c, n_chunks - 1 - (c - n_ctx_chunks))
    return jnp.where(d == 0, c, back)


def _split2(x):
    hi = x.astype(BF16)
    return hi, (x - hi.astype(F32)).astype(BF16)


def _dot_x3(a, b):
    a_hi, a_lo = _split2(a)
    b_hi, b_lo = _split2(b)
    return _dot(a_hi, b_hi) + (_dot(a_lo, b_hi) + _dot(a_hi, b_lo))


def _rwkv_scan_kernel(r_ref, kap_ref, v_ref, lw_ref, b_ref, k_ref, y_ref, s_ref, *, pairs):
    c_len = r_ref.shape[0]
    d = pl.program_id(0)
    sign = jnp.where(d == 1, -1, 1)

    @pl.when(pl.program_id(2) == 0)
    def _():
        s_ref[...] = jnp.zeros_like(s_ref)

    ti = lax.broadcasted_iota(jnp.int32, (c_len, c_len), 0)
    si = lax.broadcasted_iota(jnp.int32, (c_len, c_len), 1)
    upto = ((ti - si) * sign >= 0).astype(BF16)
    lane = lax.broadcasted_iota(jnp.int32, (c_len, PAIR), 1)
    head0 = lane < A_HEAD
    n2 = 2 * c_len
    row = lax.broadcasted_iota(jnp.int32, (n2, n2), 0)
    col = lax.broadcasted_iota(jnp.int32, (n2, n2), 1)
    same = (row // c_len) == (col // c_len)
    order = ((row % c_len) - (col % c_len)) * sign
    strict = same & (order > 0)
    incl = same & (order >= 0)
    eye = row == col

    def stack(x):
        return jnp.concatenate([jnp.where(head0, x, 0.0), jnp.where(head0, 0.0, x)], axis=0)

    def dup(x):
        return jnp.concatenate([x, x], axis=0)

    def bdot(a, b):
        return _dot(a.astype(BF16), b.astype(BF16))

    P = range(pairs)
    lanes = [slice(pi * PAIR, (pi + 1) * PAIR) for pi in P]
    lw = [lw_ref[:, ls] for ls in lanes]
    parts = [_split3(x) for x in lw]
    cl_in = [_dot(upto, hi) + (_dot(upto, mid) + _dot(upto, lo)) for hi, mid, lo in parts]
    cl_tot = [jnp.sum(x, axis=0, keepdims=True) for x in lw]
    g_in = [jnp.exp(c) for c in cl_in]
    g_ex = [jnp.exp(c - x) for c, x in zip(cl_in, lw)]
    g_inv = [jnp.exp(-c) for c in cl_in]
    g_end = [jnp.exp(t - c) for t, c in zip(cl_tot, cl_in)]

    ks = [stack(kap_ref[:, ls] * g).astype(BF16) for ls, g in zip(lanes, g_ex)]
    rs = [stack(r_ref[:, ls] * g) for ls, g in zip(lanes, g_in)]
    vs = [stack(v_ref[:, ls]).astype(BF16) for ls in lanes]
    bk = [jnp.concatenate([dup(b_ref[:, ls] * g), dup(k_ref[:, ls] * g)], axis=0).astype(BF16)
          for ls, g in zip(lanes, g_inv)]
    gram = [_dot_nt(jnp.concatenate([ks[i], rs[i].astype(BF16)], axis=0), bk[i]) for i in P]
    bhs_t = [stack(b_ref[:, ls] * g).T.astype(BF16) for ls, g in zip(lanes, g_end)]
    khs_t = [stack(k_ref[:, ls] * g).T.astype(BF16) for ls, g in zip(lanes, g_end)]
    n_ = [jnp.where(strict, gm[:n2, n2:], 0.0).astype(BF16) for gm in gram]
    lb = [jnp.where(incl, gm[n2:, :n2], 0.0).astype(BF16) for gm in gram]
    lk = [jnp.where(incl, gm[n2:, n2:], 0.0).astype(BF16) for gm in gram]

    pw = [jnp.where(strict, -gm[:n2, :n2], 0.0) for gm in gram]
    tinv = [jnp.where(eye, 1.0, 0.0) + x for x in pw]
    span = 2
    while span < c_len:
        pw = [bdot(x, x) for x in pw]
        tinv = [t + bdot(t, x) for t, x in zip(tinv, pw)]
        span *= 2

    nv = [_dot(n_[i], vs[i]) for i in P]
    kq_u = [bdot(tinv[i], jnp.concatenate([ks[i], nv[i].astype(BF16)], axis=1)).astype(BF16) for i in P]
    low = [_dot(bhs_t[i], kq_u[i]) for i in P]
    lbq = [_dot(lb[i], kq_u[i]) for i in P]
    kv = [_dot(khs_t[i], vs[i]) for i in P]
    lkv = [_dot(lk[i], vs[i]) for i in P]
    for i in P:
        a_ = jnp.where(eye, jnp.exp(cl_tot[i]), 0.0) - low[i][:, :PAIR]
        g_ = kv[i] - low[i][:, PAIR:]
        rq = rs[i] - lbq[i][:, :PAIR]
        yi = lkv[i] - lbq[i][:, PAIR:]
        prop = _dot_x3(jnp.concatenate([rq, a_], axis=0), s_ref[i])
        ys = prop[:n2] + yi
        y_ref[:, lanes[i]] = ys[:c_len] + ys[c_len:]
        s_ref[i] = prop[n2:] + g_


def _rwkv_scan(r, kap, v, lw, b, kd, n_ctx):
    n, w = r.shape
    c_len = RWKV_CHUNK
    n_chunks = n // c_len
    n_ctx_chunks = n_ctx // c_len
    pairs = _pick(w // PAIR, (16, 4, 2, 1))
    bw = pairs * PAIR

    def tok(d, p, c):
        return (_chunk_order(d, c, n_ctx_chunks, n_chunks), p)

    def tok2(d, p, c):
        return (d, _chunk_order(d, c, n_ctx_chunks, n_chunks), p)

    one = pl.BlockSpec((c_len, bw), tok)
    two = pl.BlockSpec((None, c_len, bw), tok2)
    return pl.pallas_call(
        functools.partial(_rwkv_scan_kernel, pairs=pairs),
        grid=(2, w // bw, n_chunks),
        in_specs=[one, one, one, two, two, two],
        out_specs=two,
        out_shape=jax.ShapeDtypeStruct((2, n, w), F32),
        scratch_shapes=[pltpu.VMEM((pairs, PAIR, PAIR), F32)],
        compiler_params=_cparams("parallel", "parallel", "arbitrary"),
        name="rwkv_scan",
    )(r, kap, v, lw, b, kd)


def _rope_tables(n_ctx, n_lat, half):
    nq = half // 2
    pos = jnp.arange(n_lat, dtype=jnp.int32)
    row = (pos // GRID_W).astype(F32)
    col = (pos % GRID_W).astype(F32)
    inv = ROPE_THETA ** (-jnp.arange(nq, dtype=F32) / nq)

    def part(p):
        ang = p[:, None] * inv[None, :]
        c, s = jnp.cos(ang), jnp.sin(ang)
        return jnp.concatenate([c, c], -1), jnp.concatenate([-s, s], -1)

    cr, sr = part(row)
    cc, sc = part(col)
    cos = jnp.concatenate([cr, cc], -1)
    sin = jnp.concatenate([sr, sc], -1)
    cos = jnp.concatenate([jnp.ones((n_ctx, 2 * half), F32), cos], 0)
    sin = jnp.concatenate([jnp.zeros((n_ctx, 2 * half), F32), sin], 0)
    return cos, sin


def _rope_tile(x, cos, sin, quarter):
    if 2 * quarter == LANES:
        partner = pltpu.roll(x, quarter, 1)
    else:
        lane = lax.broadcasted_iota(jnp.int32, x.shape, 1)
        lower = (lane % (2 * quarter)) < quarter
        partner = jnp.where(lower, pltpu.roll(x, LANES - quarter, 1), pltpu.roll(x, quarter, 1))
    return x * cos + partner * sin


def _retention_kernel(q_ref, k_ref, v_ref, cos_ref, sin_ref, lg_ref, y_ref, s_ref, *, heads):
    c_len = q_ref.shape[0]
    d = pl.program_id(0)
    rev = d == 1

    @pl.when(pl.program_id(1) == 0)
    def _():
        s_ref[...] = jnp.zeros_like(s_ref)

    cos = cos_ref[...]
    sin = sin_ref[...]

    def rope(x):
        tiles = [_rope_tile(x[:, j * LANES:(j + 1) * LANES], cos[:, j * LANES:(j + 1) * LANES],
                            sin[:, j * LANES:(j + 1) * LANES], LANES // 2) for j in range(B_HEAD // LANES)]
        return jnp.concatenate(tiles, axis=-1)

    n_i = lax.broadcasted_iota(jnp.int32, (c_len, c_len), 0)
    m_i = lax.broadcasted_iota(jnp.int32, (c_len, c_len), 1)
    dist = (n_i - m_i) * jnp.where(rev, -1, 1)
    seen = dist >= jnp.where(rev, 1, 0)
    dist_f = jnp.maximum(dist, 0).astype(F32)
    idx = lax.broadcasted_iota(jnp.int32, (c_len, B_HEAD), 0)
    order = jnp.where(rev, c_len - 1 - idx, idx).astype(F32)

    H = range(heads)
    cols = [slice(h * B_HEAD, (h + 1) * B_HEAD) for h in H]
    lg = [jnp.concatenate([lg_ref[h, 0:1, :]] * (B_HEAD // LANES), axis=-1) for h in H]
    lg_c = [jnp.concatenate([lg_ref[h, 0:1, :]] * (c_len // LANES), axis=-1) for h in H]
    q = [rope(q_ref[:, cs]) for cs in cols]
    k = [rope(k_ref[:, cs]) * (B_HEAD ** -0.5) for cs in cols]
    v = [v_ref[:, cs].astype(BF16) for cs in cols]
    inner = [_dot_nt(q[h].astype(BF16), k[h].astype(BF16)) for h in H]
    cross = [_dot((q[h] * jnp.exp(lg[h] * (order + 1.0))).astype(BF16), s_ref[h].astype(BF16)) for h in H]
    kt = [(k[h] * jnp.exp(lg[h] * (c_len - 1.0 - order))).T.astype(BF16) for h in H]
    upd = [_dot(kt[h], v[h]) for h in H]
    for h in H:
        decay = jnp.where(seen, jnp.exp(lg_c[h] * dist_f), 0.0)
        y_ref[:, cols[h]] = _dot((inner[h] * decay).astype(BF16), v[h]) + cross[h]
        s_ref[h] = s_ref[h] * jnp.exp(lg[h] * float(c_len)) + upd[h]


def _retention(pb, n_ctx, cos, sin):
    n = pb.shape[0]
    width = pb.shape[1] // 4
    heads = width // B_HEAD
    c_len = RET_CHUNK
    n_chunks = n // c_len
    n_ctx_chunks = n_ctx // c_len
    log_g = jnp.log1p(-jnp.exp2(-5.0 - jnp.arange(heads, dtype=F32)))
    lg = jnp.stack([log_g, log_g[::-1]])
    lg = jnp.broadcast_to(lg[:, :, None, None], (2, heads, SUBLANES, LANES))

    def part(which):
        return pl.BlockSpec((c_len, width), lambda d, c: (_chunk_order(d, c, n_ctx_chunks, n_chunks), which))

    tab = pl.BlockSpec((c_len, B_HEAD), lambda d, c: (_chunk_order(d, c, n_ctx_chunks, n_chunks), 0))
    return pl.pallas_call(
        functools.partial(_retention_kernel, heads=heads),
        grid=(2, n_chunks),
        in_specs=[part(0), part(1), part(2), tab, tab,
                  pl.BlockSpec((None, heads, SUBLANES, LANES), lambda d, c: (d, 0, 0, 0))],
        out_specs=pl.BlockSpec((None, c_len, width),
                               lambda d, c: (d, _chunk_order(d, c, n_ctx_chunks, n_chunks), 0)),
        out_shape=jax.ShapeDtypeStruct((2, n, width), F32),
        scratch_shapes=[pltpu.VMEM((heads, B_HEAD, B_HEAD), F32)],
        compiler_params=_cparams("parallel", "arbitrary"),
        name="retention",
    )(pb, pb, pb, cos, sin, lg)


def _mix_finish_kernel(ya_ref, bonus_ref, g_ref, yb_ref, gate_ref, lg_ref, lb_ref, gg_ref, gb_ref, o_ref):
    wa = bonus_ref.shape[1]
    ya = ya_ref[0] + ya_ref[1]
    seg = _head_seg_matrix(A_HEAD)
    inv = 1.0 / A_HEAD
    for j in range(wa // LANES):
        sl = slice(j * LANES, (j + 1) * LANES)
        t = ya[:, sl]
        mu = _segsum_tile(t, seg) * inv
        tc = t - mu
        var = _segsum_tile(tc * tc, seg) * inv
        o = tc * lax.rsqrt(var + A_GN_EPS) * lg_ref[:, sl] + lb_ref[:, sl] + bonus_ref[:, sl]
        o_ref[:, sl] = (o * g_ref[:, sl]).astype(o_ref.dtype)
    yb = yb_ref[0] + yb_ref[1]
    wb = yb.shape[1]
    for j in range(wb // B_HEAD):
        sl = slice(j * B_HEAD, (j + 1) * B_HEAD)
        t = yb[:, sl]
        mu = jnp.mean(t, axis=-1, keepdims=True)
        tc = t - mu
        var = jnp.mean(tc * tc, axis=-1, keepdims=True)
        o = tc * lax.rsqrt(var + B_GN_EPS) * gg_ref[:, sl] + gb_ref[:, sl]
        o_ref[:, wa + j * B_HEAD:wa + (j + 1) * B_HEAD] = (o * _silu(gate_ref[:, sl])).astype(o_ref.dtype)


def _mix_finish(ya, bonus, g, yb, pb, lnx_g, lnx_b, gn_g, gn_b):
    n, wa = bonus.shape
    wb = yb.shape[2]
    tb = _pick(n, (128, 64, 32, 16, 8))
    ra = pl.BlockSpec((tb, wa), lambda i: (i, 0))
    rb = pl.BlockSpec((tb, wb), lambda i: (i, 3))
    va = pl.BlockSpec((1, wa), lambda i: (0, 0))
    vb = pl.BlockSpec((1, wb), lambda i: (0, 0))
    return pl.pallas_call(
        _mix_finish_kernel,
        grid=(n // tb,),
        in_specs=[pl.BlockSpec((2, tb, wa), lambda i: (0, i, 0)), ra, ra,
                  pl.BlockSpec((2, tb, wb), lambda i: (0, i, 0)), rb, va, va, vb, vb],
        out_specs=pl.BlockSpec((tb, wa + wb), lambda i: (i, 0)),
        out_shape=jax.ShapeDtypeStruct((n, wa + wb), BF16),
        compiler_params=_cparams("parallel"),
        name="mix_finish",
    )(ya, bonus, g, yb, pb, lnx_g.reshape(1, wa), lnx_b.reshape(1, wa), gn_g.reshape(1, wb), gn_b.reshape(1, wb))


def _qkv_prep_kernel(x_ref, cos_ref, sin_ref, qg_ref, kg_ref, q_ref, k_ref, v_ref, *, q_heads, kv_heads):
    cos = cos_ref[...]
    sin = sin_ref[...]

    def norm_rope(t, g):
        t = t * lax.rsqrt(jnp.mean(t * t, axis=-1, keepdims=True) + QK_EPS) * g
        return _rope_tile(t, cos, sin, C_HEAD // 4)

    scale = C_HEAD ** -0.5 * LOG2_E
    for h in range(q_heads):
        sl = slice(h * C_HEAD, (h + 1) * C_HEAD)
        q_ref[:, sl] = (norm_rope(x_ref[:, sl], qg_ref[...]) * scale).astype(BF16)
    for h in range(kv_heads):
        sl = slice(h * C_HEAD, (h + 1) * C_HEAD)
        src = slice((q_heads + h) * C_HEAD, (q_heads + h + 1) * C_HEAD)
        k_ref[:, sl] = norm_rope(x_ref[:, src], kg_ref[...]).astype(BF16)
    v0 = (q_heads + kv_heads) * C_HEAD
    v_ref[...] = x_ref[:, v0:v0 + kv_heads * C_HEAD].astype(BF16)


def _qkv_prep(qkv, cos, sin, qn_g, kn_g, q_heads, kv_heads):
    n, c_in = qkv.shape
    tb = _pick(n, (128, 64, 32, 16, 8))
    kvw = kv_heads * C_HEAD
    return pl.pallas_call(
        functools.partial(_qkv_prep_kernel, q_heads=q_heads, kv_heads=kv_heads),
        grid=(n // tb,),
        in_specs=[pl.BlockSpec((tb, c_in), lambda i: (i, 0)),
                  pl.BlockSpec((tb, C_HEAD), lambda i: (i, 0)),
                  pl.BlockSpec((tb, C_HEAD), lambda i: (i, 0)),
                  pl.BlockSpec((1, C_HEAD), lambda i: (0, 0)),
                  pl.BlockSpec((1, C_HEAD), lambda i: (0, 0))],
        out_specs=[pl.BlockSpec((tb, q_heads * C_HEAD), lambda i: (i, 0)),
                   pl.BlockSpec((tb, kvw), lambda i: (i, 0)),
                   pl.BlockSpec((tb, kvw), lambda i: (i, 0))],
        out_shape=[jax.ShapeDtypeStruct((n, q_heads * C_HEAD), BF16),
                   jax.ShapeDtypeStruct((n, kvw), BF16),
                   jax.ShapeDtypeStruct((n, kvw), BF16)],
        compiler_params=_cparams("parallel"),
        name="qkv_prep",
    )(qkv, cos, sin, qn_g.reshape(1, C_HEAD), kn_g.reshape(1, C_HEAD))


def _flash_kernel(q_ref, k_ref, v_ref, o_ref, q_scr, s_scr, p_scr, m_scr, l_scr, c_scr, acc_scr, *, tq, tk, n_kv):
    nkv = n_kv // tk
    for h in range(C_GROUP):
        q_scr[h * tq:(h + 1) * tq, :] = q_ref[:, h * C_HEAD:(h + 1) * C_HEAD]
    m_scr[...] = jnp.full(m_scr.shape, -jnp.inf, F32)
    l_scr[...] = jnp.zeros(l_scr.shape, F32)
    acc_scr[...] = jnp.zeros(acc_scr.shape, F32)

    def scores(j, slot):
        s_scr[slot] = _dot_nt(q_scr[...], k_ref[pl.ds(pl.multiple_of(j * tk, tk), tk), :])

    def values(j, slot):
        return _dot(p_scr[slot], v_ref[pl.ds(pl.multiple_of(j * tk, tk), tk), :])

    def normalise(slot):
        s = s_scr[slot]
        m_old = m_scr[...]
        m_new = jnp.maximum(m_old, jnp.max(s, axis=-1, keepdims=True))
        corr = jnp.exp2(m_old - m_new)
        p = jnp.exp2(s - m_new)
        l_scr[...] = corr * l_scr[...] + jnp.sum(p, axis=-1, keepdims=True)
        m_scr[...] = m_new
        c_scr[...] = corr
        p_scr[slot] = p.astype(BF16)

    scores(0, 0)
    if nkv > 1:
        scores(1, 1)
    normalise(0)

    def step(j, cur):
        scores(j + 1, 1 - cur)
        normalise(cur)
        acc_scr[...] = (acc_scr[...] + values(j - 1, 1 - cur)) * c_scr[...]

    def pair(t, carry):
        step(2 * t + 1, 1)
        step(2 * t + 2, 0)
        return carry

    n_mid = max(nkv - 2, 0)
    if n_mid >= 2:
        lax.fori_loop(0, n_mid // 2, pair, 0)
    if n_mid % 2:
        step(nkv - 2, (nkv - 2) % 2)
    if nkv > 1:
        last = (nkv - 1) % 2
        normalise(last)
        acc_scr[...] = (acc_scr[...] + values(nkv - 2, 1 - last)) * c_scr[...]
    acc = acc_scr[...] + values(nkv - 1, (nkv - 1) % 2)
    out = acc / l_scr[...]
    for h in range(C_GROUP):
        o_ref[:, h * C_HEAD:(h + 1) * C_HEAD] = out[h * tq:(h + 1) * tq].astype(o_ref.dtype)


def _flash(q, k, v, n_ctx):
    n = q.shape[0]
    kv_heads = k.shape[1] // C_HEAD
    n_lat = n - n_ctx
    tq = _pick(n_ctx, (256, 128, 64, 32, 16, 8))
    tk = _pick(n, (1280, 1024, 640, 512, 256, 128, 64, 32, 16, 8))
    gw = C_GROUP * C_HEAD
    q0 = n_ctx // tq
    rows = C_GROUP * tq
    return pl.pallas_call(
        functools.partial(_flash_kernel, tq=tq, tk=tk, n_kv=n),
        grid=(kv_heads, n_lat // tq),
        in_specs=[pl.BlockSpec((tq, gw), lambda g, i: (q0 + i, g)),
                  pl.BlockSpec((n, C_HEAD), lambda g, i: (0, g)),
                  pl.BlockSpec((n, C_HEAD), lambda g, i: (0, g))],
        out_specs=pl.BlockSpec((tq, gw), lambda g, i: (i, g)),
        out_shape=jax.ShapeDtypeStruct((n_lat, kv_heads * gw), BF16),
        scratch_shapes=[pltpu.VMEM((rows, C_HEAD), BF16),
                        pltpu.VMEM((2, rows, tk), F32),
                        pltpu.VMEM((2, rows, tk), BF16),
                        pltpu.VMEM((rows, 1), F32), pltpu.VMEM((rows, 1), F32), pltpu.VMEM((rows, 1), F32),
                        pltpu.VMEM((rows, C_HEAD), F32)],
        compiler_params=_cparams("parallel", "parallel"),
        name="flash_gqa",
    )(q, k, v)


def _router_kernel(u_ref, w_ref, c_ref, *, n_exp):
    logits = jnp.dot(u_ref[...].astype(F32), w_ref[...], preferred_element_type=F32, precision=HIGHEST)
    lane = lax.broadcasted_iota(jnp.int32, logits.shape, 1)
    valid = lane < n_exp
    neg = jnp.float32(-jnp.inf)
    l1 = jnp.where(valid, logits, neg)
    m1 = jnp.max(l1, axis=-1, keepdims=True)
    i1 = jnp.min(jnp.where(l1 == m1, lane, LANES), axis=-1, keepdims=True)
    l2 = jnp.where(lane == i1, neg, l1)
    m2 = jnp.max(l2, axis=-1, keepdims=True)
    i2 = jnp.min(jnp.where(l2 == m2, lane, LANES), axis=-1, keepdims=True)
    e2 = jnp.exp(m2 - m1)
    g1 = 1.0 / (1.0 + e2)
    g2 = e2 / (1.0 + e2)
    c_ref[...] = jnp.where(lane == i1, g1, 0.0) + jnp.where(lane == i2, g2, 0.0)


def _router(h, mod, k_shift, router):
    m, d = h.shape
    n_exp = router.shape[1]
    w = jnp.zeros((d, LANES), F32).at[:, :n_exp].set(router)
    tb = _pick(m, (256, 128, 64, 32, 16, 8))

    def body(h_ref, sh_ref, sc_ref, w_ref, c_ref, u_scr):
        u_scr[...] = h_ref[...] * (1.0 + sc_ref[0:1, :]) + sh_ref[0:1, :]
        _router_kernel(u_scr, w_ref, c_ref, n_exp=n_exp)

    return pl.pallas_call(
        body,
        grid=(m // tb,),
        in_specs=[pl.BlockSpec((tb, d), lambda i: (i, 0)),
                  pl.BlockSpec((SUBLANES, d), lambda i: (0, k_shift)),
                  pl.BlockSpec((SUBLANES, d), lambda i: (0, k_shift + 1)),
                  pl.BlockSpec((d, LANES), lambda i: (0, 0))],
        out_specs=pl.BlockSpec((tb, LANES), lambda i: (i, 0)),
        out_shape=jax.ShapeDtypeStruct((m, LANES), F32),
        scratch_shapes=[pltpu.VMEM((tb, d), F32)],
        compiler_params=_cparams("parallel"),
        name="router_top2",
    )(h, mod, mod, w)


def _pad_cols(w, n):
    return jnp.pad(w, ((0, 0), (0, n - w.shape[1])))


def kernel(x, c, ctx, c_ctx, l0_mod_w, l0_mod_b, l0_w_in, l0_a_mu, l0_a_w0, l0_a_w2, l0_a_a0, l0_a_a2, l0_a_g2, l0_a_k_k, l0_a_k_a, l0_a_r_k, l0_a_lnx_g, l0_a_lnx_b, l0_b_gn_g, l0_b_gn_b, l0_w_out, l0_ln1_g, l0_ln1_b, l0_ffn_w1, l0_ffn_w3, l0_ffn_w2, l0_ln2_g, l0_ln2_b, l1_mod_w, l1_mod_b, l1_w_qkv, l1_q_norm_g, l1_k_norm_g, l1_w_out, l1_ln1_g, l1_ln1_b, l1_router, l1_moe_w1, l1_moe_w3, l1_moe_w2, l1_ln2_g, l1_ln2_b):
    batch, n_lat, d = x.shape
    n_ctx = ctx.shape[1]
    assert batch == 1 and c.shape[0] == 1
    a_width = l0_a_g2.shape[1]
    a_in = l0_a_mu.shape[0]
    b_width = l0_b_gn_g.shape[0]

    c_rows = jnp.zeros((SUBLANES, d), F32).at[0].set(c[0]).at[1].set(c_ctx)
    mod0 = _mod_vectors(c_rows, l0_mod_w, l0_mod_b)
    mod1 = _mod_vectors(c_rows, l1_mod_w, l1_mod_b)

    h = jnp.concatenate([ctx[0], x[0]], axis=0)

    _, u = _ln_mod(h, None, mod0, 0, n_ctx)
    a_pad = _round_up(a_in, 1024) if a_in > 1024 else a_in
    w_in_a = _pad_cols(l0_w_in[:, :a_in], a_pad).astype(BF16)
    w_in_b = l0_w_in[:, a_in:].astype(BF16)
    pa = _mm_plain(u, w_in_a, F32)
    pb = _mm_plain(u, w_in_b, F32)
    r, v, kap, g, bonus, lw, b, kd = _rwkv_prepare(pa, n_ctx, a_in, l0_a_mu, l0_a_w0, l0_a_w2, l0_a_a0, l0_a_a2,
                                                   l0_a_g2, l0_a_k_k, l0_a_k_a, l0_a_r_k)
    ya = _rwkv_scan(r, kap, v, lw, b, kd, n_ctx)
    cos_b, sin_b = _rope_tables(n_ctx, n_lat, B_HEAD // 2)
    yb = _retention(pb, n_ctx, cos_b, sin_b)
    mix = _mix_finish(ya, bonus, g, yb, pb, l0_a_lnx_g, l0_a_lnx_b, l0_b_gn_g, l0_b_gn_b)
    res = _mm_res(mix, l0_w_out.astype(BF16), h, mod0, 2, n_ctx)
    h, u = _ln_mod(res, (l0_ln1_g, l0_ln1_b), mod0, 3, n_ctx)
    d_ff = l0_ffn_w1.shape[1]
    ff_pad = _round_up(d_ff, 1024) if d_ff > 1024 else d_ff
    hid = _mm_swiglu(u, _pad_cols(l0_ffn_w1, ff_pad).astype(BF16), _pad_cols(l0_ffn_w3, ff_pad).astype(BF16))
    w2 = jnp.pad(l0_ffn_w2, ((0, ff_pad - d_ff), (0, 0))).astype(BF16)
    res = _mm_res(hid, w2, h, mod0, 5, n_ctx)
    h, u = _ln_mod(res, (l0_ln2_g, l0_ln2_b), mod1, 0, n_ctx)

    q_heads = l1_w_out.shape[0] // C_HEAD
    kv_heads = q_heads // C_GROUP
    qkv = _mm_plain(u, l1_w_qkv.astype(BF16), F32)
    cos_c, sin_c = _rope_tables(n_ctx, n_lat, C_HEAD // 2)
    q, k, vv = _qkv_prep(qkv, cos_c, sin_c, l1_q_norm_g, l1_k_norm_g, q_heads, kv_heads)
    att = _flash(q, k, vv, n_ctx)
    h_lat = h[n_ctx:]
    res = _mm_res(att, l1_w_out.astype(BF16), h_lat, mod1, 2, 0)
    h_lat, u = _ln_mod(res, (l1_ln1_g, l1_ln1_b), mod1, 3, 0)
    comb = _router(h_lat, mod1, 3, l1_router)
    n_exp = l1_router.shape[1]
    cexp = jnp.broadcast_to(comb[:, :n_exp].T[:, :, None], (n_exp, n_lat, LANES))
    hid = _mm_swiglu(u, l1_moe_w1.astype(BF16), l1_moe_w3.astype(BF16), cexp)
    w2 = l1_moe_w2.reshape(n_exp * l1_moe_w2.shape[1], d).astype(BF16)
    res = _mm_res(hid, w2, h_lat, mod1, 5, 0)
    out, _ = _ln_mod(res, (l1_ln2_g, l1_ln2_b), None, 0, 0)
    return out[None]
```

```python
import functools

import jax
import jax.numpy as jnp
from jax import lax
from jax.experimental import pallas as pl
from jax.experimental.pallas import tpu as pltpu

F32 = jnp.float32
BF16 = jnp.bfloat16
HIGHEST = lax.Precision.HIGHEST

GRID_W = 64
DEPTH = 2
ALPHA = (2 * DEPTH) ** 0.25
LN_EPS = 1e-6
ROPE_THETA = 10000.0
A_HEAD = 64
A_DECAY_LORA = 96
A_AAA_LORA = 96
A_GATE_LORA = 256
A_GN_EPS = 64e-5
B_HEAD = 256
RET_CHUNK = 128
B_GN_EPS = 1e-5
C_HEAD = 128
C_GROUP = 4
QK_EPS = 1e-6
TOP_K = 2
LOG2_E = 1.4426950408889634

LANES = 128
SUBLANES = 8
VMEM_LIMIT_BYTES = 60 * 1024 * 1024

RWKV_CHUNK = 64
PAIR = 2 * A_HEAD


def _cparams(*sem):
    return pltpu.CompilerParams(dimension_semantics=sem, vmem_limit_bytes=VMEM_LIMIT_BYTES)


def _pick(n, candidates):
    for c in candidates:
        if c <= n and n % c == 0:
            return c
    return n


def _round_up(n, m):
    return (n + m - 1) // m * m


def _silu(x):
    return x * (1.0 / (1.0 + jnp.exp(-x)))


def _sigmoid(x):
    return 1.0 / (1.0 + jnp.exp(-x))


def _dot(a, b):
    return jnp.dot(a, b, preferred_element_type=F32)


def _dot_nt(a, b, precision=None):
    return lax.dot_general(a, b, (((1,), (1,)), ((), ())), preferred_element_type=F32, precision=precision)


def _row_is_ctx(shape, row0, n_ctx):
    rows = row0 + lax.broadcasted_iota(jnp.int32, shape, 0)
    return rows < n_ctx


def _select_stream(mod_ref, shape, row0, n_ctx):
    lat = mod_ref[0:1, :]
    if n_ctx == 0:
        return jnp.broadcast_to(lat, shape)
    return jnp.where(_row_is_ctx(shape, row0, n_ctx), mod_ref[1:2, :], lat)


def _modvec_kernel(c_ref, w_ref, b_ref, o_ref):
    s = _silu(c_ref[...])
    o_ref[...] = jnp.dot(s, w_ref[...], preferred_element_type=F32, precision=HIGHEST) + b_ref[...]


def _mod_vectors(c_rows, w, b):
    d, n = w.shape
    tn = _pick(n, (512, 256, 128))
    return pl.pallas_call(
        _modvec_kernel,
        grid=(n // tn,),
        in_specs=[pl.BlockSpec((SUBLANES, d), lambda j: (0, 0)),
                  pl.BlockSpec((d, tn), lambda j: (0, j)),
                  pl.BlockSpec((1, tn), lambda j: (0, j))],
        out_specs=pl.BlockSpec((SUBLANES, tn), lambda j: (0, j)),
        out_shape=jax.ShapeDtypeStruct((SUBLANES, n), F32),
        compiler_params=_cparams("parallel"),
        name="mod_vectors",
    )(c_rows, w, b.reshape(1, n))


def _ln_mod_kernel(*refs, tb, n_ctx, do_ln, do_mod):
    refs = list(refs)
    r_ref = refs.pop(0)
    x = r_ref[...]
    if do_ln:
        g_ref, b_ref = refs.pop(0), refs.pop(0)
        mu = jnp.mean(x, axis=-1, keepdims=True)
        xc = x - mu
        var = jnp.mean(xc * xc, axis=-1, keepdims=True)
        x = xc * lax.rsqrt(var + LN_EPS) * g_ref[...] + b_ref[...]
    if do_mod:
        sh_ref, sc_ref = refs.pop(0), refs.pop(0)
    if do_ln:
        refs.pop(0)[...] = x
    if do_mod:
        row0 = pl.program_id(0) * tb
        sh = _select_stream(sh_ref, x.shape, row0, n_ctx)
        sc = _select_stream(sc_ref, x.shape, row0, n_ctx)
        refs.pop(0)[...] = (x * (1.0 + sc) + sh).astype(BF16)


def _ln_mod(r, ln, mod, k_shift, n_ctx):
    n, d = r.shape
    do_ln, do_mod = ln is not None, mod is not None
    tb = _pick(n, (256, 128, 64, 32, 16, 8))
    row = pl.BlockSpec((tb, d), lambda i: (i, 0))
    vec = pl.BlockSpec((1, d), lambda i: (0, 0))
    args, in_specs, out_specs, out_shape = [r], [row], [], []
    if do_ln:
        args += [ln[0].reshape(1, d), ln[1].reshape(1, d)]
        in_specs += [vec, vec]
        out_specs.append(row)
        out_shape.append(jax.ShapeDtypeStruct((n, d), F32))
    if do_mod:
        args += [mod, mod]
        in_specs += [pl.BlockSpec((SUBLANES, d), lambda i: (0, k_shift)),
                     pl.BlockSpec((SUBLANES, d), lambda i: (0, k_shift + 1))]
        out_specs.append(row)
        out_shape.append(jax.ShapeDtypeStruct((n, d), BF16))
    outs = pl.pallas_call(
        functools.partial(_ln_mod_kernel, tb=tb, n_ctx=n_ctx, do_ln=do_ln, do_mod=do_mod),
        grid=(n // tb,),
        in_specs=in_specs,
        out_specs=out_specs,
        out_shape=out_shape,
        compiler_params=_cparams("parallel"),
        name="ln_mod",
    )(*args)
    outs = list(outs)
    h = outs.pop(0) if do_ln else r
    u = outs.pop(0) if do_mod else None
    return h, u


def _mm_kernel(*refs, kind, tm, n_ctx, nk):
    n_w = 2 if kind == "swiglu" else 1
    n_acc = n_w if nk > 1 else 0
    accs = refs[len(refs) - n_acc:]
    refs = refs[:len(refs) - n_acc]
    x_ref, w_refs, rest = refs[0], refs[1:1 + n_w], refs[1 + n_w:]
    o_ref = rest[-1]
    k = pl.program_id(2)

    def products():
        x = x_ref[...]
        return [_dot(x, w[...]) for w in w_refs]

    def finish(tot):
        if kind == "swiglu":
            c_ref = rest[0] if len(rest) == 2 else None
            for j in range(o_ref.shape[1] // LANES):
                sl = slice(j * LANES, (j + 1) * LANES)
                y = _silu(tot[0][:, sl]) * tot[1][:, sl]
                if c_ref is not None:
                    y = y * c_ref[...]
                o_ref[:, sl] = y.astype(o_ref.dtype)
        elif kind == "res":
            h_ref, gt_ref = rest[0], rest[1]
            gt = _select_stream(gt_ref, tot[0].shape, pl.program_id(0) * tm, n_ctx)
            o_ref[...] = ALPHA * h_ref[...] + gt * tot[0]
        else:
            o_ref[...] = tot[0].astype(o_ref.dtype)

    if nk == 1:
        finish(products())
        return

    @pl.when(k == 0)
    def _():
        for acc, y in zip(accs, products()):
            acc[...] = y

    @pl.when((k > 0) & (k < nk - 1))
    def _():
        for acc, y in zip(accs, products()):
            acc[...] += y

    @pl.when(k == nk - 1)
    def _():
        finish([acc[...] + y for acc, y in zip(accs, products())])


def _mm_tiles(m, n, k):
    tm = _pick(m, (1280, 1024, 512, 256, 128, 64, 32, 16, 8))
    tn = _pick(n, (512, 256, 128))
    tk = _pick(k, (4096, 2816, 2048, 1024, 512, 256, 128))
    return tm, tn, tk


def _mm_scratch(tm, tn, nk, n_w):
    return [pltpu.VMEM((tm, tn), F32)] * n_w if nk > 1 else []


def _mm_plain(x, w, out_dtype):
    m, kd = x.shape
    n = w.shape[1]
    tm, tn, tk = _mm_tiles(m, n, kd)
    nk = kd // tk
    return pl.pallas_call(
        functools.partial(_mm_kernel, kind="plain", tm=tm, n_ctx=0, nk=nk),
        grid=(m // tm, n // tn, nk),
        in_specs=[pl.BlockSpec((tm, tk), lambda i, j, k: (i, k)),
                  pl.BlockSpec((tk, tn), lambda i, j, k: (k, j))],
        out_specs=pl.BlockSpec((tm, tn), lambda i, j, k: (i, j)),
        out_shape=jax.ShapeDtypeStruct((m, n), out_dtype),
        scratch_shapes=_mm_scratch(tm, tn, nk, 1),
        compiler_params=_cparams("parallel", "parallel", "arbitrary"),
        name="mm_plain",
    )(x, w)


def _mm_res(x, w, h, mod, k_gate, n_ctx):
    m, kd = x.shape
    n = w.shape[1]
    tm, tn, tk = _mm_tiles(m, n, kd)
    nk = kd // tk
    gate_block0 = k_gate * (n // tn)
    return pl.pallas_call(
        functools.partial(_mm_kernel, kind="res", tm=tm, n_ctx=n_ctx, nk=nk),
        grid=(m // tm, n // tn, nk),
        in_specs=[pl.BlockSpec((tm, tk), lambda i, j, k: (i, k)),
                  pl.BlockSpec((tk, tn), lambda i, j, k: (k, j)),
                  pl.BlockSpec((tm, tn), lambda i, j, k: (i, j)),
                  pl.BlockSpec((SUBLANES, tn), lambda i, j, k: (0, gate_block0 + j))],
        out_specs=pl.BlockSpec((tm, tn), lambda i, j, k: (i, j)),
        out_shape=jax.ShapeDtypeStruct((m, n), F32),
        scratch_shapes=_mm_scratch(tm, tn, nk, 1),
        compiler_params=_cparams("parallel", "parallel", "arbitrary"),
        name="mm_res",
    )(x, w, h, mod)


def _mm_swiglu(x, w1, w3, cexp=None):
    m, kd = x.shape
    stacked = w1.ndim == 3
    f = w1.shape[-1]
    n_exp = w1.shape[0] if stacked else 1
    tm, tn, tk = _mm_tiles(m, f, kd)
    nk = kd // tk
    nf = f // tn
    if stacked:
        w_spec = pl.BlockSpec((None, tk, tn), lambda i, j, k: (j // nf, k, j % nf))
    else:
        w_spec = pl.BlockSpec((tk, tn), lambda i, j, k: (k, j))
    in_specs = [pl.BlockSpec((tm, tk), lambda i, j, k: (i, k)), w_spec, w_spec]
    args = [x, w1, w3]
    if cexp is not None:
        in_specs.append(pl.BlockSpec((None, tm, LANES), lambda i, j, k: (j // nf, i, 0)))
        args.append(cexp)

    return pl.pallas_call(
        functools.partial(_mm_kernel, kind="swiglu", tm=tm, n_ctx=0, nk=nk),
        grid=(m // tm, n_exp * nf, nk),
        in_specs=in_specs,
        out_specs=pl.BlockSpec((tm, tn), lambda i, j, k: (i, j)),
        out_shape=jax.ShapeDtypeStruct((m, n_exp * f), BF16),
        scratch_shapes=_mm_scratch(tm, tn, nk, 2),
        compiler_params=_cparams("parallel", "parallel", "arbitrary"),
        name="mm_swiglu",
    )(*args)


def _split3(x):
    hi = x.astype(BF16)
    r1 = x - hi.astype(F32)
    mid = r1.astype(BF16)
    lo = (r1 - mid.astype(F32)).astype(BF16)
    return hi, mid, lo


def _head_seg_matrix(width):
    r = lax.broadcasted_iota(jnp.int32, (LANES, LANES), 0) // width
    c = lax.broadcasted_iota(jnp.int32, (LANES, LANES), 1) // width
    return (r == c).astype(BF16)


def _segsum_tile(x, seg):
    hi, mid, lo = _split3(x)
    return _dot(hi, seg) + _dot(mid, seg) + _dot(lo, seg)


def _rwkv_prep_kernel(p_ref, prev_ref, next_ref, mu_ref, w0_ref, w2_ref, a0_ref, a2_ref, g2_ref,
                      kk_ref, ka_ref, rk_ref,
                      r_out, v_out, kap_out, g_out, bonus_out, lw_out, b_out, kd_out,
                      *, tb, n_ctx, n_rows, width, win_w, win_a, o_gl):
    x = p_ref[...]
    row = pl.program_id(0) * tb + lax.broadcasted_iota(jnp.int32, (tb, 1), 0)
    local = lax.broadcasted_iota(jnp.int32, (tb, 1), 0)
    prev = jnp.where(local == 0, prev_ref[SUBLANES - 1:SUBLANES, :], pltpu.roll(x, 1, 0))
    nxt = jnp.where(local == tb - 1, next_ref[0:1, :], pltpu.roll(x, tb - 1, 0))
    prev = jnp.where((row == 0) | (row == n_ctx), 0.0, prev)
    nxt = jnp.where((row == n_ctx - 1) | (row == n_rows - 1), 0.0, nxt)
    x = x + mu_ref[...] * (0.5 * (prev + nxt) - x)

    w = width
    r = x[:, 0:w]
    k = x[:, w:2 * w]
    v = x[:, 2 * w:3 * w]
    gl = x[:, o_gl:o_gl + A_GATE_LORA]
    wl = x[:, win_w[0]:win_w[0] + win_w[1]]
    al = x[:, win_a[0]:win_a[0] + win_a[1]]

    g_out[...] = _dot(_sigmoid(gl).astype(BF16), g2_ref[...])
    lw_lora = _dot(jnp.tanh(wl).astype(BF16), w2_ref[...])
    a_lora = _dot(al.astype(BF16), a2_ref[...])

    kk = k * kk_ref[...]
    seg = _head_seg_matrix(A_HEAD)
    rk = rk_ref[...]
    ka = ka_ref[...]
    kd_sum = jnp.zeros_like(k)
    kaps = []
    for j in range(w // LANES):
        sl = slice(j * LANES, (j + 1) * LANES)
        t = kk[:, sl]
        n2 = _segsum_tile(t * t, seg)
        kaps.append(t / jnp.maximum(jnp.sqrt(n2), 1e-12))
    kap = jnp.concatenate(kaps, axis=-1)
    for d in range(2):
        z = w0_ref[d:d + 1, :] + lw_lora[:, d * w:(d + 1) * w]
        softplus = jnp.maximum(-z, 0.0) + jnp.log(1.0 + jnp.exp(-jnp.abs(z)))
        wlog = -softplus - 0.5
        lw_out[d] = -jnp.exp(wlog)
        a = _sigmoid(a0_ref[d:d + 1, :] + a_lora[:, d * w:(d + 1) * w])
        b_out[d] = kap * a
        kd = k * (1.0 + (a - 1.0) * ka)
        kd_out[d] = kd
        kd_sum = kd_sum + kd
    bon = r * kd_sum * rk
    bons = []
    for j in range(w // LANES):
        sl = slice(j * LANES, (j + 1) * LANES)
        bons.append(_segsum_tile(bon[:, sl], seg))
    bonus_out[...] = jnp.concatenate(bons, axis=-1) * v
    r_out[...] = r
    v_out[...] = v
    kap_out[...] = kap


def _aligned_window(offset, width, limit):
    start = offset // LANES * LANES
    size = min(_round_up(offset + width - start, LANES), limit - start)
    return start, size


def _window_weight(blocks, offset, window, width):
    start, size = window
    lora = blocks.shape[1]
    wt = jnp.zeros((size, 2 * width), F32)
    for d in range(2):
        r0 = offset - start + d * lora
        wt = wt.at[r0:r0 + lora, d * width:(d + 1) * width].set(blocks[d])
    return wt.astype(BF16)


def _rwkv_prepare(pa, n_ctx, a_in, mu, w0, w2, a0, a2, g2, k_k, k_a, r_k):
    n = pa.shape[0]
    w = g2.shape[1]
    tb = _pick(n_ctx if n_ctx else n, (64, 32, 16, 8))
    nblk = n // tb
    o_gl = 3 * w
    o_wl = o_gl + A_GATE_LORA
    o_al = o_wl + 2 * A_DECAY_LORA
    win_w = _aligned_window(o_wl, 2 * A_DECAY_LORA, a_in)
    win_a = _aligned_window(o_al, 2 * A_AAA_LORA, a_in)
    w2w = _window_weight(w2, o_wl, win_w, w)
    a2w = _window_weight(a2, o_al, win_a, w)
    g = tb // SUBLANES
    n_groups = n // SUBLANES
    prev_spec = pl.BlockSpec((SUBLANES, a_in), lambda i: (jnp.maximum(i * g - 1, 0), 0))
    next_spec = pl.BlockSpec((SUBLANES, a_in), lambda i: (jnp.minimum((i + 1) * g, n_groups - 1), 0))

    tok = pl.BlockSpec((tb, w), lambda i: (i, 0))
    tok2 = pl.BlockSpec((2, tb, w), lambda i: (0, i, 0))

    def full(a):
        return pl.BlockSpec(a.shape, lambda i: (0,) * a.ndim)

    params = [mu.reshape(1, a_in), w0, w2w, a0, a2w, g2.astype(BF16), k_k.reshape(1, w), k_a.reshape(1, w),
              r_k.reshape(1, w)]
    one = jax.ShapeDtypeStruct((n, w), F32)
    two = jax.ShapeDtypeStruct((2, n, w), F32)
    return pl.pallas_call(
        functools.partial(_rwkv_prep_kernel, tb=tb, n_ctx=n_ctx, n_rows=n, width=w, win_w=win_w, win_a=win_a,
                          o_gl=o_gl),
        grid=(nblk,),
        in_specs=[pl.BlockSpec((tb, a_in), lambda i: (i, 0)), prev_spec, next_spec] + [full(a) for a in params],
        out_specs=[tok, tok, tok, tok, tok, tok2, tok2, tok2],
        out_shape=[one, one, one, one, one, two, two, two],
        compiler_params=_cparams("parallel"),
        name="rwkv_prepare",
    )(pa, pa, pa, *params)


def _chunk_order(d, c, n_ctx_chunks, n_chunks):
    back = jnp.where(c < n_ctx_chunks, n_ctx_chunks - 1 - c, n_chunks - 1 - (c - n_ctx_chunks))
    return jnp.where(d == 0, c, back)


def _split2(x):
    hi = x.astype(BF16)
    return hi, (x - hi.astype(F32)).astype(BF16)


def _dot_x3(a, b):
    a_hi, a_lo = _split2(a)
    b_hi, b_lo = _split2(b)
    return _dot(a_hi, b_hi) + (_dot(a_lo, b_hi) + _dot(a_hi, b_lo))


def _rwkv_scan_kernel(r_ref, kap_ref, v_ref, lw_ref, b_ref, k_ref, y_ref, s_ref, *, pairs):
    c_len = r_ref.shape[0]
    d = pl.program_id(0)
    sign = jnp.where(d == 1, -1, 1)

    @pl.when(pl.program_id(2) == 0)
    def _():
        s_ref[...] = jnp.zeros_like(s_ref)

    ti = lax.broadcasted_iota(jnp.int32, (c_len, c_len), 0)
    si = lax.broadcasted_iota(jnp.int32, (c_len, c_len), 1)
    upto = ((ti - si) * sign >= 0).astype(BF16)
    lane = lax.broadcasted_iota(jnp.int32, (c_len, PAIR), 1)
    head0 = lane < A_HEAD
    n2 = 2 * c_len
    row = lax.broadcasted_iota(jnp.int32, (n2, n2), 0)
    col = lax.broadcasted_iota(jnp.int32, (n2, n2), 1)
    same = (row // c_len) == (col // c_len)
    order = ((row % c_len) - (col % c_len)) * sign
    strict = same & (order > 0)
    incl = same & (order >= 0)
    eye = row == col

    def stack(x):
        return jnp.concatenate([jnp.where(head0, x, 0.0), jnp.where(head0, 0.0, x)], axis=0)

    def dup(x):
        return jnp.concatenate([x, x], axis=0)

    def bdot(a, b):
        return _dot(a.astype(BF16), b.astype(BF16))

    P = range(pairs)
    lanes = [slice(pi * PAIR, (pi + 1) * PAIR) for pi in P]
    lw = [lw_ref[:, ls] for ls in lanes]
    parts = [_split3(x) for x in lw]
    cl_in = [_dot(upto, hi) + (_dot(upto, mid) + _dot(upto, lo)) for hi, mid, lo in parts]
    cl_tot = [jnp.sum(x, axis=0, keepdims=True) for x in lw]
    g_in = [jnp.exp(c) for c in cl_in]
    g_ex = [jnp.exp(c - x) for c, x in zip(cl_in, lw)]
    g_inv = [jnp.exp(-c) for c in cl_in]
    g_end = [jnp.exp(t - c) for t, c in zip(cl_tot, cl_in)]

    ks = [stack(kap_ref[:, ls] * g).astype(BF16) for ls, g in zip(lanes, g_ex)]
    rs = [stack(r_ref[:, ls] * g) for ls, g in zip(lanes, g_in)]
    vs = [stack(v_ref[:, ls]).astype(BF16) for ls in lanes]
    bk = [jnp.concatenate([dup(b_ref[:, ls] * g), dup(k_ref[:, ls] * g)], axis=0).astype(BF16)
          for ls, g in zip(lanes, g_inv)]
    gram = [_dot_nt(jnp.concatenate([ks[i], rs[i].astype(BF16)], axis=0), bk[i]) for i in P]
    bhs_t = [stack(b_ref[:, ls] * g).T.astype(BF16) for ls, g in zip(lanes, g_end)]
    khs_t = [stack(k_ref[:, ls] * g).T.astype(BF16) for ls, g in zip(lanes, g_end)]
    n_ = [jnp.where(strict, gm[:n2, n2:], 0.0).astype(BF16) for gm in gram]
    lb = [jnp.where(incl, gm[n2:, :n2], 0.0).astype(BF16) for gm in gram]
    lk = [jnp.where(incl, gm[n2:, n2:], 0.0).astype(BF16) for gm in gram]

    pw = [jnp.where(strict, -gm[:n2, :n2], 0.0) for gm in gram]
    tinv = [jnp.where(eye, 1.0, 0.0) + x for x in pw]
    span = 2
    while span < c_len:
        pw = [bdot(x, x) for x in pw]
        tinv = [t + bdot(t, x) for t, x in zip(tinv, pw)]
        span *= 2

    nv = [_dot(n_[i], vs[i]) for i in P]
    kq_u = [bdot(tinv[i], jnp.concatenate([ks[i], nv[i].astype(BF16)], axis=1)).astype(BF16) for i in P]
    low = [_dot(bhs_t[i], kq_u[i]) for i in P]
    lbq = [_dot(lb[i], kq_u[i]) for i in P]
    kv = [_dot(khs_t[i], vs[i]) for i in P]
    lkv = [_dot(lk[i], vs[i]) for i in P]
    for i in P:
        a_ = jnp.where(eye, jnp.exp(cl_tot[i]), 0.0) - low[i][:, :PAIR]
        g_ = kv[i] - low[i][:, PAIR:]
        rq = rs[i] - lbq[i][:, :PAIR]
        yi = lkv[i] - lbq[i][:, PAIR:]
        prop = _dot_x3(jnp.concatenate([rq, a_], axis=0), s_ref[i])
        ys = prop[:n2] + yi
        y_ref[:, lanes[i]] = ys[:c_len] + ys[c_len:]
        s_ref[i] = prop[n2:] + g_


def _rwkv_scan(r, kap, v, lw, b, kd, n_ctx):
    n, w = r.shape
    c_len = RWKV_CHUNK
    n_chunks = n // c_len
    n_ctx_chunks = n_ctx // c_len
    pairs = _pick(w // PAIR, (16, 4, 2, 1))
    bw = pairs * PAIR

    def tok(d, p, c):
        return (_chunk_order(d, c, n_ctx_chunks, n_chunks), p)

    def tok2(d, p, c):
        return (d, _chunk_order(d, c, n_ctx_chunks, n_chunks), p)

    one = pl.BlockSpec((c_len, bw), tok)
    two = pl.BlockSpec((None, c_len, bw), tok2)
    return pl.pallas_call(
        functools.partial(_rwkv_scan_kernel, pairs=pairs),
        grid=(2, w // bw, n_chunks),
        in_specs=[one, one, one, two, two, two],
        out_specs=two,
        out_shape=jax.ShapeDtypeStruct((2, n, w), F32),
        scratch_shapes=[pltpu.VMEM((pairs, PAIR, PAIR), F32)],
        compiler_params=_cparams("parallel", "parallel", "arbitrary"),
        name="rwkv_scan",
    )(r, kap, v, lw, b, kd)


def _rope_tables(n_ctx, n_lat, half):
    nq = half // 2
    pos = jnp.arange(n_lat, dtype=jnp.int32)
    row = (pos // GRID_W).astype(F32)
    col = (pos % GRID_W).astype(F32)
    inv = ROPE_THETA ** (-jnp.arange(nq, dtype=F32) / nq)

    def part(p):
        ang = p[:, None] * inv[None, :]
        c, s = jnp.cos(ang), jnp.sin(ang)
        return jnp.concatenate([c, c], -1), jnp.concatenate([-s, s], -1)

    cr, sr = part(row)
    cc, sc = part(col)
    cos = jnp.concatenate([cr, cc], -1)
    sin = jnp.concatenate([sr, sc], -1)
    cos = jnp.concatenate([jnp.ones((n_ctx, 2 * half), F32), cos], 0)
    sin = jnp.concatenate([jnp.zeros((n_ctx, 2 * half), F32), sin], 0)
    return cos, sin


def _rope_tile(x, cos, sin, quarter):
    if 2 * quarter == LANES:
        partner = pltpu.roll(x, quarter, 1)
    else:
        lane = lax.broadcasted_iota(jnp.int32, x.shape, 1)
        lower = (lane % (2 * quarter)) < quarter
        partner = jnp.where(lower, pltpu.roll(x, LANES - quarter, 1), pltpu.roll(x, quarter, 1))
    return x * cos + partner * sin


def _retention_kernel(q_ref, k_ref, v_ref, cos_ref, sin_ref, lg_ref, y_ref, s_ref, *, heads):
    c_len = q_ref.shape[0]
    d = pl.program_id(0)
    rev = d == 1

    @pl.when(pl.program_id(1) == 0)
    def _():
        s_ref[...] = jnp.zeros_like(s_ref)

    cos = cos_ref[...]
    sin = sin_ref[...]

    def rope(x):
        tiles = [_rope_tile(x[:, j * LANES:(j + 1) * LANES], cos[:, j * LANES:(j + 1) * LANES],
                            sin[:, j * LANES:(j + 1) * LANES], LANES // 2) for j in range(B_HEAD // LANES)]
        return jnp.concatenate(tiles, axis=-1)

    n_i = lax.broadcasted_iota(jnp.int32, (c_len, c_len), 0)
    m_i = lax.broadcasted_iota(jnp.int32, (c_len, c_len), 1)
    dist = (n_i - m_i) * jnp.where(rev, -1, 1)
    seen = dist >= jnp.where(rev, 1, 0)
    dist_f = jnp.maximum(dist, 0).astype(F32)
    idx = lax.broadcasted_iota(jnp.int32, (c_len, B_HEAD), 0)
    order = jnp.where(rev, c_len - 1 - idx, idx).astype(F32)

    H = range(heads)
    cols = [slice(h * B_HEAD, (h + 1) * B_HEAD) for h in H]
    lg = [jnp.concatenate([lg_ref[h, 0:1, :]] * (B_HEAD // LANES), axis=-1) for h in H]
    lg_c = [jnp.concatenate([lg_ref[h, 0:1, :]] * (c_len // LANES), axis=-1) for h in H]
    q = [rope(q_ref[:, cs]) for cs in cols]
    k = [rope(k_ref[:, cs]) * (B_HEAD ** -0.5) for cs in cols]
    v = [v_ref[:, cs].astype(BF16) for cs in cols]
    inner = [_dot_nt(q[h].astype(BF16), k[h].astype(BF16)) for h in H]
    cross = [_dot((q[h] * jnp.exp(lg[h] * (order + 1.0))).astype(BF16), s_ref[h].astype(BF16)) for h in H]
    kt = [(k[h] * jnp.exp(lg[h] * (c_len - 1.0 - order))).T.astype(BF16) for h in H]
    upd = [_dot(kt[h], v[h]) for h in H]
    for h in H:
        decay = jnp.where(seen, jnp.exp(lg_c[h] * dist_f), 0.0)
        y_ref[:, cols[h]] = _dot((inner[h] * decay).astype(BF16), v[h]) + cross[h]
        s_ref[h] = s_ref[h] * jnp.exp(lg[h] * float(c_len)) + upd[h]


def _retention(pb, n_ctx, cos, sin):
    n = pb.shape[0]
    width = pb.shape[1] // 4
    heads = width // B_HEAD
    c_len = RET_CHUNK
    n_chunks = n // c_len
    n_ctx_chunks = n_ctx // c_len
    log_g = jnp.log1p(-jnp.exp2(-5.0 - jnp.arange(heads, dtype=F32)))
    lg = jnp.stack([log_g, log_g[::-1]])
    lg = jnp.broadcast_to(lg[:, :, None, None], (2, heads, SUBLANES, LANES))

    def part(which):
        return pl.BlockSpec((c_len, width), lambda d, c: (_chunk_order(d, c, n_ctx_chunks, n_chunks), which))

    tab = pl.BlockSpec((c_len, B_HEAD), lambda d, c: (_chunk_order(d, c, n_ctx_chunks, n_chunks), 0))
    return pl.pallas_call(
        functools.partial(_retention_kernel, heads=heads),
        grid=(2, n_chunks),
        in_specs=[part(0), part(1), part(2), tab, tab,
                  pl.BlockSpec((None, heads, SUBLANES, LANES), lambda d, c: (d, 0, 0, 0))],
        out_specs=pl.BlockSpec((None, c_len, width),
                               lambda d, c: (d, _chunk_order(d, c, n_ctx_chunks, n_chunks), 0)),
        out_shape=jax.ShapeDtypeStruct((2, n, width), F32),
        scratch_shapes=[pltpu.VMEM((heads, B_HEAD, B_HEAD), F32)],
        compiler_params=_cparams("parallel", "arbitrary"),
        name="retention",
    )(pb, pb, pb, cos, sin, lg)


def _mix_finish_kernel(ya_ref, bonus_ref, g_ref, yb_ref, gate_ref, lg_ref, lb_ref, gg_ref, gb_ref, o_ref):
    wa = bonus_ref.shape[1]
    ya = ya_ref[0] + ya_ref[1]
    seg = _head_seg_matrix(A_HEAD)
    inv = 1.0 / A_HEAD
    for j in range(wa // LANES):
        sl = slice(j * LANES, (j + 1) * LANES)
        t = ya[:, sl]
        mu = _segsum_tile(t, seg) * inv
        tc = t - mu
        var = _segsum_tile(tc * tc, seg) * inv
        o = tc * lax.rsqrt(var + A_GN_EPS) * lg_ref[:, sl] + lb_ref[:, sl] + bonus_ref[:, sl]
        o_ref[:, sl] = (o * g_ref[:, sl]).astype(o_ref.dtype)
    yb = yb_ref[0] + yb_ref[1]
    wb = yb.shape[1]
    for j in range(wb // B_HEAD):
        sl = slice(j * B_HEAD, (j + 1) * B_HEAD)
        t = yb[:, sl]
        mu = jnp.mean(t, axis=-1, keepdims=True)
        tc = t - mu
        var = jnp.mean(tc * tc, axis=-1, keepdims=True)
        o = tc * lax.rsqrt(var + B_GN_EPS) * gg_ref[:, sl] + gb_ref[:, sl]
        o_ref[:, wa + j * B_HEAD:wa + (j + 1) * B_HEAD] = (o * _silu(gate_ref[:, sl])).astype(o_ref.dtype)


def _mix_finish(ya, bonus, g, yb, pb, lnx_g, lnx_b, gn_g, gn_b):
    n, wa = bonus.shape
    wb = yb.shape[2]
    tb = _pick(n, (128, 64, 32, 16, 8))
    ra = pl.BlockSpec((tb, wa), lambda i: (i, 0))
    rb = pl.BlockSpec((tb, wb), lambda i: (i, 3))
    va = pl.BlockSpec((1, wa), lambda i: (0, 0))
    vb = pl.BlockSpec((1, wb), lambda i: (0, 0))
    return pl.pallas_call(
        _mix_finish_kernel,
        grid=(n // tb,),
        in_specs=[pl.BlockSpec((2, tb, wa), lambda i: (0, i, 0)), ra, ra,
                  pl.BlockSpec((2, tb, wb), lambda i: (0, i, 0)), rb, va, va, vb, vb],
        out_specs=pl.BlockSpec((tb, wa + wb), lambda i: (i, 0)),
        out_shape=jax.ShapeDtypeStruct((n, wa + wb), BF16),
        compiler_params=_cparams("parallel"),
        name="mix_finish",
    )(ya, bonus, g, yb, pb, lnx_g.reshape(1, wa), lnx_b.reshape(1, wa), gn_g.reshape(1, wb), gn_b.reshape(1, wb))


def _qkv_prep_kernel(x_ref, cos_ref, sin_ref, qg_ref, kg_ref, q_ref, k_ref, v_ref, *, q_heads, kv_heads):
    cos = cos_ref[...]
    sin = sin_ref[...]
    quarter = C_HEAD // 4
    lane = lax.broadcasted_iota(jnp.int32, cos.shape, 1)
    lower = (lane % (2 * quarter)) < quarter
    scale = C_HEAD ** -0.5 * LOG2_E
    n_heads = q_heads + kv_heads
    tiles = [x_ref[:, h * C_HEAD:(h + 1) * C_HEAD] for h in range(n_heads)]
    ms = [jnp.mean(t * t, axis=-1, keepdims=True) for t in tiles]
    gq = qg_ref[...] * scale
    normed = [t * lax.rsqrt(m + QK_EPS) * (gq if h < q_heads else kg_ref[...])
              for h, (t, m) in enumerate(zip(tiles, ms))]
    up = [pltpu.roll(t, LANES - quarter, 1) for t in normed]
    down = [pltpu.roll(t, quarter, 1) for t in normed]
    for h in range(n_heads):
        y = (normed[h] * cos + jnp.where(lower, up[h], down[h]) * sin).astype(BF16)
        if h < q_heads:
            q_ref[:, h * C_HEAD:(h + 1) * C_HEAD] = y
        else:
            k_ref[:, (h - q_heads) * C_HEAD:(h - q_heads + 1) * C_HEAD] = y
    v0 = n_heads * C_HEAD
    v_ref[...] = x_ref[:, v0:v0 + kv_heads * C_HEAD].astype(BF16)


def _qkv_prep(qkv, cos, sin, qn_g, kn_g, q_heads, kv_heads):
    n, c_in = qkv.shape
    tb = _pick(n, (128, 64, 32, 16, 8))
    kvw = kv_heads * C_HEAD
    return pl.pallas_call(
        functools.partial(_qkv_prep_kernel, q_heads=q_heads, kv_heads=kv_heads),
        grid=(n // tb,),
        in_specs=[pl.BlockSpec((tb, c_in), lambda i: (i, 0)),
                  pl.BlockSpec((tb, C_HEAD), lambda i: (i, 0)),
                  pl.BlockSpec((tb, C_HEAD), lambda i: (i, 0)),
                  pl.BlockSpec((1, C_HEAD), lambda i: (0, 0)),
                  pl.BlockSpec((1, C_HEAD), lambda i: (0, 0))],
        out_specs=[pl.BlockSpec((tb, q_heads * C_HEAD), lambda i: (i, 0)),
                   pl.BlockSpec((tb, kvw), lambda i: (i, 0)),
                   pl.BlockSpec((tb, kvw), lambda i: (i, 0))],
        out_shape=[jax.ShapeDtypeStruct((n, q_heads * C_HEAD), BF16),
                   jax.ShapeDtypeStruct((n, kvw), BF16),
                   jax.ShapeDtypeStruct((n, kvw), BF16)],
        compiler_params=_cparams("parallel"),
        name="qkv_prep",
    )(qkv, cos, sin, qn_g.reshape(1, C_HEAD), kn_g.reshape(1, C_HEAD))


def _flash_kernel(q_ref, k_ref, v_ref, o_ref, q_scr, s_scr, p_scr, m_scr, l_scr, c_scr, acc_scr, *, tq, tk, n_kv):
    nkv = n_kv // tk
    for h in range(C_GROUP):
        q_scr[h * tq:(h + 1) * tq, :] = q_ref[:, h * C_HEAD:(h + 1) * C_HEAD]
    m_scr[...] = jnp.full(m_scr.shape, -jnp.inf, F32)
    l_scr[...] = jnp.zeros(l_scr.shape, F32)
    acc_scr[...] = jnp.zeros(acc_scr.shape, F32)
    n_col = tk // LANES

    def scores(j, slot):
        s_scr[slot] = _dot_nt(q_scr[...], k_ref[pl.ds(pl.multiple_of(j * tk, tk), tk), :])

    def values(j, slot):
        return _dot(p_scr[slot], v_ref[pl.ds(pl.multiple_of(j * tk, tk), tk), :])

    def normalise(slot):
        tile_max = s_scr[slot, :, 0:LANES]
        for c in range(1, n_col):
            tile_max = jnp.maximum(tile_max, s_scr[slot, :, c * LANES:(c + 1) * LANES])
        row_max = jnp.broadcast_to(jnp.max(tile_max, axis=-1, keepdims=True), tile_max.shape)
        m_old = m_scr[...]
        m_new = jnp.maximum(m_old, row_max)
        corr = jnp.exp2(m_old - m_new)
        part = None
        for c in range(n_col):
            p = jnp.exp2(s_scr[slot, :, c * LANES:(c + 1) * LANES] - m_new)
            part = p if part is None else part + p
            p_scr[slot, :, c * LANES:(c + 1) * LANES] = p.astype(BF16)
        l_scr[...] = corr * l_scr[...] + part
        m_scr[...] = m_new
        c_scr[...] = corr

    scores(0, 0)
    if nkv > 1:
        scores(1, 1)
    normalise(0)

    def step(j, cur):
        scores(j + 1, 1 - cur)
        normalise(cur)
        acc_scr[...] = (acc_scr[...] + values(j - 1, 1 - cur)) * c_scr[...]

    def pair(t, carry):
        step(2 * t + 1, 1)
        step(2 * t + 2, 0)
        return carry

    n_mid = max(nkv - 2, 0)
    if n_mid >= 2:
        lax.fori_loop(0, n_mid // 2, pair, 0)
    if n_mid % 2:
        step(nkv - 2, (nkv - 2) % 2)
    if nkv > 1:
        last = (nkv - 1) % 2
        normalise(last)
        acc_scr[...] = (acc_scr[...] + values(nkv - 2, 1 - last)) * c_scr[...]
    acc = acc_scr[...] + values(nkv - 1, (nkv - 1) % 2)
    out = acc / jnp.sum(l_scr[...], axis=-1, keepdims=True)
    for h in range(C_GROUP):
        o_ref[:, h * C_HEAD:(h + 1) * C_HEAD] = out[h * tq:(h + 1) * tq].astype(o_ref.dtype)


def _flash(q, k, v, n_ctx):
    n = q.shape[0]
    kv_heads = k.shape[1] // C_HEAD
    n_lat = n - n_ctx
    tq = _pick(n_ctx, (256, 128, 64, 32, 16, 8))
    tk = _pick(n, (1280, 1024, 640, 512, 256, 128, 64, 32, 16, 8))
    gw = C_GROUP * C_HEAD
    q0 = n_ctx // tq
    rows = C_GROUP * tq
    return pl.pallas_call(
        functools.partial(_flash_kernel, tq=tq, tk=tk, n_kv=n),
        grid=(kv_heads, n_lat // tq),
        in_specs=[pl.BlockSpec((tq, gw), lambda g, i: (q0 + i, g)),
                  pl.BlockSpec((n, C_HEAD), lambda g, i: (0, g)),
                  pl.BlockSpec((n, C_HEAD), lambda g, i: (0, g))],
        out_specs=pl.BlockSpec((tq, gw), lambda g, i: (i, g)),
        out_shape=jax.ShapeDtypeStruct((n_lat, kv_heads * gw), BF16),
        scratch_shapes=[pltpu.VMEM((rows, C_HEAD), BF16),
                        pltpu.VMEM((2, rows, tk), F32),
                        pltpu.VMEM((2, rows, tk), BF16),
                        pltpu.VMEM((rows, LANES), F32),
                        pltpu.VMEM((rows, LANES), F32),
                        pltpu.VMEM((rows, LANES), F32),
                        pltpu.VMEM((rows, C_HEAD), F32)],
        compiler_params=_cparams("parallel", "parallel"),
        name="flash_gqa",
    )(q, k, v)


def _router_kernel(u_ref, w_ref, c_ref, *, n_exp):
    logits = jnp.dot(u_ref[...].astype(F32), w_ref[...], preferred_element_type=F32, precision=HIGHEST)
    lane = lax.broadcasted_iota(jnp.int32, logits.shape, 1)
    valid = lane < n_exp
    neg = jnp.float32(-jnp.inf)
    l1 = jnp.where(valid, logits, neg)
    m1 = jnp.max(l1, axis=-1, keepdims=True)
    i1 = jnp.min(jnp.where(l1 == m1, lane, LANES), axis=-1, keepdims=True)
    l2 = jnp.where(lane == i1, neg, l1)
    m2 = jnp.max(l2, axis=-1, keepdims=True)
    i2 = jnp.min(jnp.where(l2 == m2, lane, LANES), axis=-1, keepdims=True)
    e2 = jnp.exp(m2 - m1)
    g1 = 1.0 / (1.0 + e2)
    g2 = e2 / (1.0 + e2)
    c_ref[...] = jnp.where(lane == i1, g1, 0.0) + jnp.where(lane == i2, g2, 0.0)


def _router(h, mod, k_shift, router):
    m, d = h.shape
    n_exp = router.shape[1]
    w = jnp.zeros((d, LANES), F32).at[:, :n_exp].set(router)
    tb = _pick(m, (256, 128, 64, 32, 16, 8))

    def body(h_ref, sh_ref, sc_ref, w_ref, c_ref, u_scr):
        u_scr[...] = h_ref[...] * (1.0 + sc_ref[0:1, :]) + sh_ref[0:1, :]
        _router_kernel(u_scr, w_ref, c_ref, n_exp=n_exp)

    return pl.pallas_call(
        body,
        grid=(m // tb,),
        in_specs=[pl.BlockSpec((tb, d), lambda i: (i, 0)),
                  pl.BlockSpec((SUBLANES, d), lambda i: (0, k_shift)),
                  pl.BlockSpec((SUBLANES, d), lambda i: (0, k_shift + 1)),
                  pl.BlockSpec((d, LANES), lambda i: (0, 0))],
        out_specs=pl.BlockSpec((tb, LANES), lambda i: (i, 0)),
        out_shape=jax.ShapeDtypeStruct((m, LANES), F32),
        scratch_shapes=[pltpu.VMEM((tb, d), F32)],
        compiler_params=_cparams("parallel"),
        name="router_top2",
    )(h, mod, mod, w)


def _pad_cols(w, n):
    return jnp.pad(w, ((0, 0), (0, n - w.shape[1])))


def kernel(x, c, ctx, c_ctx, l0_mod_w, l0_mod_b, l0_w_in, l0_a_mu, l0_a_w0, l0_a_w2, l0_a_a0, l0_a_a2, l0_a_g2, l0_a_k_k, l0_a_k_a, l0_a_r_k, l0_a_lnx_g, l0_a_lnx_b, l0_b_gn_g, l0_b_gn_b, l0_w_out, l0_ln1_g, l0_ln1_b, l0_ffn_w1, l0_ffn_w3, l0_ffn_w2, l0_ln2_g, l0_ln2_b, l1_mod_w, l1_mod_b, l1_w_qkv, l1_q_norm_g, l1_k_norm_g, l1_w_out, l1_ln1_g, l1_ln1_b, l1_router, l1_moe_w1, l1_moe_w3, l1_moe_w2, l1_ln2_g, l1_ln2_b):
    batch, n_lat, d = x.shape
    n_ctx = ctx.shape[1]
    assert batch == 1 and c.shape[0] == 1
    a_width = l0_a_g2.shape[1]
    a_in = l0_a_mu.shape[0]
    b_width = l0_b_gn_g.shape[0]

    c_rows = jnp.zeros((SUBLANES, d), F32).at[0].set(c[0]).at[1].set(c_ctx)
    mod0 = _mod_vectors(c_rows, l0_mod_w, l0_mod_b)
    mod1 = _mod_vectors(c_rows, l1_mod_w, l1_mod_b)

    h = jnp.concatenate([ctx[0], x[0]], axis=0)

    _, u = _ln_mod(h, None, mod0, 0, n_ctx)
    a_pad = _round_up(a_in, 1024) if a_in > 1024 else a_in
    w_in_a = _pad_cols(l0_w_in[:, :a_in], a_pad).astype(BF16)
    w_in_b = l0_w_in[:, a_in:].astype(BF16)
    pa = _mm_plain(u, w_in_a, F32)
    pb = _mm_plain(u, w_in_b, F32)
    r, v, kap, g, bonus, lw, b, kd = _rwkv_prepare(pa, n_ctx, a_in, l0_a_mu, l0_a_w0, l0_a_w2, l0_a_a0, l0_a_a2,
                                                   l0_a_g2, l0_a_k_k, l0_a_k_a, l0_a_r_k)
    ya = _rwkv_scan(r, kap, v, lw, b, kd, n_ctx)
    cos_b, sin_b = _rope_tables(n_ctx, n_lat, B_HEAD // 2)
    yb = _retention(pb, n_ctx, cos_b, sin_b)
    mix = _mix_finish(ya, bonus, g, yb, pb, l0_a_lnx_g, l0_a_lnx_b, l0_b_gn_g, l0_b_gn_b)
    res = _mm_res(mix, l0_w_out.astype(BF16), h, mod0, 2, n_ctx)
    h, u = _ln_mod(res, (l0_ln1_g, l0_ln1_b), mod0, 3, n_ctx)
    d_ff = l0_ffn_w1.shape[1]
    ff_pad = _round_up(d_ff, 1024) if d_ff > 1024 else d_ff
    hid = _mm_swiglu(u, _pad_cols(l0_ffn_w1, ff_pad).astype(BF16), _pad_cols(l0_ffn_w3, ff_pad).astype(BF16))
    w2 = jnp.pad(l0_ffn_w2, ((0, ff_pad - d_ff), (0, 0))).astype(BF16)
    res = _mm_res(hid, w2, h, mod0, 5, n_ctx)
    h, u = _ln_mod(res, (l0_ln2_g, l0_ln2_b), mod1, 0, n_ctx)

    q_heads = l1_w_out.shape[0] // C_HEAD
    kv_heads = q_heads // C_GROUP
    qkv = _mm_plain(u, l1_w_qkv.astype(BF16), F32)
    cos_c, sin_c = _rope_tables(n_ctx, n_lat, C_HEAD // 2)
    q, k, vv = _qkv_prep(qkv, cos_c, sin_c, l1_q_norm_g, l1_k_norm_g, q_heads, kv_heads)
    att = _flash(q, k, vv, n_ctx)
    h_lat = h[n_ctx:]
    res = _mm_res(att, l1_w_out.astype(BF16), h_lat, mod1, 2, 0)
    h_lat, u = _ln_mod(res, (l1_ln1_g, l1_ln1_b), mod1, 3, 0)
    comb = _router(h_lat, mod1, 3, l1_router)
    n_exp = l1_router.shape[1]
    cexp = jnp.broadcast_to(comb[:, :n_exp].T[:, :, None], (n_exp, n_lat, LANES))
    hid = _mm_swiglu(u, l1_moe_w1.astype(BF16), l1_moe_w3.astype(BF16), cexp)
    w2 = l1_moe_w2.reshape(n_exp * l1_moe_w2.shape[1], d).astype(BF16)
    res = _mm_res(hid, w2, h_lat, mod1, 5, 0)
    out, _ = _ln_mod(res, (l1_ln2_g, l1_ln2_b), None, 0, 0)
    return out[None]
```

```python
import functools

import jax
import jax.numpy as jnp
from jax import lax
from jax.experimental import pallas as pl
from jax.experimental.pallas import tpu as pltpu

F32 = jnp.float32
BF16 = jnp.bfloat16
HIGHEST = lax.Precision.HIGHEST

GRID_W = 64
DEPTH = 2
ALPHA = (2 * DEPTH) ** 0.25
LN_EPS = 1e-6
ROPE_THETA = 10000.0
A_HEAD = 64
A_DECAY_LORA = 96
A_AAA_LORA = 96
A_GATE_LORA = 256
A_GN_EPS = 64e-5
B_HEAD = 256
RET_CHUNK = 128
B_GN_EPS = 1e-5
C_HEAD = 128
C_GROUP = 4
QK_EPS = 1e-6
LOG2_E = 1.4426950408889634

LANES = 128
SUBLANES = 8
VMEM_LIMIT_BYTES = 60 * 1024 * 1024

RWKV_CHUNK = 64
PAIR = 2 * A_HEAD


def _cparams(*sem):
    return pltpu.CompilerParams(dimension_semantics=sem, vmem_limit_bytes=VMEM_LIMIT_BYTES)


def _pick(n, candidates):
    for c in candidates:
        if c <= n and n % c == 0:
            return c
    return n


def _round_up(n, m):
    return (n + m - 1) // m * m


def _silu(x):
    return x * (1.0 / (1.0 + jnp.exp(-x)))


def _sigmoid(x):
    return 1.0 / (1.0 + jnp.exp(-x))


def _dot(a, b):
    return jnp.dot(a, b, preferred_element_type=F32)


def _dot_nt(a, b, precision=None):
    return lax.dot_general(a, b, (((1,), (1,)), ((), ())), preferred_element_type=F32, precision=precision)


def _row_is_ctx(shape, row0, n_ctx):
    rows = row0 + lax.broadcasted_iota(jnp.int32, shape, 0)
    return rows < n_ctx


def _select_stream(mod_ref, shape, row0, n_ctx):
    lat = mod_ref[0:1, :]
    if n_ctx == 0:
        return jnp.broadcast_to(lat, shape)
    return jnp.where(_row_is_ctx(shape, row0, n_ctx), mod_ref[1:2, :], lat)


def _modvec_kernel(c_ref, w_ref, b_ref, o_ref):
    s = _silu(c_ref[...])
    o_ref[...] = jnp.dot(s, w_ref[...], preferred_element_type=F32, precision=HIGHEST) + b_ref[...]


def _mod_vectors(c_rows, w, b):
    d, n = w.shape
    tn = _pick(n, (512, 256, 128))
    return pl.pallas_call(
        _modvec_kernel,
        grid=(n // tn,),
        in_specs=[pl.BlockSpec((SUBLANES, d), lambda j: (0, 0)),
                  pl.BlockSpec((d, tn), lambda j: (0, j)),
                  pl.BlockSpec((1, tn), lambda j: (0, j))],
        out_specs=pl.BlockSpec((SUBLANES, tn), lambda j: (0, j)),
        out_shape=jax.ShapeDtypeStruct((SUBLANES, n), F32),
        compiler_params=_cparams("parallel"),
        name="mod_vectors",
    )(c_rows, w, b.reshape(1, n))


def _top2_combine(u, w_ref, n_exp):
    logits = jnp.dot(u, w_ref[...], preferred_element_type=F32, precision=HIGHEST)
    lane = lax.broadcasted_iota(jnp.int32, logits.shape, 1)
    neg = jnp.float32(-jnp.inf)
    l1 = jnp.where(lane < n_exp, logits, neg)
    m1 = jnp.max(l1, axis=-1, keepdims=True)
    i1 = jnp.min(jnp.where(l1 == m1, lane, LANES), axis=-1, keepdims=True)
    l2 = jnp.where(lane == i1, neg, l1)
    m2 = jnp.max(l2, axis=-1, keepdims=True)
    i2 = jnp.min(jnp.where(l2 == m2, lane, LANES), axis=-1, keepdims=True)
    e2 = jnp.exp(m2 - m1)
    return jnp.where(lane == i1, 1.0 / (1.0 + e2), 0.0) + jnp.where(lane == i2, e2 / (1.0 + e2), 0.0)


def _ln_mod_kernel(*refs, tb, n_ctx, do_ln, do_mod, n_exp):
    refs = list(refs)
    r_ref = refs.pop(0)
    x = r_ref[...]
    if do_ln:
        g_ref, b_ref = refs.pop(0), refs.pop(0)
        mu = jnp.mean(x, axis=-1, keepdims=True)
        xc = x - mu
        var = jnp.mean(xc * xc, axis=-1, keepdims=True)
        x = xc * lax.rsqrt(var + LN_EPS) * g_ref[...] + b_ref[...]
    if do_mod:
        sh_ref, sc_ref = refs.pop(0), refs.pop(0)
    if n_exp:
        w_ref = refs.pop(0)
    if do_ln:
        refs.pop(0)[...] = x
    if do_mod:
        row0 = pl.program_id(0) * tb
        sh = _select_stream(sh_ref, x.shape, row0, n_ctx)
        sc = _select_stream(sc_ref, x.shape, row0, n_ctx)
        u = x * (1.0 + sc) + sh
        refs.pop(0)[...] = u.astype(BF16)
        if n_exp:
            refs.pop(0)[...] = _top2_combine(u, w_ref, n_exp)


def _ln_mod(r, ln, mod, k_shift, n_ctx, router=None):
    n, d = r.shape
    do_ln, do_mod = ln is not None, mod is not None
    n_exp = 0 if router is None else router.shape[1]
    tb = _pick(n, (256, 128, 64, 32, 16, 8))
    row = pl.BlockSpec((tb, d), lambda i: (i, 0))
    vec = pl.BlockSpec((1, d), lambda i: (0, 0))
    args, in_specs, out_specs, out_shape = [r], [row], [], []
    if do_ln:
        args += [ln[0].reshape(1, d), ln[1].reshape(1, d)]
        in_specs += [vec, vec]
        out_specs.append(row)
        out_shape.append(jax.ShapeDtypeStruct((n, d), F32))
    if do_mod:
        args += [mod, mod]
        in_specs += [pl.BlockSpec((SUBLANES, d), lambda i: (0, k_shift)),
                     pl.BlockSpec((SUBLANES, d), lambda i: (0, k_shift + 1))]
        out_specs.append(row)
        out_shape.append(jax.ShapeDtypeStruct((n, d), BF16))
    if n_exp:
        args.append(jnp.zeros((d, LANES), F32).at[:, :n_exp].set(router))
        in_specs.append(pl.BlockSpec((d, LANES), lambda i: (0, 0)))
        out_specs.append(pl.BlockSpec((tb, LANES), lambda i: (i, 0)))
        out_shape.append(jax.ShapeDtypeStruct((n, LANES), F32))
    outs = pl.pallas_call(
        functools.partial(_ln_mod_kernel, tb=tb, n_ctx=n_ctx, do_ln=do_ln, do_mod=do_mod, n_exp=n_exp),
        grid=(n // tb,),
        in_specs=in_specs,
        out_specs=out_specs,
        out_shape=out_shape,
        compiler_params=_cparams("parallel"),
        name="ln_mod",
    )(*args)
    outs = list(outs)
    h = outs.pop(0) if do_ln else r
    u = outs.pop(0) if do_mod else None
    return (h, u, outs.pop(0)) if n_exp else (h, u)


def _mm_kernel(*refs, kind, tm, n_ctx, nk):
    n_w = 2 if kind == "swiglu" else 1
    n_acc = n_w if nk > 1 else 0
    accs = refs[len(refs) - n_acc:]
    refs = refs[:len(refs) - n_acc]
    x_ref, w_refs, rest = refs[0], refs[1:1 + n_w], refs[1 + n_w:]
    o_ref = rest[-1]
    k = pl.program_id(2)

    def products():
        x = x_ref[...]
        return [_dot(x, w[...]) for w in w_refs]

    def finish(tot):
        if kind == "swiglu":
            c_ref = rest[0] if len(rest) == 2 else None
            for j in range(o_ref.shape[1] // LANES):
                sl = slice(j * LANES, (j + 1) * LANES)
                y = _silu(tot[0][:, sl]) * tot[1][:, sl]
                if c_ref is not None:
                    y = y * c_ref[...]
                o_ref[:, sl] = y.astype(o_ref.dtype)
        elif kind == "res":
            h_ref, gt_ref = rest[0], rest[1]
            gt = _select_stream(gt_ref, tot[0].shape, pl.program_id(0) * tm, n_ctx)
            o_ref[...] = ALPHA * h_ref[...] + gt * tot[0]
        else:
            o_ref[...] = tot[0].astype(o_ref.dtype)

    if nk == 1:
        finish(products())
        return

    @pl.when(k == 0)
    def _():
        for acc, y in zip(accs, products()):
            acc[...] = y

    @pl.when((k > 0) & (k < nk - 1))
    def _():
        for acc, y in zip(accs, products()):
            acc[...] += y

    @pl.when(k == nk - 1)
    def _():
        finish([acc[...] + y for acc, y in zip(accs, products())])


def _mm_tiles(m, n, k):
    tm = _pick(m, (1280, 1024, 512, 256, 128, 64, 32, 16, 8))
    if k > 2 * 4096:
        return tm, _pick(n, (1024, 512, 256, 128)), _pick(k, (2048, 1024, 512, 256, 128))
    tn = _pick(n, (512, 256, 128))
    tk = _pick(k, (4096, 2816, 2048, 1024, 512, 256, 128))
    return tm, tn, tk


def _mm_scratch(tm, tn, nk, n_w):
    return [pltpu.VMEM((tm, tn), F32)] * n_w if nk > 1 else []


def _mm_plain(x, w, out_dtype):
    m, kd = x.shape
    n = w.shape[1]
    tm, tn, tk = _mm_tiles(m, n, kd)
    nk = kd // tk
    return pl.pallas_call(
        functools.partial(_mm_kernel, kind="plain", tm=tm, n_ctx=0, nk=nk),
        grid=(m // tm, n // tn, nk),
        in_specs=[pl.BlockSpec((tm, tk), lambda i, j, k: (i, k)),
                  pl.BlockSpec((tk, tn), lambda i, j, k: (k, j))],
        out_specs=pl.BlockSpec((tm, tn), lambda i, j, k: (i, j)),
        out_shape=jax.ShapeDtypeStruct((m, n), out_dtype),
        scratch_shapes=_mm_scratch(tm, tn, nk, 1),
        compiler_params=_cparams("parallel", "parallel", "arbitrary"),
        name="mm_plain",
    )(x, w)


def _mm_res(x, w, h, mod, k_gate, n_ctx):
    m, kd = x.shape
    n = w.shape[1]
    tm, tn, tk = _mm_tiles(m, n, kd)
    nk = kd // tk
    gate_block0 = k_gate * (n // tn)
    return pl.pallas_call(
        functools.partial(_mm_kernel, kind="res", tm=tm, n_ctx=n_ctx, nk=nk),
        grid=(m // tm, n // tn, nk),
        in_specs=[pl.BlockSpec((tm, tk), lambda i, j, k: (i, k)),
                  pl.BlockSpec((tk, tn), lambda i, j, k: (k, j)),
                  pl.BlockSpec((tm, tn), lambda i, j, k: (i, j)),
                  pl.BlockSpec((SUBLANES, tn), lambda i, j, k: (0, gate_block0 + j))],
        out_specs=pl.BlockSpec((tm, tn), lambda i, j, k: (i, j)),
        out_shape=jax.ShapeDtypeStruct((m, n), F32),
        scratch_shapes=_mm_scratch(tm, tn, nk, 1),
        compiler_params=_cparams("parallel", "parallel", "arbitrary"),
        name="mm_res",
    )(x, w, h, mod)


def _mm_swiglu(x, w1, w3, cexp=None):
    m, kd = x.shape
    stacked = w1.ndim == 3
    f = w1.shape[-1]
    n_exp = w1.shape[0] if stacked else 1
    tm, tn, tk = _mm_tiles(m, f, kd)
    nk = kd // tk
    nf = f // tn
    if stacked:
        w_spec = pl.BlockSpec((None, tk, tn), lambda i, j, k: (j // nf, k, j % nf))
    else:
        w_spec = pl.BlockSpec((tk, tn), lambda i, j, k: (k, j))
    in_specs = [pl.BlockSpec((tm, tk), lambda i, j, k: (i, k)), w_spec, w_spec]
    args = [x, w1, w3]
    if cexp is not None:
        in_specs.append(pl.BlockSpec((None, tm, LANES), lambda i, j, k: (j // nf, i, 0)))
        args.append(cexp)

    return pl.pallas_call(
        functools.partial(_mm_kernel, kind="swiglu", tm=tm, n_ctx=0, nk=nk),
        grid=(m // tm, n_exp * nf, nk),
        in_specs=in_specs,
        out_specs=pl.BlockSpec((tm, tn), lambda i, j, k: (i, j)),
        out_shape=jax.ShapeDtypeStruct((m, n_exp * f), BF16),
        scratch_shapes=_mm_scratch(tm, tn, nk, 2),
        compiler_params=_cparams("parallel", "parallel", "arbitrary"),
        name="mm_swiglu",
    )(*args)


def _split3(x):
    hi = x.astype(BF16)
    r1 = x - hi.astype(F32)
    mid = r1.astype(BF16)
    lo = (r1 - mid.astype(F32)).astype(BF16)
    return hi, mid, lo


def _head_seg_matrix(width):
    r = lax.broadcasted_iota(jnp.int32, (LANES, LANES), 0) // width
    c = lax.broadcasted_iota(jnp.int32, (LANES, LANES), 1) // width
    return (r == c).astype(BF16)


def _segsum_tile(x, seg):
    hi, mid, lo = _split3(x)
    return _dot(hi, seg) + _dot(mid, seg) + _dot(lo, seg)


def _rwkv_prep_kernel(p_ref, prev_ref, next_ref, mu_ref, w0_ref, w2_ref, a0_ref, a2_ref, g2_ref,
                      kk_ref, ka_ref, rk_ref,
                      r_out, v_out, kap_out, g_out, bonus_out, lw_out, b_out, kd_out,
                      *, tb, n_ctx, n_rows, width, win_w, win_a, o_gl):
    x = p_ref[...]
    row = pl.program_id(0) * tb + lax.broadcasted_iota(jnp.int32, (tb, 1), 0)
    local = lax.broadcasted_iota(jnp.int32, (tb, 1), 0)
    prev = jnp.where(local == 0, prev_ref[SUBLANES - 1:SUBLANES, :], pltpu.roll(x, 1, 0))
    nxt = jnp.where(local == tb - 1, next_ref[0:1, :], pltpu.roll(x, tb - 1, 0))
    prev = jnp.where((row == 0) | (row == n_ctx), 0.0, prev)
    nxt = jnp.where((row == n_ctx - 1) | (row == n_rows - 1), 0.0, nxt)
    x = x + mu_ref[...] * (0.5 * (prev + nxt) - x)

    w = width
    r = x[:, 0:w]
    k = x[:, w:2 * w]
    v = x[:, 2 * w:3 * w]
    gl = x[:, o_gl:o_gl + A_GATE_LORA]
    wl = x[:, win_w[0]:win_w[0] + win_w[1]]
    al = x[:, win_a[0]:win_a[0] + win_a[1]]

    g_out[...] = _dot(_sigmoid(gl).astype(BF16), g2_ref[...])
    lw_lora = _dot(jnp.tanh(wl).astype(BF16), w2_ref[...])
    a_lora = _dot(al.astype(BF16), a2_ref[...])

    kk = k * kk_ref[...]
    seg = _head_seg_matrix(A_HEAD)
    rk = rk_ref[...]
    ka = ka_ref[...]
    kd_sum = jnp.zeros_like(k)
    kaps = []
    for j in range(w // LANES):
        sl = slice(j * LANES, (j + 1) * LANES)
        t = kk[:, sl]
        n2 = _segsum_tile(t * t, seg)
        kaps.append(t / jnp.maximum(jnp.sqrt(n2), 1e-12))
    kap = jnp.concatenate(kaps, axis=-1)
    for d in range(2):
        z = w0_ref[d:d + 1, :] + lw_lora[:, d * w:(d + 1) * w]
        softplus = jnp.maximum(-z, 0.0) + jnp.log(1.0 + jnp.exp(-jnp.abs(z)))
        wlog = -softplus - 0.5
        lw_out[d] = -jnp.exp(wlog)
        a = _sigmoid(a0_ref[d:d + 1, :] + a_lora[:, d * w:(d + 1) * w])
        b_out[d] = kap * a
        kd = k * (1.0 + (a - 1.0) * ka)
        kd_out[d] = kd
        kd_sum = kd_sum + kd
    bon = r * kd_sum * rk
    bons = []
    for j in range(w // LANES):
        sl = slice(j * LANES, (j + 1) * LANES)
        bons.append(_segsum_tile(bon[:, sl], seg))
    bonus_out[...] = jnp.concatenate(bons, axis=-1) * v
    r_out[...] = r
    v_out[...] = v
    kap_out[...] = kap


def _aligned_window(offset, width, limit):
    start = offset // LANES * LANES
    size = min(_round_up(offset + width - start, LANES), limit - start)
    return start, size


def _window_weight(blocks, offset, window, width):
    start, size = window
    lora = blocks.shape[1]
    wt = jnp.zeros((size, 2 * width), F32)
    for d in range(2):
        r0 = offset - start + d * lora
        wt = wt.at[r0:r0 + lora, d * width:(d + 1) * width].set(blocks[d])
    return wt.astype(BF16)


def _rwkv_prepare(pa, n_ctx, a_in, mu, w0, w2, a0, a2, g2, k_k, k_a, r_k):
    n = pa.shape[0]
    w = g2.shape[1]
    tb = _pick(n_ctx if n_ctx else n, (64, 32, 16, 8))
    nblk = n // tb
    o_gl = 3 * w
    o_wl = o_gl + A_GATE_LORA
    o_al = o_wl + 2 * A_DECAY_LORA
    win_w = _aligned_window(o_wl, 2 * A_DECAY_LORA, a_in)
    win_a = _aligned_window(o_al, 2 * A_AAA_LORA, a_in)
    w2w = _window_weight(w2, o_wl, win_w, w)
    a2w = _window_weight(a2, o_al, win_a, w)
    g = tb // SUBLANES
    n_groups = n // SUBLANES
    prev_spec = pl.BlockSpec((SUBLANES, a_in), lambda i: (jnp.maximum(i * g - 1, 0), 0))
    next_spec = pl.BlockSpec((SUBLANES, a_in), lambda i: (jnp.minimum((i + 1) * g, n_groups - 1), 0))

    tok = pl.BlockSpec((tb, w), lambda i: (i, 0))
    tok2 = pl.BlockSpec((2, tb, w), lambda i: (0, i, 0))

    def full(a):
        return pl.BlockSpec(a.shape, lambda i: (0,) * a.ndim)

    params = [mu.reshape(1, a_in), w0, w2w, a0, a2w, g2.astype(BF16), k_k.reshape(1, w), k_a.reshape(1, w),
              r_k.reshape(1, w)]
    one = jax.ShapeDtypeStruct((n, w), F32)
    two = jax.ShapeDtypeStruct((2, n, w), F32)
    return pl.pallas_call(
        functools.partial(_rwkv_prep_kernel, tb=tb, n_ctx=n_ctx, n_rows=n, width=w, win_w=win_w, win_a=win_a,
                          o_gl=o_gl),
        grid=(nblk,),
        in_specs=[pl.BlockSpec((tb, a_in), lambda i: (i, 0)), prev_spec, next_spec] + [full(a) for a in params],
        out_specs=[tok, tok, tok, tok, tok, tok2, tok2, tok2],
        out_shape=[one, one, one, one, one, two, two, two],
        compiler_params=_cparams("parallel"),
        name="rwkv_prepare",
    )(pa, pa, pa, *params)


def _chunk_order(d, c, n_ctx_chunks, n_chunks):
    back = jnp.where(c < n_ctx_chunks, n_ctx_chunks - 1 - c, n_chunks - 1 - (c - n_ctx_chunks))
    return jnp.where(d == 0, c, back)


def _split2(x):
    hi = x.astype(BF16)
    return hi, (x - hi.astype(F32)).astype(BF16)


def _dot_x3(a, b):
    a_hi, a_lo = _split2(a)
    b_hi, b_lo = _split2(b)
    return _dot(a_hi, b_hi) + (_dot(a_lo, b_hi) + _dot(a_hi, b_lo))


def _rwkv_scan_kernel(r_ref, kap_ref, v_ref, lw_ref, b_ref, k_ref, y_ref, s_ref, *, pairs):
    c_len = r_ref.shape[0]
    d = pl.program_id(0)
    sign = jnp.where(d == 1, -1, 1)

    @pl.when(pl.program_id(2) == 0)
    def _():
        s_ref[...] = jnp.zeros_like(s_ref)

    ti = lax.broadcasted_iota(jnp.int32, (c_len, c_len), 0)
    si = lax.broadcasted_iota(jnp.int32, (c_len, c_len), 1)
    upto = ((ti - si) * sign >= 0).astype(BF16)
    lane = lax.broadcasted_iota(jnp.int32, (c_len, PAIR), 1)
    head0 = lane < A_HEAD
    n2 = 2 * c_len
    row = lax.broadcasted_iota(jnp.int32, (n2, n2), 0)
    col = lax.broadcasted_iota(jnp.int32, (n2, n2), 1)
    same = (row // c_len) == (col // c_len)
    order = ((row % c_len) - (col % c_len)) * sign
    strict = same & (order > 0)
    incl = same & (order >= 0)
    eye = row == col

    def stack(x):
        return jnp.concatenate([jnp.where(head0, x, 0.0), jnp.where(head0, 0.0, x)], axis=0)

    def dup(x):
        return jnp.concatenate([x, x], axis=0)

    def bdot(a, b):
        return _dot(a.astype(BF16), b.astype(BF16))

    P = range(pairs)
    lanes = [slice(pi * PAIR, (pi + 1) * PAIR) for pi in P]
    lw = [lw_ref[:, ls] for ls in lanes]
    parts = [_split3(x) for x in lw]
    cl_in = [_dot(upto, hi) + (_dot(upto, mid) + _dot(upto, lo)) for hi, mid, lo in parts]
    cl_tot = [jnp.sum(x, axis=0, keepdims=True) for x in lw]
    g_in = [jnp.exp(c) for c in cl_in]
    g_ex = [jnp.exp(c - x) for c, x in zip(cl_in, lw)]
    g_inv = [jnp.exp(-c) for c in cl_in]
    g_end = [jnp.exp(t - c) for t, c in zip(cl_tot, cl_in)]

    ks = [stack(kap_ref[:, ls] * g).astype(BF16) for ls, g in zip(lanes, g_ex)]
    rs = [stack(r_ref[:, ls] * g) for ls, g in zip(lanes, g_in)]
    vs = [stack(v_ref[:, ls]).astype(BF16) for ls in lanes]
    bk = [jnp.concatenate([dup(b_ref[:, ls] * g), dup(k_ref[:, ls] * g)], axis=0).astype(BF16)
          for ls, g in zip(lanes, g_inv)]
    gram = [_dot_nt(jnp.concatenate([ks[i], rs[i].astype(BF16)], axis=0), bk[i]) for i in P]
    bhs_t = [stack(b_ref[:, ls] * g).T.astype(BF16) for ls, g in zip(lanes, g_end)]
    khs_t = [stack(k_ref[:, ls] * g).T.astype(BF16) for ls, g in zip(lanes, g_end)]
    n_ = [jnp.where(strict, gm[:n2, n2:], 0.0).astype(BF16) for gm in gram]
    lb = [jnp.where(incl, gm[n2:, :n2], 0.0).astype(BF16) for gm in gram]
    lk = [jnp.where(incl, gm[n2:, n2:], 0.0).astype(BF16) for gm in gram]

    pw = [jnp.where(strict, -gm[:n2, :n2], 0.0) for gm in gram]
    tinv = [jnp.where(eye, 1.0, 0.0) + x for x in pw]
    span = 2
    while span < c_len:
        pw = [bdot(x, x) for x in pw]
        tinv = [t + bdot(t, x) for t, x in zip(tinv, pw)]
        span *= 2

    nv = [_dot(n_[i], vs[i]) for i in P]
    kq_u = [bdot(tinv[i], jnp.concatenate([ks[i], nv[i].astype(BF16)], axis=1)).astype(BF16) for i in P]
    low = [_dot(bhs_t[i], kq_u[i]) for i in P]
    lbq = [_dot(lb[i], kq_u[i]) for i in P]
    kv = [_dot(khs_t[i], vs[i]) for i in P]
    lkv = [_dot(lk[i], vs[i]) for i in P]
    for i in P:
        a_ = jnp.where(eye, jnp.exp(cl_tot[i]), 0.0) - low[i][:, :PAIR]
        g_ = kv[i] - low[i][:, PAIR:]
        rq = rs[i] - lbq[i][:, :PAIR]
        yi = lkv[i] - lbq[i][:, PAIR:]
        prop = _dot_x3(jnp.concatenate([rq, a_], axis=0), s_ref[i])
        ys = prop[:n2] + yi
        y_ref[:, lanes[i]] = ys[:c_len] + ys[c_len:]
        s_ref[i] = prop[n2:] + g_


def _rwkv_scan(r, kap, v, lw, b, kd, n_ctx):
    n, w = r.shape
    c_len = RWKV_CHUNK
    n_chunks = n // c_len
    n_ctx_chunks = n_ctx // c_len
    pairs = _pick(w // PAIR, (16, 4, 2, 1))
    bw = pairs * PAIR

    def tok(d, p, c):
        return (_chunk_order(d, c, n_ctx_chunks, n_chunks), p)

    def tok2(d, p, c):
        return (d, _chunk_order(d, c, n_ctx_chunks, n_chunks), p)

    one = pl.BlockSpec((c_len, bw), tok)
    two = pl.BlockSpec((None, c_len, bw), tok2)
    return pl.pallas_call(
        functools.partial(_rwkv_scan_kernel, pairs=pairs),
        grid=(2, w // bw, n_chunks),
        in_specs=[one, one, one, two, two, two],
        out_specs=two,
        out_shape=jax.ShapeDtypeStruct((2, n, w), F32),
        scratch_shapes=[pltpu.VMEM((pairs, PAIR, PAIR), F32)],
        compiler_params=_cparams("parallel", "parallel", "arbitrary"),
        name="rwkv_scan",
    )(r, kap, v, lw, b, kd)


def _rope_tables(n_ctx, n_lat, half):
    nq = half // 2
    pos = jnp.arange(n_lat, dtype=jnp.int32)
    row = (pos // GRID_W).astype(F32)
    col = (pos % GRID_W).astype(F32)
    inv = ROPE_THETA ** (-jnp.arange(nq, dtype=F32) / nq)

    def part(p):
        ang = p[:, None] * inv[None, :]
        c, s = jnp.cos(ang), jnp.sin(ang)
        return jnp.concatenate([c, c], -1), jnp.concatenate([-s, s], -1)

    cr, sr = part(row)
    cc, sc = part(col)
    cos = jnp.concatenate([cr, cc], -1)
    sin = jnp.concatenate([sr, sc], -1)
    cos = jnp.concatenate([jnp.ones((n_ctx, 2 * half), F32), cos], 0)
    sin = jnp.concatenate([jnp.zeros((n_ctx, 2 * half), F32), sin], 0)
    return cos, sin


def _rope_tile(x, cos, sin, quarter):
    if 2 * quarter == LANES:
        partner = pltpu.roll(x, quarter, 1)
    else:
        lane = lax.broadcasted_iota(jnp.int32, x.shape, 1)
        lower = (lane % (2 * quarter)) < quarter
        partner = jnp.where(lower, pltpu.roll(x, LANES - quarter, 1), pltpu.roll(x, quarter, 1))
    return x * cos + partner * sin


def _retention_kernel(q_ref, k_ref, v_ref, cos_ref, sin_ref, lg_ref, y_ref, s_ref, *, heads):
    c_len = q_ref.shape[0]
    d = pl.program_id(0)
    rev = d == 1

    @pl.when(pl.program_id(1) == 0)
    def _():
        s_ref[...] = jnp.zeros_like(s_ref)

    cos = cos_ref[...]
    sin = sin_ref[...]

    def rope(x):
        tiles = [_rope_tile(x[:, j * LANES:(j + 1) * LANES], cos[:, j * LANES:(j + 1) * LANES],
                            sin[:, j * LANES:(j + 1) * LANES], LANES // 2) for j in range(B_HEAD // LANES)]
        return jnp.concatenate(tiles, axis=-1)

    n_i = lax.broadcasted_iota(jnp.int32, (c_len, c_len), 0)
    m_i = lax.broadcasted_iota(jnp.int32, (c_len, c_len), 1)
    dist = (n_i - m_i) * jnp.where(rev, -1, 1)
    seen = dist >= jnp.where(rev, 1, 0)
    dist_f = jnp.maximum(dist, 0).astype(F32)
    idx = lax.broadcasted_iota(jnp.int32, (c_len, B_HEAD), 0)
    order = jnp.where(rev, c_len - 1 - idx, idx).astype(F32)

    H = range(heads)
    cols = [slice(h * B_HEAD, (h + 1) * B_HEAD) for h in H]
    lg = [jnp.concatenate([lg_ref[h, 0:1, :]] * (B_HEAD // LANES), axis=-1) for h in H]
    lg_c = [jnp.concatenate([lg_ref[h, 0:1, :]] * (c_len // LANES), axis=-1) for h in H]
    q = [rope(q_ref[:, cs]) for cs in cols]
    k = [rope(k_ref[:, cs]) * (B_HEAD ** -0.5) for cs in cols]
    v = [v_ref[:, cs].astype(BF16) for cs in cols]
    inner = [_dot_nt(q[h].astype(BF16), k[h].astype(BF16)) for h in H]
    cross = [_dot((q[h] * jnp.exp(lg[h] * (order + 1.0))).astype(BF16), s_ref[h].astype(BF16)) for h in H]
    kt = [(k[h] * jnp.exp(lg[h] * (c_len - 1.0 - order))).T.astype(BF16) for h in H]
    upd = [_dot(kt[h], v[h]) for h in H]
    for h in H:
        decay = jnp.where(seen, jnp.exp(lg_c[h] * dist_f), 0.0)
        y_ref[:, cols[h]] = _dot((inner[h] * decay).astype(BF16), v[h]) + cross[h]
        s_ref[h] = s_ref[h] * jnp.exp(lg[h] * float(c_len)) + upd[h]


def _retention(pb, n_ctx, cos, sin):
    n = pb.shape[0]
    width = pb.shape[1] // 4
    heads = width // B_HEAD
    c_len = RET_CHUNK
    n_chunks = n // c_len
    n_ctx_chunks = n_ctx // c_len
    log_g = jnp.log1p(-jnp.exp2(-5.0 - jnp.arange(heads, dtype=F32)))
    lg = jnp.stack([log_g, log_g[::-1]])
    lg = jnp.broadcast_to(lg[:, :, None, None], (2, heads, SUBLANES, LANES))

    def part(which):
        return pl.BlockSpec((c_len, width), lambda d, c: (_chunk_order(d, c, n_ctx_chunks, n_chunks), which))

    tab = pl.BlockSpec((c_len, B_HEAD), lambda d, c: (_chunk_order(d, c, n_ctx_chunks, n_chunks), 0))
    return pl.pallas_call(
        functools.partial(_retention_kernel, heads=heads),
        grid=(2, n_chunks),
        in_specs=[part(0), part(1), part(2), tab, tab,
                  pl.BlockSpec((None, heads, SUBLANES, LANES), lambda d, c: (d, 0, 0, 0))],
        out_specs=pl.BlockSpec((None, c_len, width),
                               lambda d, c: (d, _chunk_order(d, c, n_ctx_chunks, n_chunks), 0)),
        out_shape=jax.ShapeDtypeStruct((2, n, width), F32),
        scratch_shapes=[pltpu.VMEM((heads, B_HEAD, B_HEAD), F32)],
        compiler_params=_cparams("parallel", "arbitrary"),
        name="retention",
    )(pb, pb, pb, cos, sin, lg)


def _mix_finish_kernel(ya_ref, bonus_ref, g_ref, yb_ref, gate_ref, lg_ref, lb_ref, gg_ref, gb_ref, o_ref):
    wa = bonus_ref.shape[1]
    ya = ya_ref[0] + ya_ref[1]
    seg = _head_seg_matrix(A_HEAD)
    inv = 1.0 / A_HEAD
    for j in range(wa // LANES):
        sl = slice(j * LANES, (j + 1) * LANES)
        t = ya[:, sl]
        mu = _segsum_tile(t, seg) * inv
        tc = t - mu
        var = _segsum_tile(tc * tc, seg) * inv
        o = tc * lax.rsqrt(var + A_GN_EPS) * lg_ref[:, sl] + lb_ref[:, sl] + bonus_ref[:, sl]
        o_ref[:, sl] = (o * g_ref[:, sl]).astype(o_ref.dtype)
    yb = yb_ref[0] + yb_ref[1]
    wb = yb.shape[1]
    for j in range(wb // B_HEAD):
        sl = slice(j * B_HEAD, (j + 1) * B_HEAD)
        t = yb[:, sl]
        mu = jnp.mean(t, axis=-1, keepdims=True)
        tc = t - mu
        var = jnp.mean(tc * tc, axis=-1, keepdims=True)
        o = tc * lax.rsqrt(var + B_GN_EPS) * gg_ref[:, sl] + gb_ref[:, sl]
        o_ref[:, wa + j * B_HEAD:wa + (j + 1) * B_HEAD] = (o * _silu(gate_ref[:, sl])).astype(o_ref.dtype)


def _mix_finish(ya, bonus, g, yb, pb, lnx_g, lnx_b, gn_g, gn_b):
    n, wa = bonus.shape
    wb = yb.shape[2]
    tb = _pick(n, (128, 64, 32, 16, 8))
    ra = pl.BlockSpec((tb, wa), lambda i: (i, 0))
    rb = pl.BlockSpec((tb, wb), lambda i: (i, 3))
    va = pl.BlockSpec((1, wa), lambda i: (0, 0))
    vb = pl.BlockSpec((1, wb), lambda i: (0, 0))
    return pl.pallas_call(
        _mix_finish_kernel,
        grid=(n // tb,),
        in_specs=[pl.BlockSpec((2, tb, wa), lambda i: (0, i, 0)), ra, ra,
                  pl.BlockSpec((2, tb, wb), lambda i: (0, i, 0)), rb, va, va, vb, vb],
        out_specs=pl.BlockSpec((tb, wa + wb), lambda i: (i, 0)),
        out_shape=jax.ShapeDtypeStruct((n, wa + wb), BF16),
        compiler_params=_cparams("parallel"),
        name="mix_finish",
    )(ya, bonus, g, yb, pb, lnx_g.reshape(1, wa), lnx_b.reshape(1, wa), gn_g.reshape(1, wb), gn_b.reshape(1, wb))


def _qkv_prep_kernel(x_ref, cos_ref, sin_ref, qg_ref, kg_ref, q_ref, k_ref, v_ref, *, q_heads, kv_heads):
    cos = cos_ref[...]
    sin = sin_ref[...]
    quarter = C_HEAD // 4
    lane = lax.broadcasted_iota(jnp.int32, cos.shape, 1)
    lower = (lane % (2 * quarter)) < quarter
    scale = C_HEAD ** -0.5 * LOG2_E
    n_heads = q_heads + kv_heads
    tiles = [x_ref[:, h * C_HEAD:(h + 1) * C_HEAD] for h in range(n_heads)]
    ms = [jnp.mean(t * t, axis=-1, keepdims=True) for t in tiles]
    gq = qg_ref[...] * scale
    normed = [t * lax.rsqrt(m + QK_EPS) * (gq if h < q_heads else kg_ref[...])
              for h, (t, m) in enumerate(zip(tiles, ms))]
    up = [pltpu.roll(t, LANES - quarter, 1) for t in normed]
    down = [pltpu.roll(t, quarter, 1) for t in normed]
    for h in range(n_heads):
        y = (normed[h] * cos + jnp.where(lower, up[h], down[h]) * sin).astype(BF16)
        if h < q_heads:
            q_ref[:, h * C_HEAD:(h + 1) * C_HEAD] = y
        else:
            k_ref[:, (h - q_heads) * C_HEAD:(h - q_heads + 1) * C_HEAD] = y
    v0 = n_heads * C_HEAD
    v_ref[...] = x_ref[:, v0:v0 + kv_heads * C_HEAD].astype(BF16)


def _qkv_prep(qkv, cos, sin, qn_g, kn_g, q_heads, kv_heads):
    n, c_in = qkv.shape
    tb = _pick(n, (128, 64, 32, 16, 8))
    kvw = kv_heads * C_HEAD
    return pl.pallas_call(
        functools.partial(_qkv_prep_kernel, q_heads=q_heads, kv_heads=kv_heads),
        grid=(n // tb,),
        in_specs=[pl.BlockSpec((tb, c_in), lambda i: (i, 0)),
                  pl.BlockSpec((tb, C_HEAD), lambda i: (i, 0)),
                  pl.BlockSpec((tb, C_HEAD), lambda i: (i, 0)),
                  pl.BlockSpec((1, C_HEAD), lambda i: (0, 0)),
                  pl.BlockSpec((1, C_HEAD), lambda i: (0, 0))],
        out_specs=[pl.BlockSpec((tb, q_heads * C_HEAD), lambda i: (i, 0)),
                   pl.BlockSpec((tb, kvw), lambda i: (i, 0)),
                   pl.BlockSpec((tb, kvw), lambda i: (i, 0))],
        out_shape=[jax.ShapeDtypeStruct((n, q_heads * C_HEAD), BF16),
                   jax.ShapeDtypeStruct((n, kvw), BF16),
                   jax.ShapeDtypeStruct((n, kvw), BF16)],
        compiler_params=_cparams("parallel"),
        name="qkv_prep",
    )(qkv, cos, sin, qn_g.reshape(1, C_HEAD), kn_g.reshape(1, C_HEAD))


def _flash_kernel(*refs, tq, tk, n_kv, n_cast):
    q_ref, k_ref, v_ref = refs[:3]
    cast_in = refs[3:3 + n_cast]
    o_ref = refs[3 + n_cast]
    cast_out = refs[4 + n_cast:4 + 2 * n_cast]
    q_scr, s_scr, p_scr, m_scr, l_scr, c_scr, acc_scr = refs[4 + 2 * n_cast:]
    for src, dst in zip(cast_in, cast_out):
        dst[...] = src[...].astype(BF16)
    nkv = n_kv // tk
    for h in range(C_GROUP):
        q_scr[h * tq:(h + 1) * tq, :] = q_ref[:, h * C_HEAD:(h + 1) * C_HEAD]
    m_scr[...] = jnp.full(m_scr.shape, -jnp.inf, F32)
    l_scr[...] = jnp.zeros(l_scr.shape, F32)
    acc_scr[...] = jnp.zeros(acc_scr.shape, F32)
    n_col = tk // LANES

    def scores(j, slot):
        s_scr[slot] = _dot_nt(q_scr[...], k_ref[pl.ds(pl.multiple_of(j * tk, tk), tk), :])

    def values(j, slot):
        return _dot(p_scr[slot], v_ref[pl.ds(pl.multiple_of(j * tk, tk), tk), :])

    def normalise(slot):
        tile_max = s_scr[slot, :, 0:LANES]
        for c in range(1, n_col):
            tile_max = jnp.maximum(tile_max, s_scr[slot, :, c * LANES:(c + 1) * LANES])
        row_max = jnp.broadcast_to(jnp.max(tile_max, axis=-1, keepdims=True), tile_max.shape)
        m_old = m_scr[...]
        m_new = jnp.maximum(m_old, row_max)
        corr = jnp.exp2(m_old - m_new)
        part = None
        for c in range(n_col):
            p = jnp.exp2(s_scr[slot, :, c * LANES:(c + 1) * LANES] - m_new)
            part = p if part is None else part + p
            p_scr[slot, :, c * LANES:(c + 1) * LANES] = p.astype(BF16)
        l_scr[...] = corr * l_scr[...] + part
        m_scr[...] = m_new
        c_scr[...] = corr

    scores(0, 0)
    if nkv > 1:
        scores(1, 1)
    normalise(0)

    def step(j, cur):
        scores(j + 1, 1 - cur)
        normalise(cur)
        acc_scr[...] = (acc_scr[...] + values(j - 1, 1 - cur)) * c_scr[...]

    def pair(t, carry):
        step(2 * t + 1, 1)
        step(2 * t + 2, 0)
        return carry

    n_mid = max(nkv - 2, 0)
    if n_mid >= 2:
        lax.fori_loop(0, n_mid // 2, pair, 0)
    if n_mid % 2:
        step(nkv - 2, (nkv - 2) % 2)
    if nkv > 1:
        last = (nkv - 1) % 2
        normalise(last)
        acc_scr[...] = (acc_scr[...] + values(nkv - 2, 1 - last)) * c_scr[...]
    acc = acc_scr[...] + values(nkv - 1, (nkv - 1) % 2)
    out = acc / jnp.sum(l_scr[...], axis=-1, keepdims=True)
    for h in range(C_GROUP):
        o_ref[:, h * C_HEAD:(h + 1) * C_HEAD] = out[h * tq:(h + 1) * tq].astype(o_ref.dtype)


CAST_COLS = 8 * LANES


def _flash(q, k, v, n_ctx, cast=()):
    n = q.shape[0]
    kv_heads = k.shape[1] // C_HEAD
    n_lat = n - n_ctx
    tq = _pick(n_ctx, (256, 128, 64, 32, 16, 8))
    tk = _pick(n, (1280, 1024, 640, 512, 256, 128, 64, 32, 16, 8))
    gw = C_GROUP * C_HEAD
    q0 = n_ctx // tq
    rows = C_GROUP * tq
    n_qb = n_lat // tq
    steps = kv_heads * n_qb
    slabs = [a.size // (steps * CAST_COLS) for a in cast]
    if not all(a.size == sl * steps * CAST_COLS and sl % (2 * SUBLANES) == 0 for a, sl in zip(cast, slabs)):
        return _flash(q, k, v, n_ctx)[0], [a.astype(BF16) for a in cast]
    cast_specs = [pl.BlockSpec((sl, CAST_COLS), lambda g, i: (g * n_qb + i, 0)) for sl in slabs]
    outs = pl.pallas_call(
        functools.partial(_flash_kernel, tq=tq, tk=tk, n_kv=n, n_cast=len(cast)),
        grid=(kv_heads, n_qb),
        in_specs=[pl.BlockSpec((tq, gw), lambda g, i: (q0 + i, g)),
                  pl.BlockSpec((n, C_HEAD), lambda g, i: (0, g)),
                  pl.BlockSpec((n, C_HEAD), lambda g, i: (0, g))] + cast_specs,
        out_specs=[pl.BlockSpec((tq, gw), lambda g, i: (i, g))] + cast_specs,
        out_shape=[jax.ShapeDtypeStruct((n_lat, kv_heads * gw), BF16)]
        + [jax.ShapeDtypeStruct((sl * steps, CAST_COLS), BF16) for sl in slabs],
        scratch_shapes=[pltpu.VMEM((rows, C_HEAD), BF16),
                        pltpu.VMEM((2, rows, tk), F32),
                        pltpu.VMEM((2, rows, tk), BF16),
                        pltpu.VMEM((rows, LANES), F32),
                        pltpu.VMEM((rows, LANES), F32),
                        pltpu.VMEM((rows, LANES), F32),
                        pltpu.VMEM((rows, C_HEAD), F32)],
        compiler_params=_cparams("parallel", "parallel"),
        name="flash_gqa",
    )(q, k, v, *[a.reshape(sl * steps, CAST_COLS) for a, sl in zip(cast, slabs)])
    return outs[0], [o.reshape(a.shape) for o, a in zip(outs[1:], cast)]


def _pad_cols(w, n):
    return jnp.pad(w, ((0, 0), (0, n - w.shape[1])))


def kernel(x, c, ctx, c_ctx, l0_mod_w, l0_mod_b, l0_w_in, l0_a_mu, l0_a_w0, l0_a_w2, l0_a_a0, l0_a_a2, l0_a_g2, l0_a_k_k, l0_a_k_a, l0_a_r_k, l0_a_lnx_g, l0_a_lnx_b, l0_b_gn_g, l0_b_gn_b, l0_w_out, l0_ln1_g, l0_ln1_b, l0_ffn_w1, l0_ffn_w3, l0_ffn_w2, l0_ln2_g, l0_ln2_b, l1_mod_w, l1_mod_b, l1_w_qkv, l1_q_norm_g, l1_k_norm_g, l1_w_out, l1_ln1_g, l1_ln1_b, l1_router, l1_moe_w1, l1_moe_w3, l1_moe_w2, l1_ln2_g, l1_ln2_b):
    batch, n_lat, d = x.shape
    n_ctx = ctx.shape[1]
    assert batch == 1 and c.shape[0] == 1
    a_width = l0_a_g2.shape[1]
    a_in = l0_a_mu.shape[0]
    b_width = l0_b_gn_g.shape[0]

    c_rows = jnp.zeros((SUBLANES, d), F32).at[0].set(c[0]).at[1].set(c_ctx)
    mod0 = _mod_vectors(c_rows, l0_mod_w, l0_mod_b)
    mod1 = _mod_vectors(c_rows, l1_mod_w, l1_mod_b)

    h = jnp.concatenate([ctx[0], x[0]], axis=0)

    _, u = _ln_mod(h, None, mod0, 0, n_ctx)
    a_pad = _round_up(a_in, 1024) if a_in > 1024 else a_in
    w_in_a = _pad_cols(l0_w_in[:, :a_in], a_pad).astype(BF16)
    w_in_b = l0_w_in[:, a_in:].astype(BF16)
    pa = _mm_plain(u, w_in_a, F32)
    pb = _mm_plain(u, w_in_b, F32)
    r, v, kap, g, bonus, lw, b, kd = _rwkv_prepare(pa, n_ctx, a_in, l0_a_mu, l0_a_w0, l0_a_w2, l0_a_a0, l0_a_a2,
                                                   l0_a_g2, l0_a_k_k, l0_a_k_a, l0_a_r_k)
    ya = _rwkv_scan(r, kap, v, lw, b, kd, n_ctx)
    cos_b, sin_b = _rope_tables(n_ctx, n_lat, B_HEAD // 2)
    yb = _retention(pb, n_ctx, cos_b, sin_b)
    mix = _mix_finish(ya, bonus, g, yb, pb, l0_a_lnx_g, l0_a_lnx_b, l0_b_gn_g, l0_b_gn_b)
    res = _mm_res(mix, l0_w_out.astype(BF16), h, mod0, 2, n_ctx)
    h, u = _ln_mod(res, (l0_ln1_g, l0_ln1_b), mod0, 3, n_ctx)
    d_ff = l0_ffn_w1.shape[1]
    ff_pad = _round_up(d_ff, 1024) if d_ff > 1024 else d_ff
    hid = _mm_swiglu(u, _pad_cols(l0_ffn_w1, ff_pad).astype(BF16), _pad_cols(l0_ffn_w3, ff_pad).astype(BF16))
    w2 = jnp.pad(l0_ffn_w2, ((0, ff_pad - d_ff), (0, 0))).astype(BF16)
    res = _mm_res(hid, w2, h, mod0, 5, n_ctx)
    h, u = _ln_mod(res, (l0_ln2_g, l0_ln2_b), mod1, 0, n_ctx)

    q_heads = l1_w_out.shape[0] // C_HEAD
    kv_heads = q_heads // C_GROUP
    qkv = _mm_plain(u, l1_w_qkv.astype(BF16), F32)
    cos_c, sin_c = _rope_tables(n_ctx, n_lat, C_HEAD // 2)
    q, k, vv = _qkv_prep(qkv, cos_c, sin_c, l1_q_norm_g, l1_k_norm_g, q_heads, kv_heads)
    att, (moe_w1, moe_w3, moe_w2) = _flash(q, k, vv, n_ctx, cast=(l1_moe_w1, l1_moe_w3, l1_moe_w2))
    h_lat = h[n_ctx:]
    res = _mm_res(att, l1_w_out.astype(BF16), h_lat, mod1, 2, 0)
    h_lat, u, comb = _ln_mod(res, (l1_ln1_g, l1_ln1_b), mod1, 3, 0, router=l1_router)
    n_exp = l1_router.shape[1]
    cexp = jnp.broadcast_to(comb[:, :n_exp].T[:, :, None], (n_exp, n_lat, LANES))
    hid = _mm_swiglu(u, moe_w1, moe_w3, cexp)
    w2 = moe_w2.reshape(n_exp * moe_w2.shape[1], d)
    res = _mm_res(hid, w2, h_lat, mod1, 5, 0)
    out, _ = _ln_mod(res, (l1_ln2_g, l1_ln2_b), None, 0, 0)
    return out[None]
```

```python
import functools

import jax
import jax.numpy as jnp
from jax import lax
from jax.experimental import pallas as pl
from jax.experimental.pallas import tpu as pltpu

F32 = jnp.float32
BF16 = jnp.bfloat16
HIGHEST = lax.Precision.HIGHEST

GRID_W = 64
DEPTH = 2
ALPHA = (2 * DEPTH) ** 0.25
LN_EPS = 1e-6
ROPE_THETA = 10000.0
A_HEAD = 64
A_DECAY_LORA = 96
A_AAA_LORA = 96
A_GATE_LORA = 256
A_GN_EPS = 64e-5
B_HEAD = 256
RET_CHUNK = 128
B_GN_EPS = 1e-5
C_HEAD = 128
C_GROUP = 4
QK_EPS = 1e-6
LOG2_E = 1.4426950408889634

LANES = 128
SUBLANES = 8
VMEM_LIMIT_BYTES = 60 * 1024 * 1024

RWKV_CHUNK = 64
PAIR = 2 * A_HEAD


def _cparams(*sem):
    return pltpu.CompilerParams(dimension_semantics=sem, vmem_limit_bytes=VMEM_LIMIT_BYTES)


def _pick(n, candidates):
    for c in candidates:
        if c <= n and n % c == 0:
            return c
    return n


def _round_up(n, m):
    return (n + m - 1) // m * m


def _silu(x):
    return x * (1.0 / (1.0 + jnp.exp(-x)))


def _sigmoid(x):
    return 1.0 / (1.0 + jnp.exp(-x))


def _dot(a, b):
    return jnp.dot(a, b, preferred_element_type=F32)


def _dot_nt(a, b, precision=None):
    return lax.dot_general(a, b, (((1,), (1,)), ((), ())), preferred_element_type=F32, precision=precision)


def _row_is_ctx(shape, row0, n_ctx):
    rows = row0 + lax.broadcasted_iota(jnp.int32, shape, 0)
    return rows < n_ctx


def _select_stream(mod_ref, shape, row0, n_ctx):
    lat = mod_ref[0:1, :]
    if n_ctx == 0:
        return jnp.broadcast_to(lat, shape)
    return jnp.where(_row_is_ctx(shape, row0, n_ctx), mod_ref[1:2, :], lat)


def _modvec_kernel(c_ref, w_ref, b_ref, o_ref):
    s = _silu(c_ref[...])
    o_ref[...] = jnp.dot(s, w_ref[...], preferred_element_type=F32, precision=HIGHEST) + b_ref[...]


def _mod_vectors(c_rows, w, b):
    d, n = w.shape
    tn = _pick(n, (512, 256, 128))
    return pl.pallas_call(
        _modvec_kernel,
        grid=(n // tn,),
        in_specs=[pl.BlockSpec((SUBLANES, d), lambda j: (0, 0)),
                  pl.BlockSpec((d, tn), lambda j: (0, j)),
                  pl.BlockSpec((1, tn), lambda j: (0, j))],
        out_specs=pl.BlockSpec((SUBLANES, tn), lambda j: (0, j)),
        out_shape=jax.ShapeDtypeStruct((SUBLANES, n), F32),
        compiler_params=_cparams("parallel"),
        name="mod_vectors",
    )(c_rows, w, b.reshape(1, n))


def _top2_combine(u, w_ref, n_exp):
    logits = jnp.dot(u, w_ref[...], preferred_element_type=F32, precision=HIGHEST)
    lane = lax.broadcasted_iota(jnp.int32, logits.shape, 1)
    neg = jnp.float32(-jnp.inf)
    l1 = jnp.where(lane < n_exp, logits, neg)
    m1 = jnp.max(l1, axis=-1, keepdims=True)
    i1 = jnp.min(jnp.where(l1 == m1, lane, LANES), axis=-1, keepdims=True)
    l2 = jnp.where(lane == i1, neg, l1)
    m2 = jnp.max(l2, axis=-1, keepdims=True)
    i2 = jnp.min(jnp.where(l2 == m2, lane, LANES), axis=-1, keepdims=True)
    e2 = jnp.exp(m2 - m1)
    return jnp.where(lane == i1, 1.0 / (1.0 + e2), 0.0) + jnp.where(lane == i2, e2 / (1.0 + e2), 0.0)


def _ln_mod_kernel(*refs, tb, n_ctx, do_ln, do_mod, n_exp):
    refs = list(refs)
    r_ref = refs.pop(0)
    x = r_ref[...]
    if do_ln:
        g_ref, b_ref = refs.pop(0), refs.pop(0)
        mu = jnp.mean(x, axis=-1, keepdims=True)
        xc = x - mu
        var = jnp.mean(xc * xc, axis=-1, keepdims=True)
        x = xc * lax.rsqrt(var + LN_EPS) * g_ref[...] + b_ref[...]
    if do_mod:
        sh_ref, sc_ref = refs.pop(0), refs.pop(0)
    if n_exp:
        w_ref = refs.pop(0)
    if do_ln:
        refs.pop(0)[...] = x
    if do_mod:
        row0 = pl.program_id(0) * tb
        sh = _select_stream(sh_ref, x.shape, row0, n_ctx)
        sc = _select_stream(sc_ref, x.shape, row0, n_ctx)
        u = x * (1.0 + sc) + sh
        refs.pop(0)[...] = u.astype(BF16)
        if n_exp:
            refs.pop(0)[...] = _top2_combine(u, w_ref, n_exp)


def _ln_mod(r, ln, mod, k_shift, n_ctx, router=None):
    n, d = r.shape
    do_ln, do_mod = ln is not None, mod is not None
    n_exp = 0 if router is None else router.shape[1]
    tb = _pick(n, (256, 128, 64, 32, 16, 8))
    row = pl.BlockSpec((tb, d), lambda i: (i, 0))
    vec = pl.BlockSpec((1, d), lambda i: (0, 0))
    args, in_specs, out_specs, out_shape = [r], [row], [], []
    if do_ln:
        args += [ln[0].reshape(1, d), ln[1].reshape(1, d)]
        in_specs += [vec, vec]
        out_specs.append(row)
        out_shape.append(jax.ShapeDtypeStruct((n, d), F32))
    if do_mod:
        args += [mod, mod]
        in_specs += [pl.BlockSpec((SUBLANES, d), lambda i: (0, k_shift)),
                     pl.BlockSpec((SUBLANES, d), lambda i: (0, k_shift + 1))]
        out_specs.append(row)
        out_shape.append(jax.ShapeDtypeStruct((n, d), BF16))
    if n_exp:
        args.append(jnp.zeros((d, LANES), F32).at[:, :n_exp].set(router))
        in_specs.append(pl.BlockSpec((d, LANES), lambda i: (0, 0)))
        out_specs.append(pl.BlockSpec((tb, LANES), lambda i: (i, 0)))
        out_shape.append(jax.ShapeDtypeStruct((n, LANES), F32))
    outs = pl.pallas_call(
        functools.partial(_ln_mod_kernel, tb=tb, n_ctx=n_ctx, do_ln=do_ln, do_mod=do_mod, n_exp=n_exp),
        grid=(n // tb,),
        in_specs=in_specs,
        out_specs=out_specs,
        out_shape=out_shape,
        compiler_params=_cparams("parallel"),
        name="ln_mod",
    )(*args)
    outs = list(outs)
    h = outs.pop(0) if do_ln else r
    u = outs.pop(0) if do_mod else None
    return (h, u, outs.pop(0)) if n_exp else (h, u)


def _mm_kernel(*refs, kind, tm, n_ctx, nk):
    n_w = 2 if kind == "swiglu" else 1
    n_acc = n_w if nk > 1 else 0
    accs = refs[len(refs) - n_acc:]
    refs = refs[:len(refs) - n_acc]
    x_ref, w_refs, rest = refs[0], refs[1:1 + n_w], refs[1 + n_w:]
    o_ref = rest[-1]
    k = pl.program_id(2)

    def products():
        x = x_ref[...]
        return [_dot(x, w[...]) for w in w_refs]

    def finish(tot):
        if kind == "swiglu":
            c_ref = rest[0] if len(rest) == 2 else None
            for j in range(o_ref.shape[1] // LANES):
                sl = slice(j * LANES, (j + 1) * LANES)
                y = _silu(tot[0][:, sl]) * tot[1][:, sl]
                if c_ref is not None:
                    y = y * c_ref[...]
                o_ref[:, sl] = y.astype(o_ref.dtype)
        elif kind == "res":
            h_ref, gt_ref = rest[0], rest[1]
            gt = _select_stream(gt_ref, tot[0].shape, pl.program_id(0) * tm, n_ctx)
            o_ref[...] = ALPHA * h_ref[...] + gt * tot[0]
        else:
            o_ref[...] = tot[0].astype(o_ref.dtype)

    if nk == 1:
        finish(products())
        return

    @pl.when(k == 0)
    def _():
        for acc, y in zip(accs, products()):
            acc[...] = y

    @pl.when((k > 0) & (k < nk - 1))
    def _():
        for acc, y in zip(accs, products()):
            acc[...] += y

    @pl.when(k == nk - 1)
    def _():
        finish([acc[...] + y for acc, y in zip(accs, products())])


def _mm_tiles(m, n, k):
    tm = _pick(m, (1280, 1024, 512, 256, 128, 64, 32, 16, 8))
    if k > 2 * 4096:
        return tm, _pick(n, (1024, 512, 256, 128)), _pick(k, (2048, 1024, 512, 256, 128))
    tn = _pick(n, (512, 256, 128))
    tk = _pick(k, (4096, 2816, 2048, 1024, 512, 256, 128))
    return tm, tn, tk


def _mm_scratch(tm, tn, nk, n_w):
    return [pltpu.VMEM((tm, tn), F32)] * n_w if nk > 1 else []


def _mm_plain(x, w, out_dtype):
    m, kd = x.shape
    n = w.shape[1]
    tm, tn, tk = _mm_tiles(m, n, kd)
    nk = kd // tk
    return pl.pallas_call(
        functools.partial(_mm_kernel, kind="plain", tm=tm, n_ctx=0, nk=nk),
        grid=(m // tm, n // tn, nk),
        in_specs=[pl.BlockSpec((tm, tk), lambda i, j, k: (i, k)),
                  pl.BlockSpec((tk, tn), lambda i, j, k: (k, j))],
        out_specs=pl.BlockSpec((tm, tn), lambda i, j, k: (i, j)),
        out_shape=jax.ShapeDtypeStruct((m, n), out_dtype),
        scratch_shapes=_mm_scratch(tm, tn, nk, 1),
        compiler_params=_cparams("parallel", "parallel", "arbitrary"),
        name="mm_plain",
    )(x, w)


def _mm_res(x, w, h, mod, k_gate, n_ctx):
    m, kd = x.shape
    n = w.shape[1]
    tm, tn, tk = _mm_tiles(m, n, kd)
    nk = kd // tk
    gate_block0 = k_gate * (n // tn)
    return pl.pallas_call(
        functools.partial(_mm_kernel, kind="res", tm=tm, n_ctx=n_ctx, nk=nk),
        grid=(m // tm, n // tn, nk),
        in_specs=[pl.BlockSpec((tm, tk), lambda i, j, k: (i, k)),
                  pl.BlockSpec((tk, tn), lambda i, j, k: (k, j)),
                  pl.BlockSpec((tm, tn), lambda i, j, k: (i, j)),
                  pl.BlockSpec((SUBLANES, tn), lambda i, j, k: (0, gate_block0 + j))],
        out_specs=pl.BlockSpec((tm, tn), lambda i, j, k: (i, j)),
        out_shape=jax.ShapeDtypeStruct((m, n), F32),
        scratch_shapes=_mm_scratch(tm, tn, nk, 1),
        compiler_params=_cparams("parallel", "parallel", "arbitrary"),
        name="mm_res",
    )(x, w, h, mod)


def _mm_swiglu(x, w1, w3, cexp=None):
    m, kd = x.shape
    stacked = w1.ndim == 3
    f = w1.shape[-1]
    n_exp = w1.shape[0] if stacked else 1
    tm, tn, tk = _mm_tiles(m, f, kd)
    nk = kd // tk
    nf = f // tn
    if stacked:
        w_spec = pl.BlockSpec((None, tk, tn), lambda i, j, k: (j // nf, k, j % nf))
    else:
        w_spec = pl.BlockSpec((tk, tn), lambda i, j, k: (k, j))
    in_specs = [pl.BlockSpec((tm, tk), lambda i, j, k: (i, k)), w_spec, w_spec]
    args = [x, w1, w3]
    if cexp is not None:
        in_specs.append(pl.BlockSpec((None, tm, LANES), lambda i, j, k: (j // nf, i, 0)))
        args.append(cexp)

    return pl.pallas_call(
        functools.partial(_mm_kernel, kind="swiglu", tm=tm, n_ctx=0, nk=nk),
        grid=(m // tm, n_exp * nf, nk),
        in_specs=in_specs,
        out_specs=pl.BlockSpec((tm, tn), lambda i, j, k: (i, j)),
        out_shape=jax.ShapeDtypeStruct((m, n_exp * f), BF16),
        scratch_shapes=_mm_scratch(tm, tn, nk, 2),
        compiler_params=_cparams("parallel", "parallel", "arbitrary"),
        name="mm_swiglu",
    )(*args)


def _split3(x):
    hi = x.astype(BF16)
    r1 = x - hi.astype(F32)
    mid = r1.astype(BF16)
    lo = (r1 - mid.astype(F32)).astype(BF16)
    return hi, mid, lo


def _head_seg_matrix(width):
    r = lax.broadcasted_iota(jnp.int32, (LANES, LANES), 0) // width
    c = lax.broadcasted_iota(jnp.int32, (LANES, LANES), 1) // width
    return (r == c).astype(BF16)


def _segsum_tile(x, seg):
    hi, mid, lo = _split3(x)
    return _dot(hi, seg) + _dot(mid, seg) + _dot(lo, seg)


def _rwkv_prep_kernel(p_ref, prev_ref, next_ref, mu_ref, w0_ref, w2_ref, a0_ref, a2_ref, g2_ref,
                      kk_ref, ka_ref, rk_ref,
                      r_out, v_out, kap_out, g_out, bonus_out, lw_out, b_out, kd_out,
                      *, tb, n_ctx, n_rows, width, win_w, win_a, o_gl):
    x = p_ref[...]
    row = pl.program_id(0) * tb + lax.broadcasted_iota(jnp.int32, (tb, 1), 0)
    local = lax.broadcasted_iota(jnp.int32, (tb, 1), 0)
    prev = jnp.where(local == 0, prev_ref[SUBLANES - 1:SUBLANES, :], pltpu.roll(x, 1, 0))
    nxt = jnp.where(local == tb - 1, next_ref[0:1, :], pltpu.roll(x, tb - 1, 0))
    prev = jnp.where((row == 0) | (row == n_ctx), 0.0, prev)
    nxt = jnp.where((row == n_ctx - 1) | (row == n_rows - 1), 0.0, nxt)
    x = x + mu_ref[...] * (0.5 * (prev + nxt) - x)

    w = width
    r = x[:, 0:w]
    k = x[:, w:2 * w]
    v = x[:, 2 * w:3 * w]
    gl = x[:, o_gl:o_gl + A_GATE_LORA]
    wl = x[:, win_w[0]:win_w[0] + win_w[1]]
    al = x[:, win_a[0]:win_a[0] + win_a[1]]

    g_out[...] = _dot(_sigmoid(gl).astype(BF16), g2_ref[...])
    lw_lora = _dot(jnp.tanh(wl).astype(BF16), w2_ref[...])
    a_lora = _dot(al.astype(BF16), a2_ref[...])

    kk = k * kk_ref[...]
    seg = _head_seg_matrix(A_HEAD)
    rk = rk_ref[...]
    ka = ka_ref[...]
    kd_sum = jnp.zeros_like(k)
    kaps = []
    for j in range(w // LANES):
        sl = slice(j * LANES, (j + 1) * LANES)
        t = kk[:, sl]
        n2 = _segsum_tile(t * t, seg)
        kaps.append(t / jnp.maximum(jnp.sqrt(n2), 1e-12))
    kap = jnp.concatenate(kaps, axis=-1)
    for d in range(2):
        z = w0_ref[d:d + 1, :] + lw_lora[:, d * w:(d + 1) * w]
        softplus = jnp.maximum(-z, 0.0) + jnp.log(1.0 + jnp.exp(-jnp.abs(z)))
        wlog = -softplus - 0.5
        lw_out[d] = -jnp.exp(wlog)
        a = _sigmoid(a0_ref[d:d + 1, :] + a_lora[:, d * w:(d + 1) * w])
        b_out[d] = kap * a
        kd = k * (1.0 + (a - 1.0) * ka)
        kd_out[d] = kd
        kd_sum = kd_sum + kd
    bon = r * kd_sum * rk
    bons = []
    for j in range(w // LANES):
        sl = slice(j * LANES, (j + 1) * LANES)
        bons.append(_segsum_tile(bon[:, sl], seg))
    bonus_out[...] = jnp.concatenate(bons, axis=-1) * v
    r_out[...] = r
    v_out[...] = v
    kap_out[...] = kap


def _aligned_window(offset, width, limit):
    start = offset // LANES * LANES
    size = min(_round_up(offset + width - start, LANES), limit - start)
    return start, size


def _window_weight(blocks, offset, window, width):
    start, size = window
    lora = blocks.shape[1]
    wt = jnp.zeros((size, 2 * width), F32)
    for d in range(2):
        r0 = offset - start + d * lora
        wt = wt.at[r0:r0 + lora, d * width:(d + 1) * width].set(blocks[d])
    return wt.astype(BF16)


def _rwkv_prepare(pa, n_ctx, a_in, mu, w0, w2, a0, a2, g2, k_k, k_a, r_k):
    n = pa.shape[0]
    w = g2.shape[1]
    tb = _pick(n_ctx if n_ctx else n, (64, 32, 16, 8))
    nblk = n // tb
    o_gl = 3 * w
    o_wl = o_gl + A_GATE_LORA
    o_al = o_wl + 2 * A_DECAY_LORA
    win_w = _aligned_window(o_wl, 2 * A_DECAY_LORA, a_in)
    win_a = _aligned_window(o_al, 2 * A_AAA_LORA, a_in)
    w2w = _window_weight(w2, o_wl, win_w, w)
    a2w = _window_weight(a2, o_al, win_a, w)
    g = tb // SUBLANES
    n_groups = n // SUBLANES
    prev_spec = pl.BlockSpec((SUBLANES, a_in), lambda i: (jnp.maximum(i * g - 1, 0), 0))
    next_spec = pl.BlockSpec((SUBLANES, a_in), lambda i: (jnp.minimum((i + 1) * g, n_groups - 1), 0))

    tok = pl.BlockSpec((tb, w), lambda i: (i, 0))
    tok2 = pl.BlockSpec((2, tb, w), lambda i: (0, i, 0))

    def full(a):
        return pl.BlockSpec(a.shape, lambda i: (0,) * a.ndim)

    params = [mu.reshape(1, a_in), w0, w2w, a0, a2w, g2.astype(BF16), k_k.reshape(1, w), k_a.reshape(1, w),
              r_k.reshape(1, w)]
    one = jax.ShapeDtypeStruct((n, w), F32)
    two = jax.ShapeDtypeStruct((2, n, w), F32)
    return pl.pallas_call(
        functools.partial(_rwkv_prep_kernel, tb=tb, n_ctx=n_ctx, n_rows=n, width=w, win_w=win_w, win_a=win_a,
                          o_gl=o_gl),
        grid=(nblk,),
        in_specs=[pl.BlockSpec((tb, a_in), lambda i: (i, 0)), prev_spec, next_spec] + [full(a) for a in params],
        out_specs=[tok, tok, tok, tok, tok, tok2, tok2, tok2],
        out_shape=[one, one, one, one, one, two, two, two],
        compiler_params=_cparams("parallel"),
        name="rwkv_prepare",
    )(pa, pa, pa, *params)


def _chunk_order(d, c, n_ctx_chunks, n_chunks):
    back = jnp.where(c < n_ctx_chunks, n_ctx_chunks - 1 - c, n_chunks - 1 - (c - n_ctx_chunks))
    return jnp.where(d == 0, c, back)


def _split2(x):
    hi = x.astype(BF16)
    return hi, (x - hi.astype(F32)).astype(BF16)


def _dot_x3(a, b):
    a_hi, a_lo = _split2(a)
    b_hi, b_lo = _split2(b)
    return _dot(a_hi, b_hi) + (_dot(a_lo, b_hi) + _dot(a_hi, b_lo))


def _rwkv_scan_kernel(r_ref, kap_ref, v_ref, lw_ref, b_ref, k_ref, y_ref, s_ref, *, pairs):
    c_len = r_ref.shape[0]
    d = pl.program_id(0)
    sign = jnp.where(d == 1, -1, 1)

    @pl.when(pl.program_id(2) == 0)
    def _():
        s_ref[...] = jnp.zeros_like(s_ref)

    ti = lax.broadcasted_iota(jnp.int32, (c_len, c_len), 0)
    si = lax.broadcasted_iota(jnp.int32, (c_len, c_len), 1)
    upto = ((ti - si) * sign >= 0).astype(BF16)
    lane = lax.broadcasted_iota(jnp.int32, (c_len, PAIR), 1)
    head0 = lane < A_HEAD
    n2 = 2 * c_len
    row = lax.broadcasted_iota(jnp.int32, (n2, n2), 0)
    col = lax.broadcasted_iota(jnp.int32, (n2, n2), 1)
    same = (row // c_len) == (col // c_len)
    order = ((row % c_len) - (col % c_len)) * sign
    strict = same & (order > 0)
    incl = same & (order >= 0)
    eye = row == col

    def stack(x):
        return jnp.concatenate([jnp.where(head0, x, 0.0), jnp.where(head0, 0.0, x)], axis=0)

    def dup(x):
        return jnp.concatenate([x, x], axis=0)

    def bdot(a, b):
        return _dot(a.astype(BF16), b.astype(BF16))

    P = range(pairs)
    lanes = [slice(pi * PAIR, (pi + 1) * PAIR) for pi in P]
    lw = [lw_ref[:, ls] for ls in lanes]
    parts = [_split3(x) for x in lw]
    cl_in = [_dot(upto, hi) + (_dot(upto, mid) + _dot(upto, lo)) for hi, mid, lo in parts]
    cl_tot = [jnp.sum(x, axis=0, keepdims=True) for x in lw]
    g_in = [jnp.exp(c) for c in cl_in]
    g_ex = [jnp.exp(c - x) for c, x in zip(cl_in, lw)]
    g_inv = [jnp.exp(-c) for c in cl_in]
    g_end = [jnp.exp(t - c) for t, c in zip(cl_tot, cl_in)]

    ks = [stack(kap_ref[:, ls] * g).astype(BF16) for ls, g in zip(lanes, g_ex)]
    rs = [stack(r_ref[:, ls] * g) for ls, g in zip(lanes, g_in)]
    vs = [stack(v_ref[:, ls]).astype(BF16) for ls in lanes]
    bk = [jnp.concatenate([dup(b_ref[:, ls] * g), dup(k_ref[:, ls] * g)], axis=0).astype(BF16)
          for ls, g in zip(lanes, g_inv)]
    gram = [_dot_nt(jnp.concatenate([ks[i], rs[i].astype(BF16)], axis=0), bk[i]) for i in P]
    bhs_t = [stack(b_ref[:, ls] * g).T.astype(BF16) for ls, g in zip(lanes, g_end)]
    khs_t = [stack(k_ref[:, ls] * g).T.astype(BF16) for ls, g in zip(lanes, g_end)]
    n_ = [jnp.where(strict, gm[:n2, n2:], 0.0).astype(BF16) for gm in gram]
    lb = [jnp.where(incl, gm[n2:, :n2], 0.0).astype(BF16) for gm in gram]
    lk = [jnp.where(incl, gm[n2:, n2:], 0.0).astype(BF16) for gm in gram]

    pw = [jnp.where(strict, -gm[:n2, :n2], 0.0) for gm in gram]
    tinv = [jnp.where(eye, 1.0, 0.0) + x for x in pw]
    span = 2
    while span < c_len:
        pw = [bdot(x, x) for x in pw]
        tinv = [t + bdot(t, x) for t, x in zip(tinv, pw)]
        span *= 2

    nv = [_dot(n_[i], vs[i]) for i in P]
    kq_u = [bdot(tinv[i], jnp.concatenate([ks[i], nv[i].astype(BF16)], axis=1)).astype(BF16) for i in P]
    low = [_dot(bhs_t[i], kq_u[i]) for i in P]
    lbq = [_dot(lb[i], kq_u[i]) for i in P]
    kv = [_dot(khs_t[i], vs[i]) for i in P]
    lkv = [_dot(lk[i], vs[i]) for i in P]
    for i in P:
        a_ = jnp.where(eye, jnp.exp(cl_tot[i]), 0.0) - low[i][:, :PAIR]
        g_ = kv[i] - low[i][:, PAIR:]
        rq = rs[i] - lbq[i][:, :PAIR]
        yi = lkv[i] - lbq[i][:, PAIR:]
        prop = _dot_x3(jnp.concatenate([rq, a_], axis=0), s_ref[i])
        ys = prop[:n2] + yi
        y_ref[:, lanes[i]] = ys[:c_len] + ys[c_len:]
        s_ref[i] = prop[n2:] + g_


def _rwkv_scan(r, kap, v, lw, b, kd, n_ctx):
    n, w = r.shape
    c_len = RWKV_CHUNK
    n_chunks = n // c_len
    n_ctx_chunks = n_ctx // c_len
    pairs = _pick(w // PAIR, (16, 4, 2, 1))
    bw = pairs * PAIR

    def tok(d, p, c):
        return (_chunk_order(d, c, n_ctx_chunks, n_chunks), p)

    def tok2(d, p, c):
        return (d, _chunk_order(d, c, n_ctx_chunks, n_chunks), p)

    one = pl.BlockSpec((c_len, bw), tok)
    two = pl.BlockSpec((None, c_len, bw), tok2)
    return pl.pallas_call(
        functools.partial(_rwkv_scan_kernel, pairs=pairs),
        grid=(2, w // bw, n_chunks),
        in_specs=[one, one, one, two, two, two],
        out_specs=two,
        out_shape=jax.ShapeDtypeStruct((2, n, w), F32),
        scratch_shapes=[pltpu.VMEM((pairs, PAIR, PAIR), F32)],
        compiler_params=_cparams("parallel", "parallel", "arbitrary"),
        name="rwkv_scan",
    )(r, kap, v, lw, b, kd)


def _rope_tables(n_ctx, n_lat, half):
    nq = half // 2
    pos = jnp.arange(n_lat, dtype=jnp.int32)
    row = (pos // GRID_W).astype(F32)
    col = (pos % GRID_W).astype(F32)
    inv = ROPE_THETA ** (-jnp.arange(nq, dtype=F32) / nq)

    def part(p):
        ang = p[:, None] * inv[None, :]
        c, s = jnp.cos(ang), jnp.sin(ang)
        return jnp.concatenate([c, c], -1), jnp.concatenate([-s, s], -1)

    cr, sr = part(row)
    cc, sc = part(col)
    cos = jnp.concatenate([cr, cc], -1)
    sin = jnp.concatenate([sr, sc], -1)
    cos = jnp.concatenate([jnp.ones((n_ctx, 2 * half), F32), cos], 0)
    sin = jnp.concatenate([jnp.zeros((n_ctx, 2 * half), F32), sin], 0)
    return cos, sin


def _rope_tile(x, cos, sin, quarter):
    if 2 * quarter == LANES:
        partner = pltpu.roll(x, quarter, 1)
    else:
        lane = lax.broadcasted_iota(jnp.int32, x.shape, 1)
        lower = (lane % (2 * quarter)) < quarter
        partner = jnp.where(lower, pltpu.roll(x, LANES - quarter, 1), pltpu.roll(x, quarter, 1))
    return x * cos + partner * sin


def _retention_kernel(q_ref, k_ref, v_ref, cos_ref, sin_ref, lg_ref, y_ref, s_ref, *, heads):
    c_len = q_ref.shape[0]
    d = pl.program_id(0)
    rev = d == 1

    @pl.when(pl.program_id(1) == 0)
    def _():
        s_ref[...] = jnp.zeros_like(s_ref)

    cos = cos_ref[...]
    sin = sin_ref[...]

    def rope(x):
        tiles = [_rope_tile(x[:, j * LANES:(j + 1) * LANES], cos[:, j * LANES:(j + 1) * LANES],
                            sin[:, j * LANES:(j + 1) * LANES], LANES // 2) for j in range(B_HEAD // LANES)]
        return jnp.concatenate(tiles, axis=-1)

    n_i = lax.broadcasted_iota(jnp.int32, (c_len, c_len), 0)
    m_i = lax.broadcasted_iota(jnp.int32, (c_len, c_len), 1)
    dist = (n_i - m_i) * jnp.where(rev, -1, 1)
    seen = dist >= jnp.where(rev, 1, 0)
    dist_f = jnp.maximum(dist, 0).astype(F32)
    idx = lax.broadcasted_iota(jnp.int32, (c_len, B_HEAD), 0)
    order = jnp.where(rev, c_len - 1 - idx, idx).astype(F32)

    H = range(heads)
    cols = [slice(h * B_HEAD, (h + 1) * B_HEAD) for h in H]
    lg = [jnp.concatenate([lg_ref[h, 0:1, :]] * (B_HEAD // LANES), axis=-1) for h in H]
    lg_c = [jnp.concatenate([lg_ref[h, 0:1, :]] * (c_len // LANES), axis=-1) for h in H]
    q = [rope(q_ref[:, cs]) for cs in cols]
    k = [rope(k_ref[:, cs]) * (B_HEAD ** -0.5) for cs in cols]
    v = [v_ref[:, cs].astype(BF16) for cs in cols]
    inner = [_dot_nt(q[h].astype(BF16), k[h].astype(BF16)) for h in H]
    cross = [_dot((q[h] * jnp.exp(lg[h] * (order + 1.0))).astype(BF16), s_ref[h].astype(BF16)) for h in H]
    kt = [(k[h] * jnp.exp(lg[h] * (c_len - 1.0 - order))).T.astype(BF16) for h in H]
    upd = [_dot(kt[h], v[h]) for h in H]
    for h in H:
        decay = jnp.where(seen, jnp.exp(lg_c[h] * dist_f), 0.0)
        y_ref[:, cols[h]] = _dot((inner[h] * decay).astype(BF16), v[h]) + cross[h]
        s_ref[h] = s_ref[h] * jnp.exp(lg[h] * float(c_len)) + upd[h]


def _retention(pb, n_ctx, cos, sin):
    n = pb.shape[0]
    width = pb.shape[1] // 4
    heads = width // B_HEAD
    c_len = RET_CHUNK
    n_chunks = n // c_len
    n_ctx_chunks = n_ctx // c_len
    log_g = jnp.log1p(-jnp.exp2(-5.0 - jnp.arange(heads, dtype=F32)))
    lg = jnp.stack([log_g, log_g[::-1]])
    lg = jnp.broadcast_to(lg[:, :, None, None], (2, heads, SUBLANES, LANES))

    def part(which):
        return pl.BlockSpec((c_len, width), lambda d, c: (_chunk_order(d, c, n_ctx_chunks, n_chunks), which))

    tab = pl.BlockSpec((c_len, B_HEAD), lambda d, c: (_chunk_order(d, c, n_ctx_chunks, n_chunks), 0))
    return pl.pallas_call(
        functools.partial(_retention_kernel, heads=heads),
        grid=(2, n_chunks),
        in_specs=[part(0), part(1), part(2), tab, tab,
                  pl.BlockSpec((None, heads, SUBLANES, LANES), lambda d, c: (d, 0, 0, 0))],
        out_specs=pl.BlockSpec((None, c_len, width),
                               lambda d, c: (d, _chunk_order(d, c, n_ctx_chunks, n_chunks), 0)),
        out_shape=jax.ShapeDtypeStruct((2, n, width), F32),
        scratch_shapes=[pltpu.VMEM((heads, B_HEAD, B_HEAD), F32)],
        compiler_params=_cparams("parallel", "arbitrary"),
        name="retention",
    )(pb, pb, pb, cos, sin, lg)


def _mix_finish_kernel(ya_ref, bonus_ref, g_ref, yb_ref, gate_ref, lg_ref, lb_ref, gg_ref, gb_ref, o_ref):
    wa = bonus_ref.shape[1]
    ya = ya_ref[0] + ya_ref[1]
    seg = _head_seg_matrix(A_HEAD)
    inv = 1.0 / A_HEAD
    for j in range(wa // LANES):
        sl = slice(j * LANES, (j + 1) * LANES)
        t = ya[:, sl]
        mu = _segsum_tile(t, seg) * inv
        tc = t - mu
        var = _segsum_tile(tc * tc, seg) * inv
        o = tc * lax.rsqrt(var + A_GN_EPS) * lg_ref[:, sl] + lb_ref[:, sl] + bonus_ref[:, sl]
        o_ref[:, sl] = (o * g_ref[:, sl]).astype(o_ref.dtype)
    yb = yb_ref[0] + yb_ref[1]
    wb = yb.shape[1]
    for j in range(wb // B_HEAD):
        sl = slice(j * B_HEAD, (j + 1) * B_HEAD)
        t = yb[:, sl]
        mu = jnp.mean(t, axis=-1, keepdims=True)
        tc = t - mu
        var = jnp.mean(tc * tc, axis=-1, keepdims=True)
        o = tc * lax.rsqrt(var + B_GN_EPS) * gg_ref[:, sl] + gb_ref[:, sl]
        o_ref[:, wa + j * B_HEAD:wa + (j + 1) * B_HEAD] = (o * _silu(gate_ref[:, sl])).astype(o_ref.dtype)


def _mix_finish(ya, bonus, g, yb, pb, lnx_g, lnx_b, gn_g, gn_b):
    n, wa = bonus.shape
    wb = yb.shape[2]
    tb = _pick(n, (128, 64, 32, 16, 8))
    ra = pl.BlockSpec((tb, wa), lambda i: (i, 0))
    rb = pl.BlockSpec((tb, wb), lambda i: (i, 3))
    va = pl.BlockSpec((1, wa), lambda i: (0, 0))
    vb = pl.BlockSpec((1, wb), lambda i: (0, 0))
    return pl.pallas_call(
        _mix_finish_kernel,
        grid=(n // tb,),
        in_specs=[pl.BlockSpec((2, tb, wa), lambda i: (0, i, 0)), ra, ra,
                  pl.BlockSpec((2, tb, wb), lambda i: (0, i, 0)), rb, va, va, vb, vb],
        out_specs=pl.BlockSpec((tb, wa + wb), lambda i: (i, 0)),
        out_shape=jax.ShapeDtypeStruct((n, wa + wb), BF16),
        compiler_params=_cparams("parallel"),
        name="mix_finish",
    )(ya, bonus, g, yb, pb, lnx_g.reshape(1, wa), lnx_b.reshape(1, wa), gn_g.reshape(1, wb), gn_b.reshape(1, wb))


def _qkv_prep_kernel(x_ref, cos_ref, sin_ref, qg_ref, kg_ref, q_ref, k_ref, v_ref, *, q_heads, kv_heads):
    cos = cos_ref[...]
    sin = sin_ref[...]
    quarter = C_HEAD // 4
    lane = lax.broadcasted_iota(jnp.int32, cos.shape, 1)
    lower = (lane % (2 * quarter)) < quarter
    scale = C_HEAD ** -0.5 * LOG2_E
    n_heads = q_heads + kv_heads
    tiles = [x_ref[:, h * C_HEAD:(h + 1) * C_HEAD] for h in range(n_heads)]
    ms = [jnp.mean(t * t, axis=-1, keepdims=True) for t in tiles]
    gq = qg_ref[...] * scale
    normed = [t * lax.rsqrt(m + QK_EPS) * (gq if h < q_heads else kg_ref[...])
              for h, (t, m) in enumerate(zip(tiles, ms))]
    up = [pltpu.roll(t, LANES - quarter, 1) for t in normed]
    down = [pltpu.roll(t, quarter, 1) for t in normed]
    for h in range(n_heads):
        y = (normed[h] * cos + jnp.where(lower, up[h], down[h]) * sin).astype(BF16)
        if h < q_heads:
            q_ref[:, h * C_HEAD:(h + 1) * C_HEAD] = y
        else:
            k_ref[:, (h - q_heads) * C_HEAD:(h - q_heads + 1) * C_HEAD] = y
    v0 = n_heads * C_HEAD
    v_ref[...] = x_ref[:, v0:v0 + kv_heads * C_HEAD].astype(BF16)


def _qkv_prep(qkv, cos, sin, qn_g, kn_g, q_heads, kv_heads):
    n, c_in = qkv.shape
    tb = _pick(n, (128, 64, 32, 16, 8))
    kvw = kv_heads * C_HEAD
    return pl.pallas_call(
        functools.partial(_qkv_prep_kernel, q_heads=q_heads, kv_heads=kv_heads),
        grid=(n // tb,),
        in_specs=[pl.BlockSpec((tb, c_in), lambda i: (i, 0)),
                  pl.BlockSpec((tb, C_HEAD), lambda i: (i, 0)),
                  pl.BlockSpec((tb, C_HEAD), lambda i: (i, 0)),
                  pl.BlockSpec((1, C_HEAD), lambda i: (0, 0)),
                  pl.BlockSpec((1, C_HEAD), lambda i: (0, 0))],
        out_specs=[pl.BlockSpec((tb, q_heads * C_HEAD), lambda i: (i, 0)),
                   pl.BlockSpec((tb, kvw), lambda i: (i, 0)),
                   pl.BlockSpec((tb, kvw), lambda i: (i, 0))],
        out_shape=[jax.ShapeDtypeStruct((n, q_heads * C_HEAD), BF16),
                   jax.ShapeDtypeStruct((n, kvw), BF16),
                   jax.ShapeDtypeStruct((n, kvw), BF16)],
        compiler_params=_cparams("parallel"),
        name="qkv_prep",
    )(qkv, cos, sin, qn_g.reshape(1, C_HEAD), kn_g.reshape(1, C_HEAD))


def _flash_kernel(*refs, tq, tk, n_kv, n_cast):
    q_ref, k_ref, v_ref = refs[:3]
    cast_in = refs[3:3 + n_cast]
    o_ref = refs[3 + n_cast]
    cast_out = refs[4 + n_cast:4 + 2 * n_cast]
    q_scr, s_scr, p_scr, m_scr, l_scr, c_scr, acc_scr = refs[4 + 2 * n_cast:]
    for src, dst in zip(cast_in, cast_out):
        dst[...] = src[...].astype(BF16)
    nkv = n_kv // tk
    for h in range(C_GROUP):
        q_scr[h * tq:(h + 1) * tq, :] = q_ref[:, h * C_HEAD:(h + 1) * C_HEAD]
    m_scr[...] = jnp.full(m_scr.shape, -jnp.inf, F32)
    l_scr[...] = jnp.zeros(l_scr.shape, F32)
    acc_scr[...] = jnp.zeros(acc_scr.shape, F32)
    n_col = tk // LANES

    def scores(j, slot):
        s_scr[slot] = _dot_nt(q_scr[...], k_ref[pl.ds(pl.multiple_of(j * tk, tk), tk), :])

    def values(j, slot):
        return _dot(p_scr[slot], v_ref[pl.ds(pl.multiple_of(j * tk, tk), tk), :])

    def normalise(slot):
        tile_max = s_scr[slot, :, 0:LANES]
        for c in range(1, n_col):
            tile_max = jnp.maximum(tile_max, s_scr[slot, :, c * LANES:(c + 1) * LANES])
        row_max = jnp.broadcast_to(jnp.max(tile_max, axis=-1, keepdims=True), tile_max.shape)
        m_old = m_scr[...]
        m_new = jnp.maximum(m_old, row_max)
        corr = jnp.exp2(m_old - m_new)
        part = None
        for c in range(n_col):
            p = jnp.exp2(s_scr[slot, :, c * LANES:(c + 1) * LANES] - m_new)
            part = p if part is None else part + p
            p_scr[slot, :, c * LANES:(c + 1) * LANES] = p.astype(BF16)
        l_scr[...] = corr * l_scr[...] + part
        m_scr[...] = m_new
        c_scr[...] = corr

    scores(0, 0)
    if nkv > 1:
        scores(1, 1)
    normalise(0)

    def step(j, cur):
        scores(j + 1, 1 - cur)
        normalise(cur)
        acc_scr[...] = (acc_scr[...] + values(j - 1, 1 - cur)) * c_scr[...]

    def pair(t, carry):
        step(2 * t + 1, 1)
        step(2 * t + 2, 0)
        return carry

    n_mid = max(nkv - 2, 0)
    if n_mid >= 2:
        lax.fori_loop(0, n_mid // 2, pair, 0)
    if n_mid % 2:
        step(nkv - 2, (nkv - 2) % 2)
    if nkv > 1:
        last = (nkv - 1) % 2
        normalise(last)
        acc_scr[...] = (acc_scr[...] + values(nkv - 2, 1 - last)) * c_scr[...]
    acc = acc_scr[...] + values(nkv - 1, (nkv - 1) % 2)
    out = acc / jnp.sum(l_scr[...], axis=-1, keepdims=True)
    for h in range(C_GROUP):
        o_ref[:, h * C_HEAD:(h + 1) * C_HEAD] = out[h * tq:(h + 1) * tq].astype(o_ref.dtype)


def _flash(q, k, v, n_ctx, cast=()):
    n = q.shape[0]
    kv_heads = k.shape[1] // C_HEAD
    n_lat = n - n_ctx
    tq = _pick(n_ctx, (256, 128, 64, 32, 16, 8))
    tk = _pick(n, (1280, 1024, 640, 512, 256, 128, 64, 32, 16, 8))
    gw = C_GROUP * C_HEAD
    q0 = n_ctx // tq
    rows = C_GROUP * tq
    n_qb = n_lat // tq
    steps = kv_heads * n_qb
    slabs = [a.shape[0] * a.shape[1] // steps for a in cast]
    if not all(a.ndim == 3 and a.shape[0] * a.shape[1] == sl * steps and a.shape[1] % sl == 0
               and sl % (2 * SUBLANES) == 0 for a, sl in zip(cast, slabs)):
        return _flash(q, k, v, n_ctx)[0], [a.astype(BF16) for a in cast]

    def slab_spec(a, sl):
        per = a.shape[1] // sl
        return pl.BlockSpec((None, sl, a.shape[2]), lambda g, i: ((g * n_qb + i) // per, (g * n_qb + i) % per, 0))

    cast_specs = [slab_spec(a, sl) for a, sl in zip(cast, slabs)]
    outs = pl.pallas_call(
        functools.partial(_flash_kernel, tq=tq, tk=tk, n_kv=n, n_cast=len(cast)),
        grid=(kv_heads, n_qb),
        in_specs=[pl.BlockSpec((tq, gw), lambda g, i: (q0 + i, g)),
                  pl.BlockSpec((n, C_HEAD), lambda g, i: (0, g)),
                  pl.BlockSpec((n, C_HEAD), lambda g, i: (0, g))] + cast_specs,
        out_specs=[pl.BlockSpec((tq, gw), lambda g, i: (i, g))] + cast_specs,
        out_shape=[jax.ShapeDtypeStruct((n_lat, kv_heads * gw), BF16)]
        + [jax.ShapeDtypeStruct(a.shape, BF16) for a in cast],
        scratch_shapes=[pltpu.VMEM((rows, C_HEAD), BF16),
                        pltpu.VMEM((2, rows, tk), F32),
                        pltpu.VMEM((2, rows, tk), BF16),
                        pltpu.VMEM((rows, LANES), F32),
                        pltpu.VMEM((rows, LANES), F32),
                        pltpu.VMEM((rows, LANES), F32),
                        pltpu.VMEM((rows, C_HEAD), F32)],
        compiler_params=_cparams("parallel", "parallel"),
        name="flash_gqa",
    )(q, k, v, *cast)
    return outs[0], list(outs[1:])


def _pad_cols(w, n):
    return jnp.pad(w, ((0, 0), (0, n - w.shape[1])))


def kernel(x, c, ctx, c_ctx, l0_mod_w, l0_mod_b, l0_w_in, l0_a_mu, l0_a_w0, l0_a_w2, l0_a_a0, l0_a_a2, l0_a_g2, l0_a_k_k, l0_a_k_a, l0_a_r_k, l0_a_lnx_g, l0_a_lnx_b, l0_b_gn_g, l0_b_gn_b, l0_w_out, l0_ln1_g, l0_ln1_b, l0_ffn_w1, l0_ffn_w3, l0_ffn_w2, l0_ln2_g, l0_ln2_b, l1_mod_w, l1_mod_b, l1_w_qkv, l1_q_norm_g, l1_k_norm_g, l1_w_out, l1_ln1_g, l1_ln1_b, l1_router, l1_moe_w1, l1_moe_w3, l1_moe_w2, l1_ln2_g, l1_ln2_b):
    batch, n_lat, d = x.shape
    n_ctx = ctx.shape[1]
    assert batch == 1 and c.shape[0] == 1
    a_width = l0_a_g2.shape[1]
    a_in = l0_a_mu.shape[0]
    b_width = l0_b_gn_g.shape[0]

    c_rows = jnp.zeros((SUBLANES, d), F32).at[0].set(c[0]).at[1].set(c_ctx)
    mod0 = _mod_vectors(c_rows, l0_mod_w, l0_mod_b)
    mod1 = _mod_vectors(c_rows, l1_mod_w, l1_mod_b)

    h = jnp.concatenate([ctx[0], x[0]], axis=0)

    _, u = _ln_mod(h, None, mod0, 0, n_ctx)
    a_pad = _round_up(a_in, 1024) if a_in > 1024 else a_in
    w_in_a = _pad_cols(l0_w_in[:, :a_in], a_pad).astype(BF16)
    w_in_b = l0_w_in[:, a_in:].astype(BF16)
    pa = _mm_plain(u, w_in_a, F32)
    pb = _mm_plain(u, w_in_b, F32)
    r, v, kap, g, bonus, lw, b, kd = _rwkv_prepare(pa, n_ctx, a_in, l0_a_mu, l0_a_w0, l0_a_w2, l0_a_a0, l0_a_a2,
                                                   l0_a_g2, l0_a_k_k, l0_a_k_a, l0_a_r_k)
    ya = _rwkv_scan(r, kap, v, lw, b, kd, n_ctx)
    cos_b, sin_b = _rope_tables(n_ctx, n_lat, B_HEAD // 2)
    yb = _retention(pb, n_ctx, cos_b, sin_b)
    mix = _mix_finish(ya, bonus, g, yb, pb, l0_a_lnx_g, l0_a_lnx_b, l0_b_gn_g, l0_b_gn_b)
    res = _mm_res(mix, l0_w_out.astype(BF16), h, mod0, 2, n_ctx)
    h, u = _ln_mod(res, (l0_ln1_g, l0_ln1_b), mod0, 3, n_ctx)
    d_ff = l0_ffn_w1.shape[1]
    ff_pad = _round_up(d_ff, 1024) if d_ff > 1024 else d_ff
    hid = _mm_swiglu(u, _pad_cols(l0_ffn_w1, ff_pad).astype(BF16), _pad_cols(l0_ffn_w3, ff_pad).astype(BF16))
    w2 = jnp.pad(l0_ffn_w2, ((0, ff_pad - d_ff), (0, 0))).astype(BF16)
    res = _mm_res(hid, w2, h, mod0, 5, n_ctx)
    h, u = _ln_mod(res, (l0_ln2_g, l0_ln2_b), mod1, 0, n_ctx)

    q_heads = l1_w_out.shape[0] // C_HEAD
    kv_heads = q_heads // C_GROUP
    qkv = _mm_plain(u, l1_w_qkv.astype(BF16), F32)
    cos_c, sin_c = _rope_tables(n_ctx, n_lat, C_HEAD // 2)
    q, k, vv = _qkv_prep(qkv, cos_c, sin_c, l1_q_norm_g, l1_k_norm_g, q_heads, kv_heads)
    att, (moe_w1, moe_w3, moe_w2) = _flash(q, k, vv, n_ctx, cast=(l1_moe_w1, l1_moe_w3, l1_moe_w2))
    h_lat = h[n_ctx:]
    res = _mm_res(att, l1_w_out.astype(BF16), h_lat, mod1, 2, 0)
    h_lat, u, comb = _ln_mod(res, (l1_ln1_g, l1_ln1_b), mod1, 3, 0, router=l1_router)
    n_exp = l1_router.shape[1]
    cexp = jnp.broadcast_to(comb[:, :n_exp].T[:, :, None], (n_exp, n_lat, LANES))
    hid = _mm_swiglu(u, moe_w1, moe_w3, cexp)
    w2 = moe_w2.reshape(n_exp * moe_w2.shape[1], d)
    res = _mm_res(hid, w2, h_lat, mod1, 5, 0)
    out, _ = _ln_mod(res, (l1_ln2_g, l1_ln2_b), None, 0, 0)
    return out[None]
```

```python
import functools

import jax
import jax.numpy as jnp
from jax import lax
from jax.experimental import pallas as pl
from jax.experimental.pallas import tpu as pltpu

F32 = jnp.float32
BF16 = jnp.bfloat16
HIGHEST = lax.Precision.HIGHEST

GRID_W = 64
DEPTH = 2
ALPHA = (2 * DEPTH) ** 0.25
LN_EPS = 1e-6
ROPE_THETA = 10000.0
A_HEAD = 64
A_DECAY_LORA = 96
A_AAA_LORA = 96
A_GATE_LORA = 256
A_GN_EPS = 64e-5
B_HEAD = 256
RET_CHUNK = 128
B_GN_EPS = 1e-5
C_HEAD = 128
C_GROUP = 4
QK_EPS = 1e-6
LOG2_E = 1.4426950408889634

LANES = 128
SUBLANES = 8
VMEM_LIMIT_BYTES = 60 * 1024 * 1024

RWKV_CHUNK = 64
PAIR = 2 * A_HEAD


def _cparams(*sem):
    return pltpu.CompilerParams(dimension_semantics=sem, vmem_limit_bytes=VMEM_LIMIT_BYTES)


def _pick(n, candidates):
    for c in candidates:
        if c <= n and n % c == 0:
            return c
    return n


def _round_up(n, m):
    return (n + m - 1) // m * m


def _silu(x):
    return x * (1.0 / (1.0 + jnp.exp(-x)))


def _sigmoid(x):
    return 1.0 / (1.0 + jnp.exp(-x))


def _dot(a, b):
    return jnp.dot(a, b, preferred_element_type=F32)


def _dot_nt(a, b, precision=None):
    return lax.dot_general(a, b, (((1,), (1,)), ((), ())), preferred_element_type=F32, precision=precision)


def _row_is_ctx(shape, row0, n_ctx):
    rows = row0 + lax.broadcasted_iota(jnp.int32, shape, 0)
    return rows < n_ctx


def _select_stream(mod_ref, shape, row0, n_ctx):
    lat = mod_ref[0:1, :]
    if n_ctx == 0:
        return jnp.broadcast_to(lat, shape)
    return jnp.where(_row_is_ctx(shape, row0, n_ctx), mod_ref[1:2, :], lat)


def _modvec_kernel(c_ref, w_ref, b_ref, o_ref):
    s = _silu(c_ref[...])
    o_ref[...] = jnp.dot(s, w_ref[...], preferred_element_type=F32, precision=HIGHEST) + b_ref[...]


def _mod_vectors(c_rows, w, b):
    d, n = w.shape
    tn = _pick(n, (512, 256, 128))
    return pl.pallas_call(
        _modvec_kernel,
        grid=(n // tn,),
        in_specs=[pl.BlockSpec((SUBLANES, d), lambda j: (0, 0)),
                  pl.BlockSpec((d, tn), lambda j: (0, j)),
                  pl.BlockSpec((1, tn), lambda j: (0, j))],
        out_specs=pl.BlockSpec((SUBLANES, tn), lambda j: (0, j)),
        out_shape=jax.ShapeDtypeStruct((SUBLANES, n), F32),
        compiler_params=_cparams("parallel"),
        name="mod_vectors",
    )(c_rows, w, b.reshape(1, n))


def _top2_combine(u, w_ref, n_exp):
    logits = jnp.dot(u, w_ref[...], preferred_element_type=F32, precision=HIGHEST)
    lane = lax.broadcasted_iota(jnp.int32, logits.shape, 1)
    neg = jnp.float32(-jnp.inf)
    l1 = jnp.where(lane < n_exp, logits, neg)
    m1 = jnp.max(l1, axis=-1, keepdims=True)
    i1 = jnp.min(jnp.where(l1 == m1, lane, LANES), axis=-1, keepdims=True)
    l2 = jnp.where(lane == i1, neg, l1)
    m2 = jnp.max(l2, axis=-1, keepdims=True)
    i2 = jnp.min(jnp.where(l2 == m2, lane, LANES), axis=-1, keepdims=True)
    e2 = jnp.exp(m2 - m1)
    return jnp.where(lane == i1, 1.0 / (1.0 + e2), 0.0) + jnp.where(lane == i2, e2 / (1.0 + e2), 0.0)


def _ln_mod_kernel(*refs, tb, n_ctx, do_ln, do_mod, n_exp):
    refs = list(refs)
    r_ref = refs.pop(0)
    x = r_ref[...]
    if do_ln:
        g_ref, b_ref = refs.pop(0), refs.pop(0)
        mu = jnp.mean(x, axis=-1, keepdims=True)
        xc = x - mu
        var = jnp.mean(xc * xc, axis=-1, keepdims=True)
        x = xc * lax.rsqrt(var + LN_EPS) * g_ref[...] + b_ref[...]
    if do_mod:
        sh_ref, sc_ref = refs.pop(0), refs.pop(0)
    if n_exp:
        w_ref = refs.pop(0)
    if do_ln:
        refs.pop(0)[...] = x
    if do_mod:
        row0 = pl.program_id(0) * tb
        sh = _select_stream(sh_ref, x.shape, row0, n_ctx)
        sc = _select_stream(sc_ref, x.shape, row0, n_ctx)
        u = x * (1.0 + sc) + sh
        refs.pop(0)[...] = u.astype(BF16)
        if n_exp:
            refs.pop(0)[...] = _top2_combine(u, w_ref, n_exp)


def _ln_mod(r, ln, mod, k_shift, n_ctx, router=None):
    n, d = r.shape
    do_ln, do_mod = ln is not None, mod is not None
    n_exp = 0 if router is None else router.shape[1]
    tb = _pick(n, (256, 128, 64, 32, 16, 8))
    row = pl.BlockSpec((tb, d), lambda i: (i, 0))
    vec = pl.BlockSpec((1, d), lambda i: (0, 0))
    args, in_specs, out_specs, out_shape = [r], [row], [], []
    if do_ln:
        args += [ln[0].reshape(1, d), ln[1].reshape(1, d)]
        in_specs += [vec, vec]
        out_specs.append(row)
        out_shape.append(jax.ShapeDtypeStruct((n, d), F32))
    if do_mod:
        args += [mod, mod]
        in_specs += [pl.BlockSpec((SUBLANES, d), lambda i: (0, k_shift)),
                     pl.BlockSpec((SUBLANES, d), lambda i: (0, k_shift + 1))]
        out_specs.append(row)
        out_shape.append(jax.ShapeDtypeStruct((n, d), BF16))
    if n_exp:
        args.append(jnp.zeros((d, LANES), F32).at[:, :n_exp].set(router))
        in_specs.append(pl.BlockSpec((d, LANES), lambda i: (0, 0)))
        out_specs.append(pl.BlockSpec((tb, LANES), lambda i: (i, 0)))
        out_shape.append(jax.ShapeDtypeStruct((n, LANES), F32))
    outs = pl.pallas_call(
        functools.partial(_ln_mod_kernel, tb=tb, n_ctx=n_ctx, do_ln=do_ln, do_mod=do_mod, n_exp=n_exp),
        grid=(n // tb,),
        in_specs=in_specs,
        out_specs=out_specs,
        out_shape=out_shape,
        compiler_params=_cparams("parallel"),
        name="ln_mod",
    )(*args)
    outs = list(outs)
    h = outs.pop(0) if do_ln else r
    u = outs.pop(0) if do_mod else None
    return (h, u, outs.pop(0)) if n_exp else (h, u)


def _mm_kernel(*refs, kind, tm, n_ctx, nk):
    n_w = 2 if kind == "swiglu" else 1
    n_acc = n_w if nk > 1 else 0
    accs = refs[len(refs) - n_acc:]
    refs = refs[:len(refs) - n_acc]
    x_ref, w_refs, rest = refs[0], refs[1:1 + n_w], refs[1 + n_w:]
    o_ref = rest[-1]
    k = pl.program_id(2)

    def products():
        x = x_ref[...]
        return [_dot(x, w[...]) for w in w_refs]

    def finish(tot):
        if kind == "swiglu":
            c_ref = rest[0] if len(rest) == 2 else None
            for j in range(o_ref.shape[1] // LANES):
                sl = slice(j * LANES, (j + 1) * LANES)
                y = _silu(tot[0][:, sl]) * tot[1][:, sl]
                if c_ref is not None:
                    y = y * c_ref[...]
                o_ref[:, sl] = y.astype(o_ref.dtype)
        elif kind == "res":
            h_ref, gt_ref = rest[0], rest[1]
            gt = _select_stream(gt_ref, tot[0].shape, pl.program_id(0) * tm, n_ctx)
            o_ref[...] = ALPHA * h_ref[...] + gt * tot[0]
        else:
            o_ref[...] = tot[0].astype(o_ref.dtype)

    if nk == 1:
        finish(products())
        return

    @pl.when(k == 0)
    def _():
        for acc, y in zip(accs, products()):
            acc[...] = y

    @pl.when((k > 0) & (k < nk - 1))
    def _():
        for acc, y in zip(accs, products()):
            acc[...] += y

    @pl.when(k == nk - 1)
    def _():
        finish([acc[...] + y for acc, y in zip(accs, products())])


def _mm_tiles(m, n, k):
    tm = _pick(m, (1280, 1024, 512, 256, 128, 64, 32, 16, 8))
    if k > 2 * 4096:
        return tm, _pick(n, (1024, 512, 256, 128)), _pick(k, (2048, 1024, 512, 256, 128))
    tn = _pick(n, (512, 256, 128))
    tk = _pick(k, (4096, 2816, 2048, 1024, 512, 256, 128))
    return tm, tn, tk


def _mm_scratch(tm, tn, nk, n_w):
    return [pltpu.VMEM((tm, tn), F32)] * n_w if nk > 1 else []


def _mm_plain(x, w, out_dtype):
    m, kd = x.shape
    n = w.shape[1]
    tm, tn, tk = _mm_tiles(m, n, kd)
    nk = kd // tk
    return pl.pallas_call(
        functools.partial(_mm_kernel, kind="plain", tm=tm, n_ctx=0, nk=nk),
        grid=(m // tm, n // tn, nk),
        in_specs=[pl.BlockSpec((tm, tk), lambda i, j, k: (i, k)),
                  pl.BlockSpec((tk, tn), lambda i, j, k: (k, j))],
        out_specs=pl.BlockSpec((tm, tn), lambda i, j, k: (i, j)),
        out_shape=jax.ShapeDtypeStruct((m, n), out_dtype),
        scratch_shapes=_mm_scratch(tm, tn, nk, 1),
        compiler_params=_cparams("parallel", "parallel", "arbitrary"),
        name="mm_plain",
    )(x, w)


def _mm_res(x, w, h, mod, k_gate, n_ctx):
    m, kd = x.shape
    n = w.shape[1]
    tm, tn, tk = _mm_tiles(m, n, kd)
    nk = kd // tk
    gate_block0 = k_gate * (n // tn)
    return pl.pallas_call(
        functools.partial(_mm_kernel, kind="res", tm=tm, n_ctx=n_ctx, nk=nk),
        grid=(m // tm, n // tn, nk),
        in_specs=[pl.BlockSpec((tm, tk), lambda i, j, k: (i, k)),
                  pl.BlockSpec((tk, tn), lambda i, j, k: (k, j)),
                  pl.BlockSpec((tm, tn), lambda i, j, k: (i, j)),
                  pl.BlockSpec((SUBLANES, tn), lambda i, j, k: (0, gate_block0 + j))],
        out_specs=pl.BlockSpec((tm, tn), lambda i, j, k: (i, j)),
        out_shape=jax.ShapeDtypeStruct((m, n), F32),
        scratch_shapes=_mm_scratch(tm, tn, nk, 1),
        compiler_params=_cparams("parallel", "parallel", "arbitrary"),
        name="mm_res",
    )(x, w, h, mod)


def _mm_swiglu(x, w1, w3, cexp=None):
    m, kd = x.shape
    stacked = w1.ndim == 3
    f = w1.shape[-1]
    n_exp = w1.shape[0] if stacked else 1
    tm, tn, tk = _mm_tiles(m, f, kd)
    nk = kd // tk
    nf = f // tn
    if stacked:
        w_spec = pl.BlockSpec((None, tk, tn), lambda i, j, k: (j // nf, k, j % nf))
    else:
        w_spec = pl.BlockSpec((tk, tn), lambda i, j, k: (k, j))
    in_specs = [pl.BlockSpec((tm, tk), lambda i, j, k: (i, k)), w_spec, w_spec]
    args = [x, w1, w3]
    if cexp is not None:
        in_specs.append(pl.BlockSpec((None, tm, LANES), lambda i, j, k: (j // nf, i, 0)))
        args.append(cexp)

    return pl.pallas_call(
        functools.partial(_mm_kernel, kind="swiglu", tm=tm, n_ctx=0, nk=nk),
        grid=(m // tm, n_exp * nf, nk),
        in_specs=in_specs,
        out_specs=pl.BlockSpec((tm, tn), lambda i, j, k: (i, j)),
        out_shape=jax.ShapeDtypeStruct((m, n_exp * f), BF16),
        scratch_shapes=_mm_scratch(tm, tn, nk, 2),
        compiler_params=_cparams("parallel", "parallel", "arbitrary"),
        name="mm_swiglu",
    )(*args)


def _split3(x):
    hi = x.astype(BF16)
    r1 = x - hi.astype(F32)
    mid = r1.astype(BF16)
    lo = (r1 - mid.astype(F32)).astype(BF16)
    return hi, mid, lo


def _head_seg_matrix(width):
    r = lax.broadcasted_iota(jnp.int32, (LANES, LANES), 0) // width
    c = lax.broadcasted_iota(jnp.int32, (LANES, LANES), 1) // width
    return (r == c).astype(BF16)


def _segsum_tile(x, seg):
    hi, mid, lo = _split3(x)
    return _dot(hi, seg) + _dot(mid, seg) + _dot(lo, seg)


def _rwkv_prep_kernel(p_ref, prev_ref, next_ref, mu_ref, w0_ref, w2_ref, a0_ref, a2_ref, g2_ref,
                      kk_ref, ka_ref, rk_ref,
                      r_out, v_out, kap_out, g_out, bonus_out, lw_out, b_out, kd_out,
                      *, tb, n_ctx, n_rows, width, win_w, win_a, o_gl):
    x = p_ref[...]
    row = pl.program_id(0) * tb + lax.broadcasted_iota(jnp.int32, (tb, 1), 0)
    local = lax.broadcasted_iota(jnp.int32, (tb, 1), 0)
    prev = jnp.where(local == 0, prev_ref[SUBLANES - 1:SUBLANES, :], pltpu.roll(x, 1, 0))
    nxt = jnp.where(local == tb - 1, next_ref[0:1, :], pltpu.roll(x, tb - 1, 0))
    prev = jnp.where((row == 0) | (row == n_ctx), 0.0, prev)
    nxt = jnp.where((row == n_ctx - 1) | (row == n_rows - 1), 0.0, nxt)
    x = x + mu_ref[...] * (0.5 * (prev + nxt) - x)

    w = width
    r = x[:, 0:w]
    k = x[:, w:2 * w]
    v = x[:, 2 * w:3 * w]
    gl = x[:, o_gl:o_gl + A_GATE_LORA]
    wl = x[:, win_w[0]:win_w[0] + win_w[1]]
    al = x[:, win_a[0]:win_a[0] + win_a[1]]

    g_out[...] = _dot(_sigmoid(gl).astype(BF16), g2_ref[...])
    lw_lora = _dot(jnp.tanh(wl).astype(BF16), w2_ref[...])
    a_lora = _dot(al.astype(BF16), a2_ref[...])

    kk = k * kk_ref[...]
    seg = _head_seg_matrix(A_HEAD)
    rk = rk_ref[...]
    ka = ka_ref[...]
    kd_sum = jnp.zeros_like(k)
    kaps = []
    for j in range(w // LANES):
        sl = slice(j * LANES, (j + 1) * LANES)
        t = kk[:, sl]
        n2 = _segsum_tile(t * t, seg)
        kaps.append(t / jnp.maximum(jnp.sqrt(n2), 1e-12))
    kap = jnp.concatenate(kaps, axis=-1)
    for d in range(2):
        z = w0_ref[d:d + 1, :] + lw_lora[:, d * w:(d + 1) * w]
        softplus = jnp.maximum(-z, 0.0) + jnp.log(1.0 + jnp.exp(-jnp.abs(z)))
        wlog = -softplus - 0.5
        lw_out[d] = -jnp.exp(wlog)
        a = _sigmoid(a0_ref[d:d + 1, :] + a_lora[:, d * w:(d + 1) * w])
        b_out[d] = kap * a
        kd = k * (1.0 + (a - 1.0) * ka)
        kd_out[d] = kd
        kd_sum = kd_sum + kd
    bon = r * kd_sum * rk
    bons = []
    for j in range(w // LANES):
        sl = slice(j * LANES, (j + 1) * LANES)
        bons.append(_segsum_tile(bon[:, sl], seg))
    bonus_out[...] = jnp.concatenate(bons, axis=-1) * v
    r_out[...] = r
    v_out[...] = v
    kap_out[...] = kap


def _aligned_window(offset, width, limit):
    start = offset // LANES * LANES
    size = min(_round_up(offset + width - start, LANES), limit - start)
    return start, size


def _window_weight(blocks, offset, window, width):
    start, size = window
    lora = blocks.shape[1]
    wt = jnp.zeros((size, 2 * width), F32)
    for d in range(2):
        r0 = offset - start + d * lora
        wt = wt.at[r0:r0 + lora, d * width:(d + 1) * width].set(blocks[d])
    return wt.astype(BF16)


def _rwkv_prepare(pa, n_ctx, a_in, mu, w0, w2, a0, a2, g2, k_k, k_a, r_k):
    n = pa.shape[0]
    w = g2.shape[1]
    tb = _pick(n_ctx if n_ctx else n, (64, 32, 16, 8))
    nblk = n // tb
    o_gl = 3 * w
    o_wl = o_gl + A_GATE_LORA
    o_al = o_wl + 2 * A_DECAY_LORA
    win_w = _aligned_window(o_wl, 2 * A_DECAY_LORA, a_in)
    win_a = _aligned_window(o_al, 2 * A_AAA_LORA, a_in)
    w2w = _window_weight(w2, o_wl, win_w, w)
    a2w = _window_weight(a2, o_al, win_a, w)
    g = tb // SUBLANES
    n_groups = n // SUBLANES
    prev_spec = pl.BlockSpec((SUBLANES, a_in), lambda i: (jnp.maximum(i * g - 1, 0), 0))
    next_spec = pl.BlockSpec((SUBLANES, a_in), lambda i: (jnp.minimum((i + 1) * g, n_groups - 1), 0))

    tok = pl.BlockSpec((tb, w), lambda i: (i, 0))
    tok2 = pl.BlockSpec((2, tb, w), lambda i: (0, i, 0))

    def full(a):
        return pl.BlockSpec(a.shape, lambda i: (0,) * a.ndim)

    params = [mu.reshape(1, a_in), w0, w2w, a0, a2w, g2.astype(BF16), k_k.reshape(1, w), k_a.reshape(1, w),
              r_k.reshape(1, w)]
    one = jax.ShapeDtypeStruct((n, w), F32)
    two = jax.ShapeDtypeStruct((2, n, w), F32)
    return pl.pallas_call(
        functools.partial(_rwkv_prep_kernel, tb=tb, n_ctx=n_ctx, n_rows=n, width=w, win_w=win_w, win_a=win_a,
                          o_gl=o_gl),
        grid=(nblk,),
        in_specs=[pl.BlockSpec((tb, a_in), lambda i: (i, 0)), prev_spec, next_spec] + [full(a) for a in params],
        out_specs=[tok, tok, tok, tok, tok, tok2, tok2, tok2],
        out_shape=[one, one, one, one, one, two, two, two],
        compiler_params=_cparams("parallel"),
        name="rwkv_prepare",
    )(pa, pa, pa, *params)


def _chunk_order(d, c, n_ctx_chunks, n_chunks):
    back = jnp.where(c < n_ctx_chunks, n_ctx_chunks - 1 - c, n_chunks - 1 - (c - n_ctx_chunks))
    return jnp.where(d == 0, c, back)


def _split2(x):
    hi = x.astype(BF16)
    return hi, (x - hi.astype(F32)).astype(BF16)


def _dot_x3(a, b):
    a_hi, a_lo = _split2(a)
    b_hi, b_lo = _split2(b)
    return _dot(a_hi, b_hi) + (_dot(a_lo, b_hi) + _dot(a_hi, b_lo))


def _rwkv_scan_kernel(r_ref, kap_ref, v_ref, lw_ref, b_ref, k_ref, y_ref, s_ref, *, pairs):
    c_len = r_ref.shape[0]
    d = pl.program_id(0)
    sign = jnp.where(d == 1, -1, 1)

    @pl.when(pl.program_id(2) == 0)
    def _():
        s_ref[...] = jnp.zeros_like(s_ref)

    ti = lax.broadcasted_iota(jnp.int32, (c_len, c_len), 0)
    si = lax.broadcasted_iota(jnp.int32, (c_len, c_len), 1)
    upto = ((ti - si) * sign >= 0).astype(BF16)
    lane = lax.broadcasted_iota(jnp.int32, (c_len, PAIR), 1)
    head0 = lane < A_HEAD
    n2 = 2 * c_len
    row = lax.broadcasted_iota(jnp.int32, (n2, n2), 0)
    col = lax.broadcasted_iota(jnp.int32, (n2, n2), 1)
    same = (row // c_len) == (col // c_len)
    order = ((row % c_len) - (col % c_len)) * sign
    strict = same & (order > 0)
    incl = same & (order >= 0)
    eye = row == col

    def stack(x):
        return jnp.concatenate([jnp.where(head0, x, 0.0), jnp.where(head0, 0.0, x)], axis=0)

    def dup(x):
        return jnp.concatenate([x, x], axis=0)

    def bdot(a, b):
        return _dot(a.astype(BF16), b.astype(BF16))

    P = range(pairs)
    lanes = [slice(pi * PAIR, (pi + 1) * PAIR) for pi in P]
    lw = [lw_ref[:, ls] for ls in lanes]
    parts = [_split3(x) for x in lw]
    cl_in = [_dot(upto, hi) + (_dot(upto, mid) + _dot(upto, lo)) for hi, mid, lo in parts]
    cl_tot = [jnp.sum(x, axis=0, keepdims=True) for x in lw]
    g_in = [jnp.exp(c) for c in cl_in]
    g_ex = [jnp.exp(c - x) for c, x in zip(cl_in, lw)]
    g_inv = [jnp.exp(-c) for c in cl_in]
    g_end = [jnp.exp(t - c) for t, c in zip(cl_tot, cl_in)]

    ks = [stack(kap_ref[:, ls] * g).astype(BF16) for ls, g in zip(lanes, g_ex)]
    rs = [stack(r_ref[:, ls] * g) for ls, g in zip(lanes, g_in)]
    vs = [stack(v_ref[:, ls]).astype(BF16) for ls in lanes]
    bk = [jnp.concatenate([dup(b_ref[:, ls] * g), dup(k_ref[:, ls] * g)], axis=0).astype(BF16)
          for ls, g in zip(lanes, g_inv)]
    gram = [_dot_nt(jnp.concatenate([ks[i], rs[i].astype(BF16)], axis=0), bk[i]) for i in P]
    bhs_t = [stack(b_ref[:, ls] * g).T.astype(BF16) for ls, g in zip(lanes, g_end)]
    khs_t = [stack(k_ref[:, ls] * g).T.astype(BF16) for ls, g in zip(lanes, g_end)]
    n_ = [jnp.where(strict, gm[:n2, n2:], 0.0).astype(BF16) for gm in gram]
    lb = [jnp.where(incl, gm[n2:, :n2], 0.0).astype(BF16) for gm in gram]
    lk = [jnp.where(incl, gm[n2:, n2:], 0.0).astype(BF16) for gm in gram]

    pw = [jnp.where(strict, -gm[:n2, :n2], 0.0) for gm in gram]
    tinv = [jnp.where(eye, 1.0, 0.0) + x for x in pw]
    span = 2
    while span < c_len:
        pw = [bdot(x, x) for x in pw]
        tinv = [t + bdot(t, x) for t, x in zip(tinv, pw)]
        span *= 2

    nv = [_dot(n_[i], vs[i]) for i in P]
    kq_u = [bdot(tinv[i], jnp.concatenate([ks[i], nv[i].astype(BF16)], axis=1)).astype(BF16) for i in P]
    low = [_dot(bhs_t[i], kq_u[i]) for i in P]
    lbq = [_dot(lb[i], kq_u[i]) for i in P]
    kv = [_dot(khs_t[i], vs[i]) for i in P]
    lkv = [_dot(lk[i], vs[i]) for i in P]
    for i in P:
        a_ = jnp.where(eye, jnp.exp(cl_tot[i]), 0.0) - low[i][:, :PAIR]
        g_ = kv[i] - low[i][:, PAIR:]
        rq = rs[i] - lbq[i][:, :PAIR]
        yi = lkv[i] - lbq[i][:, PAIR:]
        prop = _dot_x3(jnp.concatenate([rq, a_], axis=0), s_ref[i])
        ys = prop[:n2] + yi
        y_ref[:, lanes[i]] = ys[:c_len] + ys[c_len:]
        s_ref[i] = prop[n2:] + g_


def _rwkv_scan(r, kap, v, lw, b, kd, n_ctx):
    n, w = r.shape
    c_len = RWKV_CHUNK
    n_chunks = n // c_len
    n_ctx_chunks = n_ctx // c_len
    pairs = _pick(w // PAIR, (16, 4, 2, 1))
    bw = pairs * PAIR

    def tok(d, p, c):
        return (_chunk_order(d, c, n_ctx_chunks, n_chunks), p)

    def tok2(d, p, c):
        return (d, _chunk_order(d, c, n_ctx_chunks, n_chunks), p)

    one = pl.BlockSpec((c_len, bw), tok)
    two = pl.BlockSpec((None, c_len, bw), tok2)
    return pl.pallas_call(
        functools.partial(_rwkv_scan_kernel, pairs=pairs),
        grid=(2, w // bw, n_chunks),
        in_specs=[one, one, one, two, two, two],
        out_specs=two,
        out_shape=jax.ShapeDtypeStruct((2, n, w), F32),
        scratch_shapes=[pltpu.VMEM((pairs, PAIR, PAIR), F32)],
        compiler_params=_cparams("parallel", "parallel", "arbitrary"),
        name="rwkv_scan",
    )(r, kap, v, lw, b, kd)


def _rope_tables(n_ctx, n_lat, half):
    nq = half // 2
    pos = jnp.arange(n_lat, dtype=jnp.int32)
    row = (pos // GRID_W).astype(F32)
    col = (pos % GRID_W).astype(F32)
    inv = ROPE_THETA ** (-jnp.arange(nq, dtype=F32) / nq)

    def part(p):
        ang = p[:, None] * inv[None, :]
        c, s = jnp.cos(ang), jnp.sin(ang)
        return jnp.concatenate([c, c], -1), jnp.concatenate([-s, s], -1)

    cr, sr = part(row)
    cc, sc = part(col)
    cos = jnp.concatenate([cr, cc], -1)
    sin = jnp.concatenate([sr, sc], -1)
    cos = jnp.concatenate([jnp.ones((n_ctx, 2 * half), F32), cos], 0)
    sin = jnp.concatenate([jnp.zeros((n_ctx, 2 * half), F32), sin], 0)
    return cos, sin


def _rope_tile(x, cos, sin, quarter):
    if 2 * quarter == LANES:
        partner = pltpu.roll(x, quarter, 1)
    else:
        lane = lax.broadcasted_iota(jnp.int32, x.shape, 1)
        lower = (lane % (2 * quarter)) < quarter
        partner = jnp.where(lower, pltpu.roll(x, LANES - quarter, 1), pltpu.roll(x, quarter, 1))
    return x * cos + partner * sin


def _retention_kernel(q_ref, k_ref, v_ref, cos_ref, sin_ref, lg_ref, y_ref, s_ref, *, heads):
    c_len = q_ref.shape[0]
    d = pl.program_id(0)
    rev = d == 1

    @pl.when(pl.program_id(1) == 0)
    def _():
        s_ref[...] = jnp.zeros_like(s_ref)

    cos = cos_ref[...]
    sin = sin_ref[...]

    def rope(x):
        tiles = [_rope_tile(x[:, j * LANES:(j + 1) * LANES], cos[:, j * LANES:(j + 1) * LANES],
                            sin[:, j * LANES:(j + 1) * LANES], LANES // 2) for j in range(B_HEAD // LANES)]
        return jnp.concatenate(tiles, axis=-1)

    n_i = lax.broadcasted_iota(jnp.int32, (c_len, c_len), 0)
    m_i = lax.broadcasted_iota(jnp.int32, (c_len, c_len), 1)
    dist = (n_i - m_i) * jnp.where(rev, -1, 1)
    seen = dist >= jnp.where(rev, 1, 0)
    dist_f = jnp.maximum(dist, 0).astype(F32)
    idx = lax.broadcasted_iota(jnp.int32, (c_len, B_HEAD), 0)
    order = jnp.where(rev, c_len - 1 - idx, idx).astype(F32)

    H = range(heads)
    cols = [slice(h * B_HEAD, (h + 1) * B_HEAD) for h in H]
    lg = [jnp.concatenate([lg_ref[h, 0:1, :]] * (B_HEAD // LANES), axis=-1) for h in H]
    lg_c = [jnp.concatenate([lg_ref[h, 0:1, :]] * (c_len // LANES), axis=-1) for h in H]
    q = [rope(q_ref[:, cs]) for cs in cols]
    k = [rope(k_ref[:, cs]) * (B_HEAD ** -0.5) for cs in cols]
    v = [v_ref[:, cs].astype(BF16) for cs in cols]
    inner = [_dot_nt(q[h].astype(BF16), k[h].astype(BF16)) for h in H]
    cross = [_dot((q[h] * jnp.exp(lg[h] * (order + 1.0))).astype(BF16), s_ref[h].astype(BF16)) for h in H]
    kt = [(k[h] * jnp.exp(lg[h] * (c_len - 1.0 - order))).T.astype(BF16) for h in H]
    upd = [_dot(kt[h], v[h]) for h in H]
    for h in H:
        decay = jnp.where(seen, jnp.exp(lg_c[h] * dist_f), 0.0)
        y_ref[:, cols[h]] = _dot((inner[h] * decay).astype(BF16), v[h]) + cross[h]
        s_ref[h] = s_ref[h] * jnp.exp(lg[h] * float(c_len)) + upd[h]


def _retention(pb, n_ctx, cos, sin):
    n = pb.shape[0]
    width = pb.shape[1] // 4
    heads = width // B_HEAD
    c_len = RET_CHUNK
    n_chunks = n // c_len
    n_ctx_chunks = n_ctx // c_len
    log_g = jnp.log1p(-jnp.exp2(-5.0 - jnp.arange(heads, dtype=F32)))
    lg = jnp.stack([log_g, log_g[::-1]])
    lg = jnp.broadcast_to(lg[:, :, None, None], (2, heads, SUBLANES, LANES))

    def part(which):
        return pl.BlockSpec((c_len, width), lambda d, c: (_chunk_order(d, c, n_ctx_chunks, n_chunks), which))

    tab = pl.BlockSpec((c_len, B_HEAD), lambda d, c: (_chunk_order(d, c, n_ctx_chunks, n_chunks), 0))
    return pl.pallas_call(
        functools.partial(_retention_kernel, heads=heads),
        grid=(2, n_chunks),
        in_specs=[part(0), part(1), part(2), tab, tab,
                  pl.BlockSpec((None, heads, SUBLANES, LANES), lambda d, c: (d, 0, 0, 0))],
        out_specs=pl.BlockSpec((None, c_len, width),
                               lambda d, c: (d, _chunk_order(d, c, n_ctx_chunks, n_chunks), 0)),
        out_shape=jax.ShapeDtypeStruct((2, n, width), F32),
        scratch_shapes=[pltpu.VMEM((heads, B_HEAD, B_HEAD), F32)],
        compiler_params=_cparams("parallel", "arbitrary"),
        name="retention",
    )(pb, pb, pb, cos, sin, lg)


def _mix_finish_kernel(ya_ref, bonus_ref, g_ref, yb_ref, gate_ref, lg_ref, lb_ref, gg_ref, gb_ref, o_ref):
    wa = bonus_ref.shape[1]
    ya = ya_ref[0] + ya_ref[1]
    seg = _head_seg_matrix(A_HEAD)
    inv = 1.0 / A_HEAD
    for j in range(wa // LANES):
        sl = slice(j * LANES, (j + 1) * LANES)
        t = ya[:, sl]
        mu = _segsum_tile(t, seg) * inv
        tc = t - mu
        var = _segsum_tile(tc * tc, seg) * inv
        o = tc * lax.rsqrt(var + A_GN_EPS) * lg_ref[:, sl] + lb_ref[:, sl] + bonus_ref[:, sl]
        o_ref[:, sl] = (o * g_ref[:, sl]).astype(o_ref.dtype)
    yb = yb_ref[0] + yb_ref[1]
    wb = yb.shape[1]
    for j in range(wb // B_HEAD):
        sl = slice(j * B_HEAD, (j + 1) * B_HEAD)
        t = yb[:, sl]
        mu = jnp.mean(t, axis=-1, keepdims=True)
        tc = t - mu
        var = jnp.mean(tc * tc, axis=-1, keepdims=True)
        o = tc * lax.rsqrt(var + B_GN_EPS) * gg_ref[:, sl] + gb_ref[:, sl]
        o_ref[:, wa + j * B_HEAD:wa + (j + 1) * B_HEAD] = (o * _silu(gate_ref[:, sl])).astype(o_ref.dtype)


def _mix_finish(ya, bonus, g, yb, pb, lnx_g, lnx_b, gn_g, gn_b):
    n, wa = bonus.shape
    wb = yb.shape[2]
    tb = _pick(n, (128, 64, 32, 16, 8))
    ra = pl.BlockSpec((tb, wa), lambda i: (i, 0))
    rb = pl.BlockSpec((tb, wb), lambda i: (i, 3))
    va = pl.BlockSpec((1, wa), lambda i: (0, 0))
    vb = pl.BlockSpec((1, wb), lambda i: (0, 0))
    return pl.pallas_call(
        _mix_finish_kernel,
        grid=(n // tb,),
        in_specs=[pl.BlockSpec((2, tb, wa), lambda i: (0, i, 0)), ra, ra,
                  pl.BlockSpec((2, tb, wb), lambda i: (0, i, 0)), rb, va, va, vb, vb],
        out_specs=pl.BlockSpec((tb, wa + wb), lambda i: (i, 0)),
        out_shape=jax.ShapeDtypeStruct((n, wa + wb), BF16),
        compiler_params=_cparams("parallel"),
        name="mix_finish",
    )(ya, bonus, g, yb, pb, lnx_g.reshape(1, wa), lnx_b.reshape(1, wa), gn_g.reshape(1, wb), gn_b.reshape(1, wb))


def _qkv_prep_kernel(x_ref, cos_ref, sin_ref, qg_ref, kg_ref, q_ref, k_ref, v_ref, *, q_heads, kv_heads):
    cos = cos_ref[...]
    sin = sin_ref[...]
    quarter = C_HEAD // 4
    lane = lax.broadcasted_iota(jnp.int32, cos.shape, 1)
    lower = (lane % (2 * quarter)) < quarter
    scale = C_HEAD ** -0.5 * LOG2_E
    n_heads = q_heads + kv_heads
    tiles = [x_ref[:, h * C_HEAD:(h + 1) * C_HEAD] for h in range(n_heads)]
    ms = [jnp.mean(t * t, axis=-1, keepdims=True) for t in tiles]
    gq = qg_ref[...] * scale
    normed = [t * lax.rsqrt(m + QK_EPS) * (gq if h < q_heads else kg_ref[...])
              for h, (t, m) in enumerate(zip(tiles, ms))]
    up = [pltpu.roll(t, LANES - quarter, 1) for t in normed]
    down = [pltpu.roll(t, quarter, 1) for t in normed]
    for h in range(n_heads):
        y = (normed[h] * cos + jnp.where(lower, up[h], down[h]) * sin).astype(BF16)
        if h < q_heads:
            q_ref[:, h * C_HEAD:(h + 1) * C_HEAD] = y
        else:
            k_ref[:, (h - q_heads) * C_HEAD:(h - q_heads + 1) * C_HEAD] = y
    v0 = n_heads * C_HEAD
    v_ref[...] = x_ref[:, v0:v0 + kv_heads * C_HEAD].astype(BF16)


def _qkv_prep(qkv, cos, sin, qn_g, kn_g, q_heads, kv_heads):
    n, c_in = qkv.shape
    tb = _pick(n, (128, 64, 32, 16, 8))
    kvw = kv_heads * C_HEAD
    return pl.pallas_call(
        functools.partial(_qkv_prep_kernel, q_heads=q_heads, kv_heads=kv_heads),
        grid=(n // tb,),
        in_specs=[pl.BlockSpec((tb, c_in), lambda i: (i, 0)),
                  pl.BlockSpec((tb, C_HEAD), lambda i: (i, 0)),
                  pl.BlockSpec((tb, C_HEAD), lambda i: (i, 0)),
                  pl.BlockSpec((1, C_HEAD), lambda i: (0, 0)),
                  pl.BlockSpec((1, C_HEAD), lambda i: (0, 0))],
        out_specs=[pl.BlockSpec((tb, q_heads * C_HEAD), lambda i: (i, 0)),
                   pl.BlockSpec((tb, kvw), lambda i: (i, 0)),
                   pl.BlockSpec((tb, kvw), lambda i: (i, 0))],
        out_shape=[jax.ShapeDtypeStruct((n, q_heads * C_HEAD), BF16),
                   jax.ShapeDtypeStruct((n, kvw), BF16),
                   jax.ShapeDtypeStruct((n, kvw), BF16)],
        compiler_params=_cparams("parallel"),
        name="qkv_prep",
    )(qkv, cos, sin, qn_g.reshape(1, C_HEAD), kn_g.reshape(1, C_HEAD))


def _flash_kernel(*refs, tq, tk, n_kv, n_cast):
    q_ref, k_ref, v_ref = refs[:3]
    cast_in = refs[3:3 + n_cast]
    o_ref = refs[3 + n_cast]
    cast_out = refs[4 + n_cast:4 + 2 * n_cast]
    q_scr, s_scr, p_scr, m_scr, l_scr, c_scr, acc_scr = refs[4 + 2 * n_cast:]
    for src, dst in zip(cast_in, cast_out):
        dst[...] = src[...].astype(BF16)
    nkv = n_kv // tk
    for h in range(C_GROUP):
        q_scr[h * tq:(h + 1) * tq, :] = q_ref[:, h * C_HEAD:(h + 1) * C_HEAD]
    m_scr[...] = jnp.full(m_scr.shape, -jnp.inf, F32)
    l_scr[...] = jnp.zeros(l_scr.shape, F32)
    acc_scr[...] = jnp.zeros(acc_scr.shape, F32)
    n_col = tk // LANES

    half = s_scr.shape[1] // 2
    halves = (slice(0, half), slice(half, 2 * half))

    def scores(j, slot):
        kj = k_ref[pl.ds(pl.multiple_of(j * tk, tk), tk), :]
        for rs in halves:
            s_scr[slot, rs, :] = _dot_nt(q_scr[rs, :], kj)

    def values(j, slot):
        vj = v_ref[pl.ds(pl.multiple_of(j * tk, tk), tk), :]
        return jnp.concatenate([_dot(p_scr[slot, rs, :], vj) for rs in halves], axis=0)

    def normalise(slot):
        for rs in halves:
            tile_max = s_scr[slot, rs, 0:LANES]
            for c in range(1, n_col):
                tile_max = jnp.maximum(tile_max, s_scr[slot, rs, c * LANES:(c + 1) * LANES])
            row_max = jnp.broadcast_to(jnp.max(tile_max, axis=-1, keepdims=True), tile_max.shape)
            m_old = m_scr[rs, :]
            m_new = jnp.maximum(m_old, row_max)
            corr = jnp.exp2(m_old - m_new)
            part = None
            for c in range(n_col):
                p = jnp.exp2(s_scr[slot, rs, c * LANES:(c + 1) * LANES] - m_new)
                part = p if part is None else part + p
                p_scr[slot, rs, c * LANES:(c + 1) * LANES] = p.astype(BF16)
            l_scr[rs, :] = corr * l_scr[rs, :] + part
            m_scr[rs, :] = m_new
            c_scr[rs, :] = corr

    scores(0, 0)
    if nkv > 1:
        scores(1, 1)
    normalise(0)

    def step(j, cur):
        scores(j + 1, 1 - cur)
        normalise(cur)
        acc_scr[...] = (acc_scr[...] + values(j - 1, 1 - cur)) * c_scr[...]

    def pair(t, carry):
        step(2 * t + 1, 1)
        step(2 * t + 2, 0)
        return carry

    n_mid = max(nkv - 2, 0)
    if n_mid >= 2:
        lax.fori_loop(0, n_mid // 2, pair, 0)
    if n_mid % 2:
        step(nkv - 2, (nkv - 2) % 2)
    if nkv > 1:
        last = (nkv - 1) % 2
        normalise(last)
        acc_scr[...] = (acc_scr[...] + values(nkv - 2, 1 - last)) * c_scr[...]
    acc = acc_scr[...] + values(nkv - 1, (nkv - 1) % 2)
    out = acc / jnp.sum(l_scr[...], axis=-1, keepdims=True)
    for h in range(C_GROUP):
        o_ref[:, h * C_HEAD:(h + 1) * C_HEAD] = out[h * tq:(h + 1) * tq].astype(o_ref.dtype)


def _flash(q, k, v, n_ctx, cast=()):
    n = q.shape[0]
    kv_heads = k.shape[1] // C_HEAD
    n_lat = n - n_ctx
    tq = _pick(n_ctx, (256, 128, 64, 32, 16, 8))
    tk = _pick(n, (1280, 1024, 640, 512, 256, 128, 64, 32, 16, 8))
    gw = C_GROUP * C_HEAD
    q0 = n_ctx // tq
    rows = C_GROUP * tq
    n_qb = n_lat // tq
    steps = kv_heads * n_qb
    slabs = [a.shape[0] * a.shape[1] // steps for a in cast]
    if not all(a.ndim == 3 and a.shape[0] * a.shape[1] == sl * steps and a.shape[1] % sl == 0
               and sl % (2 * SUBLANES) == 0 for a, sl in zip(cast, slabs)):
        return _flash(q, k, v, n_ctx)[0], [a.astype(BF16) for a in cast]

    def slab_spec(a, sl):
        per = a.shape[1] // sl
        return pl.BlockSpec((None, sl, a.shape[2]), lambda g, i: ((g * n_qb + i) // per, (g * n_qb + i) % per, 0))

    cast_specs = [slab_spec(a, sl) for a, sl in zip(cast, slabs)]
    outs = pl.pallas_call(
        functools.partial(_flash_kernel, tq=tq, tk=tk, n_kv=n, n_cast=len(cast)),
        grid=(kv_heads, n_qb),
        in_specs=[pl.BlockSpec((tq, gw), lambda g, i: (q0 + i, g)),
                  pl.BlockSpec((n, C_HEAD), lambda g, i: (0, g)),
                  pl.BlockSpec((n, C_HEAD), lambda g, i: (0, g))] + cast_specs,
        out_specs=[pl.BlockSpec((tq, gw), lambda g, i: (i, g))] + cast_specs,
        out_shape=[jax.ShapeDtypeStruct((n_lat, kv_heads * gw), BF16)]
        + [jax.ShapeDtypeStruct(a.shape, BF16) for a in cast],
        scratch_shapes=[pltpu.VMEM((rows, C_HEAD), BF16),
                        pltpu.VMEM((2, rows, tk), F32),
                        pltpu.VMEM((2, rows, tk), BF16),
                        pltpu.VMEM((rows, LANES), F32),
                        pltpu.VMEM((rows, LANES), F32),
                        pltpu.VMEM((rows, LANES), F32),
                        pltpu.VMEM((rows, C_HEAD), F32)],
        compiler_params=_cparams("parallel", "parallel"),
        name="flash_gqa",
    )(q, k, v, *cast)
    return outs[0], list(outs[1:])


def _pad_cols(w, n):
    return jnp.pad(w, ((0, 0), (0, n - w.shape[1])))


def kernel(x, c, ctx, c_ctx, l0_mod_w, l0_mod_b, l0_w_in, l0_a_mu, l0_a_w0, l0_a_w2, l0_a_a0, l0_a_a2, l0_a_g2, l0_a_k_k, l0_a_k_a, l0_a_r_k, l0_a_lnx_g, l0_a_lnx_b, l0_b_gn_g, l0_b_gn_b, l0_w_out, l0_ln1_g, l0_ln1_b, l0_ffn_w1, l0_ffn_w3, l0_ffn_w2, l0_ln2_g, l0_ln2_b, l1_mod_w, l1_mod_b, l1_w_qkv, l1_q_norm_g, l1_k_norm_g, l1_w_out, l1_ln1_g, l1_ln1_b, l1_router, l1_moe_w1, l1_moe_w3, l1_moe_w2, l1_ln2_g, l1_ln2_b):
    batch, n_lat, d = x.shape
    n_ctx = ctx.shape[1]
    assert batch == 1 and c.shape[0] == 1
    a_width = l0_a_g2.shape[1]
    a_in = l0_a_mu.shape[0]
    b_width = l0_b_gn_g.shape[0]

    c_rows = jnp.zeros((SUBLANES, d), F32).at[0].set(c[0]).at[1].set(c_ctx)
    mod0 = _mod_vectors(c_rows, l0_mod_w, l0_mod_b)
    mod1 = _mod_vectors(c_rows, l1_mod_w, l1_mod_b)

    h = jnp.concatenate([ctx[0], x[0]], axis=0)

    _, u = _ln_mod(h, None, mod0, 0, n_ctx)
    a_pad = _round_up(a_in, 1024) if a_in > 1024 else a_in
    w_in_a = _pad_cols(l0_w_in[:, :a_in], a_pad).astype(BF16)
    w_in_b = l0_w_in[:, a_in:].astype(BF16)
    pa = _mm_plain(u, w_in_a, F32)
    pb = _mm_plain(u, w_in_b, F32)
    r, v, kap, g, bonus, lw, b, kd = _rwkv_prepare(pa, n_ctx, a_in, l0_a_mu, l0_a_w0, l0_a_w2, l0_a_a0, l0_a_a2,
                                                   l0_a_g2, l0_a_k_k, l0_a_k_a, l0_a_r_k)
    ya = _rwkv_scan(r, kap, v, lw, b, kd, n_ctx)
    cos_b, sin_b = _rope_tables(n_ctx, n_lat, B_HEAD // 2)
    yb = _retention(pb, n_ctx, cos_b, sin_b)
    mix = _mix_finish(ya, bonus, g, yb, pb, l0_a_lnx_g, l0_a_lnx_b, l0_b_gn_g, l0_b_gn_b)
    res = _mm_res(mix, l0_w_out.astype(BF16), h, mod0, 2, n_ctx)
    h, u = _ln_mod(res, (l0_ln1_g, l0_ln1_b), mod0, 3, n_ctx)
    d_ff = l0_ffn_w1.shape[1]
    ff_pad = _round_up(d_ff, 1024) if d_ff > 1024 else d_ff
    hid = _mm_swiglu(u, _pad_cols(l0_ffn_w1, ff_pad).astype(BF16), _pad_cols(l0_ffn_w3, ff_pad).astype(BF16))
    w2 = jnp.pad(l0_ffn_w2, ((0, ff_pad - d_ff), (0, 0))).astype(BF16)
    res = _mm_res(hid, w2, h, mod0, 5, n_ctx)
    h, u = _ln_mod(res, (l0_ln2_g, l0_ln2_b), mod1, 0, n_ctx)

    q_heads = l1_w_out.shape[0] // C_HEAD
    kv_heads = q_heads // C_GROUP
    qkv = _mm_plain(u, l1_w_qkv.astype(BF16), F32)
    cos_c, sin_c = _rope_tables(n_ctx, n_lat, C_HEAD // 2)
    q, k, vv = _qkv_prep(qkv, cos_c, sin_c, l1_q_norm_g, l1_k_norm_g, q_heads, kv_heads)
    att, (moe_w1, moe_w3, moe_w2) = _flash(q, k, vv, n_ctx, cast=(l1_moe_w1, l1_moe_w3, l1_moe_w2))
    h_lat = h[n_ctx:]
    res = _mm_res(att, l1_w_out.astype(BF16), h_lat, mod1, 2, 0)
    h_lat, u, comb = _ln_mod(res, (l1_ln1_g, l1_ln1_b), mod1, 3, 0, router=l1_router)
    n_exp = l1_router.shape[1]
    cexp = jnp.broadcast_to(comb[:, :n_exp].T[:, :, None], (n_exp, n_lat, LANES))
    hid = _mm_swiglu(u, moe_w1, moe_w3, cexp)
    w2 = moe_w2.reshape(n_exp * moe_w2.shape[1], d)
    res = _mm_res(hid, w2, h_lat, mod1, 5, 0)
    out, _ = _ln_mod(res, (l1_ln2_g, l1_ln2_b), None, 0, 0)
    return out[None]
```

```python
import functools

import jax
import jax.numpy as jnp
from jax import lax
from jax.experimental import pallas as pl
from jax.experimental.pallas import tpu as pltpu

F32 = jnp.float32
BF16 = jnp.bfloat16
HIGHEST = lax.Precision.HIGHEST

GRID_W = 64
DEPTH = 2
ALPHA = (2 * DEPTH) ** 0.25
LN_EPS = 1e-6
ROPE_THETA = 10000.0
A_HEAD = 64
A_DECAY_LORA = 96
A_AAA_LORA = 96
A_GATE_LORA = 256
A_GN_EPS = 64e-5
B_HEAD = 256
RET_CHUNK = 128
B_GN_EPS = 1e-5
C_HEAD = 128
C_GROUP = 4
QK_EPS = 1e-6
LOG2_E = 1.4426950408889634

LANES = 128
SUBLANES = 8
VMEM_LIMIT_BYTES = 60 * 1024 * 1024

RWKV_CHUNK = 64
PAIR = 2 * A_HEAD


def _cparams(*sem):
    return pltpu.CompilerParams(dimension_semantics=sem, vmem_limit_bytes=VMEM_LIMIT_BYTES)


def _pick(n, candidates):
    for c in candidates:
        if c <= n and n % c == 0:
            return c
    return n


def _round_up(n, m):
    return (n + m - 1) // m * m


def _silu(x):
    return x * (1.0 / (1.0 + jnp.exp(-x)))


def _sigmoid(x):
    return 1.0 / (1.0 + jnp.exp(-x))


def _dot(a, b):
    return jnp.dot(a, b, preferred_element_type=F32)


def _dot_nt(a, b, precision=None):
    return lax.dot_general(a, b, (((1,), (1,)), ((), ())), preferred_element_type=F32, precision=precision)


def _row_is_ctx(shape, row0, n_ctx):
    rows = row0 + lax.broadcasted_iota(jnp.int32, shape, 0)
    return rows < n_ctx


def _select_stream(mod_ref, shape, row0, n_ctx):
    lat = mod_ref[0:1, :]
    if n_ctx == 0:
        return jnp.broadcast_to(lat, shape)
    return jnp.where(_row_is_ctx(shape, row0, n_ctx), mod_ref[1:2, :], lat)


def _modvec_kernel(c_ref, w_ref, b_ref, o_ref):
    s = _silu(c_ref[...])
    o_ref[...] = jnp.dot(s, w_ref[...], preferred_element_type=F32, precision=HIGHEST) + b_ref[...]


def _mod_vectors(c_rows, w, b):
    d, n = w.shape
    tn = _pick(n, (512, 256, 128))
    return pl.pallas_call(
        _modvec_kernel,
        grid=(n // tn,),
        in_specs=[pl.BlockSpec((SUBLANES, d), lambda j: (0, 0)),
                  pl.BlockSpec((d, tn), lambda j: (0, j)),
                  pl.BlockSpec((1, tn), lambda j: (0, j))],
        out_specs=pl.BlockSpec((SUBLANES, tn), lambda j: (0, j)),
        out_shape=jax.ShapeDtypeStruct((SUBLANES, n), F32),
        compiler_params=_cparams("parallel"),
        name="mod_vectors",
    )(c_rows, w, b.reshape(1, n))


def _top2_combine(u, w_ref, n_exp):
    logits = jnp.dot(u, w_ref[...], preferred_element_type=F32, precision=HIGHEST)
    lane = lax.broadcasted_iota(jnp.int32, logits.shape, 1)
    neg = jnp.float32(-jnp.inf)
    l1 = jnp.where(lane < n_exp, logits, neg)
    m1 = jnp.max(l1, axis=-1, keepdims=True)
    i1 = jnp.min(jnp.where(l1 == m1, lane, LANES), axis=-1, keepdims=True)
    l2 = jnp.where(lane == i1, neg, l1)
    m2 = jnp.max(l2, axis=-1, keepdims=True)
    i2 = jnp.min(jnp.where(l2 == m2, lane, LANES), axis=-1, keepdims=True)
    e2 = jnp.exp(m2 - m1)
    return jnp.where(lane == i1, 1.0 / (1.0 + e2), 0.0) + jnp.where(lane == i2, e2 / (1.0 + e2), 0.0)


def _ln_mod_kernel(*refs, tb, n_ctx, do_ln, do_mod, n_exp):
    refs = list(refs)
    r_ref = refs.pop(0)
    x = r_ref[...]
    if do_ln:
        g_ref, b_ref = refs.pop(0), refs.pop(0)
        mu = jnp.mean(x, axis=-1, keepdims=True)
        xc = x - mu
        var = jnp.mean(xc * xc, axis=-1, keepdims=True)
        x = xc * lax.rsqrt(var + LN_EPS) * g_ref[...] + b_ref[...]
    if do_mod:
        sh_ref, sc_ref = refs.pop(0), refs.pop(0)
    if n_exp:
        w_ref = refs.pop(0)
    if do_ln:
        refs.pop(0)[...] = x
    if do_mod:
        row0 = pl.program_id(0) * tb
        sh = _select_stream(sh_ref, x.shape, row0, n_ctx)
        sc = _select_stream(sc_ref, x.shape, row0, n_ctx)
        u = x * (1.0 + sc) + sh
        refs.pop(0)[...] = u.astype(BF16)
        if n_exp:
            refs.pop(0)[...] = _top2_combine(u, w_ref, n_exp)


def _ln_mod(r, ln, mod, k_shift, n_ctx, router=None):
    n, d = r.shape
    do_ln, do_mod = ln is not None, mod is not None
    n_exp = 0 if router is None else router.shape[1]
    tb = _pick(n, (256, 128, 64, 32, 16, 8))
    row = pl.BlockSpec((tb, d), lambda i: (i, 0))
    vec = pl.BlockSpec((1, d), lambda i: (0, 0))
    args, in_specs, out_specs, out_shape = [r], [row], [], []
    if do_ln:
        args += [ln[0].reshape(1, d), ln[1].reshape(1, d)]
        in_specs += [vec, vec]
        out_specs.append(row)
        out_shape.append(jax.ShapeDtypeStruct((n, d), F32))
    if do_mod:
        args += [mod, mod]
        in_specs += [pl.BlockSpec((SUBLANES, d), lambda i: (0, k_shift)),
                     pl.BlockSpec((SUBLANES, d), lambda i: (0, k_shift + 1))]
        out_specs.append(row)
        out_shape.append(jax.ShapeDtypeStruct((n, d), BF16))
    if n_exp:
        args.append(jnp.zeros((d, LANES), F32).at[:, :n_exp].set(router))
        in_specs.append(pl.BlockSpec((d, LANES), lambda i: (0, 0)))
        out_specs.append(pl.BlockSpec((tb, LANES), lambda i: (i, 0)))
        out_shape.append(jax.ShapeDtypeStruct((n, LANES), F32))
    outs = pl.pallas_call(
        functools.partial(_ln_mod_kernel, tb=tb, n_ctx=n_ctx, do_ln=do_ln, do_mod=do_mod, n_exp=n_exp),
        grid=(n // tb,),
        in_specs=in_specs,
        out_specs=out_specs,
        out_shape=out_shape,
        compiler_params=_cparams("parallel"),
        name="ln_mod",
    )(*args)
    outs = list(outs)
    h = outs.pop(0) if do_ln else r
    u = outs.pop(0) if do_mod else None
    return (h, u, outs.pop(0)) if n_exp else (h, u)


def _mm_kernel(*refs, kind, tm, n_ctx, nk):
    n_w = 2 if kind == "swiglu" else 1
    n_acc = n_w if nk > 1 else 0
    accs = refs[len(refs) - n_acc:]
    refs = refs[:len(refs) - n_acc]
    x_ref, w_refs, rest = refs[0], refs[1:1 + n_w], refs[1 + n_w:]
    o_ref = rest[-1]
    k = pl.program_id(2)

    def products():
        x = x_ref[...]
        return [_dot(x, w[...]) for w in w_refs]

    def finish(tot):
        if kind == "swiglu":
            c_ref = rest[0] if len(rest) == 2 else None
            for j in range(o_ref.shape[1] // LANES):
                sl = slice(j * LANES, (j + 1) * LANES)
                y = _silu(tot[0][:, sl]) * tot[1][:, sl]
                if c_ref is not None:
                    y = y * c_ref[...]
                o_ref[:, sl] = y.astype(o_ref.dtype)
        elif kind == "res":
            h_ref, gt_ref = rest[0], rest[1]
            gt = _select_stream(gt_ref, tot[0].shape, pl.program_id(0) * tm, n_ctx)
            o_ref[...] = ALPHA * h_ref[...] + gt * tot[0]
        else:
            o_ref[...] = tot[0].astype(o_ref.dtype)

    if nk == 1:
        finish(products())
        return

    @pl.when(k == 0)
    def _():
        for acc, y in zip(accs, products()):
            acc[...] = y

    @pl.when((k > 0) & (k < nk - 1))
    def _():
        for acc, y in zip(accs, products()):
            acc[...] += y

    @pl.when(k == nk - 1)
    def _():
        finish([acc[...] + y for acc, y in zip(accs, products())])


def _mm_tiles(m, n, k):
    tm = _pick(m, (1280, 1024, 512, 256, 128, 64, 32, 16, 8))
    if k > 2 * 4096:
        return tm, _pick(n, (1024, 512, 256, 128)), _pick(k, (2048, 1024, 512, 256, 128))
    tn = _pick(n, (512, 256, 128))
    tk = _pick(k, (4096, 2816, 2048, 1024, 512, 256, 128))
    return tm, tn, tk


def _mm_scratch(tm, tn, nk, n_w):
    return [pltpu.VMEM((tm, tn), F32)] * n_w if nk > 1 else []


def _mm_plain(x, w, out_dtype):
    m, kd = x.shape
    n = w.shape[1]
    tm, tn, tk = _mm_tiles(m, n, kd)
    nk = kd // tk
    return pl.pallas_call(
        functools.partial(_mm_kernel, kind="plain", tm=tm, n_ctx=0, nk=nk),
        grid=(m // tm, n // tn, nk),
        in_specs=[pl.BlockSpec((tm, tk), lambda i, j, k: (i, k)),
                  pl.BlockSpec((tk, tn), lambda i, j, k: (k, j))],
        out_specs=pl.BlockSpec((tm, tn), lambda i, j, k: (i, j)),
        out_shape=jax.ShapeDtypeStruct((m, n), out_dtype),
        scratch_shapes=_mm_scratch(tm, tn, nk, 1),
        compiler_params=_cparams("parallel", "parallel", "arbitrary"),
        name="mm_plain",
    )(x, w)


def _mm_res(x, w, h, mod, k_gate, n_ctx):
    m, kd = x.shape
    n = w.shape[1]
    tm, tn, tk = _mm_tiles(m, n, kd)
    nk = kd // tk
    gate_block0 = k_gate * (n // tn)
    return pl.pallas_call(
        functools.partial(_mm_kernel, kind="res", tm=tm, n_ctx=n_ctx, nk=nk),
        grid=(m // tm, n // tn, nk),
        in_specs=[pl.BlockSpec((tm, tk), lambda i, j, k: (i, k)),
                  pl.BlockSpec((tk, tn), lambda i, j, k: (k, j)),
                  pl.BlockSpec((tm, tn), lambda i, j, k: (i, j)),
                  pl.BlockSpec((SUBLANES, tn), lambda i, j, k: (0, gate_block0 + j))],
        out_specs=pl.BlockSpec((tm, tn), lambda i, j, k: (i, j)),
        out_shape=jax.ShapeDtypeStruct((m, n), F32),
        scratch_shapes=_mm_scratch(tm, tn, nk, 1),
        compiler_params=_cparams("parallel", "parallel", "arbitrary"),
        name="mm_res",
    )(x, w, h, mod)


def _mm_swiglu(x, w1, w3, cexp=None):
    m, kd = x.shape
    stacked = w1.ndim == 3
    f = w1.shape[-1]
    n_exp = w1.shape[0] if stacked else 1
    tm, tn, tk = _mm_tiles(m, f, kd)
    nk = kd // tk
    nf = f // tn
    if stacked:
        w_spec = pl.BlockSpec((None, tk, tn), lambda i, j, k: (j // nf, k, j % nf))
    else:
        w_spec = pl.BlockSpec((tk, tn), lambda i, j, k: (k, j))
    in_specs = [pl.BlockSpec((tm, tk), lambda i, j, k: (i, k)), w_spec, w_spec]
    args = [x, w1, w3]
    if cexp is not None:
        in_specs.append(pl.BlockSpec((None, tm, LANES), lambda i, j, k: (j // nf, i, 0)))
        args.append(cexp)

    return pl.pallas_call(
        functools.partial(_mm_kernel, kind="swiglu", tm=tm, n_ctx=0, nk=nk),
        grid=(m // tm, n_exp * nf, nk),
        in_specs=in_specs,
        out_specs=pl.BlockSpec((tm, tn), lambda i, j, k: (i, j)),
        out_shape=jax.ShapeDtypeStruct((m, n_exp * f), BF16),
        scratch_shapes=_mm_scratch(tm, tn, nk, 2),
        compiler_params=_cparams("parallel", "parallel", "arbitrary"),
        name="mm_swiglu",
    )(*args)


def _split3(x):
    hi = x.astype(BF16)
    r1 = x - hi.astype(F32)
    mid = r1.astype(BF16)
    lo = (r1 - mid.astype(F32)).astype(BF16)
    return hi, mid, lo


def _head_seg_matrix(width):
    r = lax.broadcasted_iota(jnp.int32, (LANES, LANES), 0) // width
    c = lax.broadcasted_iota(jnp.int32, (LANES, LANES), 1) // width
    return (r == c).astype(BF16)


def _segsum_tile(x, seg):
    hi, mid, lo = _split3(x)
    return _dot(hi, seg) + _dot(mid, seg) + _dot(lo, seg)


def _rwkv_prep_kernel(p_ref, prev_ref, next_ref, mu_ref, w0_ref, w2_ref, a0_ref, a2_ref, g2_ref,
                      kk_ref, ka_ref, rk_ref,
                      r_out, v_out, kap_out, g_out, bonus_out, lw_out, b_out, kd_out,
                      *, tb, n_ctx, n_rows, width, win_w, win_a, o_gl):
    x = p_ref[...]
    row = pl.program_id(0) * tb + lax.broadcasted_iota(jnp.int32, (tb, 1), 0)
    local = lax.broadcasted_iota(jnp.int32, (tb, 1), 0)
    prev = jnp.where(local == 0, prev_ref[SUBLANES - 1:SUBLANES, :], pltpu.roll(x, 1, 0))
    nxt = jnp.where(local == tb - 1, next_ref[0:1, :], pltpu.roll(x, tb - 1, 0))
    prev = jnp.where((row == 0) | (row == n_ctx), 0.0, prev)
    nxt = jnp.where((row == n_ctx - 1) | (row == n_rows - 1), 0.0, nxt)
    x = x + mu_ref[...] * (0.5 * (prev + nxt) - x)

    w = width
    r = x[:, 0:w]
    k = x[:, w:2 * w]
    v = x[:, 2 * w:3 * w]
    gl = x[:, o_gl:o_gl + A_GATE_LORA]
    wl = x[:, win_w[0]:win_w[0] + win_w[1]]
    al = x[:, win_a[0]:win_a[0] + win_a[1]]

    g_out[...] = _dot(_sigmoid(gl).astype(BF16), g2_ref[...])
    lw_lora = _dot(jnp.tanh(wl).astype(BF16), w2_ref[...])
    a_lora = _dot(al.astype(BF16), a2_ref[...])

    kk = k * kk_ref[...]
    seg = _head_seg_matrix(A_HEAD)
    rk = rk_ref[...]
    ka = ka_ref[...]
    kd_sum = jnp.zeros_like(k)
    kaps = []
    for j in range(w // LANES):
        sl = slice(j * LANES, (j + 1) * LANES)
        t = kk[:, sl]
        n2 = _segsum_tile(t * t, seg)
        kaps.append(t / jnp.maximum(jnp.sqrt(n2), 1e-12))
    kap = jnp.concatenate(kaps, axis=-1)
    for d in range(2):
        z = w0_ref[d:d + 1, :] + lw_lora[:, d * w:(d + 1) * w]
        softplus = jnp.maximum(-z, 0.0) + jnp.log(1.0 + jnp.exp(-jnp.abs(z)))
        wlog = -softplus - 0.5
        lw_out[d] = -jnp.exp(wlog)
        a = _sigmoid(a0_ref[d:d + 1, :] + a_lora[:, d * w:(d + 1) * w])
        b_out[d] = kap * a
        kd = k * (1.0 + (a - 1.0) * ka)
        kd_out[d] = kd
        kd_sum = kd_sum + kd
    bon = r * kd_sum * rk
    bons = []
    for j in range(w // LANES):
        sl = slice(j * LANES, (j + 1) * LANES)
        bons.append(_segsum_tile(bon[:, sl], seg))
    bonus_out[...] = jnp.concatenate(bons, axis=-1) * v
    r_out[...] = r
    v_out[...] = v
    kap_out[...] = kap


def _aligned_window(offset, width, limit):
    start = offset // LANES * LANES
    size = min(_round_up(offset + width - start, LANES), limit - start)
    return start, size


def _window_weight(blocks, offset, window, width):
    start, size = window
    lora = blocks.shape[1]
    wt = jnp.zeros((size, 2 * width), F32)
    for d in range(2):
        r0 = offset - start + d * lora
        wt = wt.at[r0:r0 + lora, d * width:(d + 1) * width].set(blocks[d])
    return wt.astype(BF16)


def _rwkv_prepare(pa, n_ctx, a_in, mu, w0, w2, a0, a2, g2, k_k, k_a, r_k):
    n = pa.shape[0]
    w = g2.shape[1]
    tb = _pick(n_ctx if n_ctx else n, (64, 32, 16, 8))
    nblk = n // tb
    o_gl = 3 * w
    o_wl = o_gl + A_GATE_LORA
    o_al = o_wl + 2 * A_DECAY_LORA
    win_w = _aligned_window(o_wl, 2 * A_DECAY_LORA, a_in)
    win_a = _aligned_window(o_al, 2 * A_AAA_LORA, a_in)
    w2w = _window_weight(w2, o_wl, win_w, w)
    a2w = _window_weight(a2, o_al, win_a, w)
    g = tb // SUBLANES
    n_groups = n // SUBLANES
    prev_spec = pl.BlockSpec((SUBLANES, a_in), lambda i: (jnp.maximum(i * g - 1, 0), 0))
    next_spec = pl.BlockSpec((SUBLANES, a_in), lambda i: (jnp.minimum((i + 1) * g, n_groups - 1), 0))

    tok = pl.BlockSpec((tb, w), lambda i: (i, 0))
    tok2 = pl.BlockSpec((2, tb, w), lambda i: (0, i, 0))

    def full(a):
        return pl.BlockSpec(a.shape, lambda i: (0,) * a.ndim)

    params = [mu.reshape(1, a_in), w0, w2w, a0, a2w, g2.astype(BF16), k_k.reshape(1, w), k_a.reshape(1, w),
              r_k.reshape(1, w)]
    one = jax.ShapeDtypeStruct((n, w), F32)
    two = jax.ShapeDtypeStruct((2, n, w), F32)
    return pl.pallas_call(
        functools.partial(_rwkv_prep_kernel, tb=tb, n_ctx=n_ctx, n_rows=n, width=w, win_w=win_w, win_a=win_a,
                          o_gl=o_gl),
        grid=(nblk,),
        in_specs=[pl.BlockSpec((tb, a_in), lambda i: (i, 0)), prev_spec, next_spec] + [full(a) for a in params],
        out_specs=[tok, tok, tok, tok, tok, tok2, tok2, tok2],
        out_shape=[one, one, one, one, one, two, two, two],
        compiler_params=_cparams("parallel"),
        name="rwkv_prepare",
    )(pa, pa, pa, *params)


def _chunk_order(d, c, n_ctx_chunks, n_chunks):
    back = jnp.where(c < n_ctx_chunks, n_ctx_chunks - 1 - c, n_chunks - 1 - (c - n_ctx_chunks))
    return jnp.where(d == 0, c, back)


def _split2(x):
    hi = x.astype(BF16)
    return hi, (x - hi.astype(F32)).astype(BF16)


def _dot_x3(a, b):
    a_hi, a_lo = _split2(a)
    b_hi, b_lo = _split2(b)
    return _dot(a_hi, b_hi) + (_dot(a_lo, b_hi) + _dot(a_hi, b_lo))


def _rwkv_scan_kernel(r_ref, kap_ref, v_ref, lw_ref, b_ref, k_ref, y_ref, s_ref, *, pairs):
    c_len = r_ref.shape[0]
    d = pl.program_id(0)
    sign = jnp.where(d == 1, -1, 1)

    @pl.when(pl.program_id(2) == 0)
    def _():
        s_ref[...] = jnp.zeros_like(s_ref)

    ti = lax.broadcasted_iota(jnp.int32, (c_len, c_len), 0)
    si = lax.broadcasted_iota(jnp.int32, (c_len, c_len), 1)
    upto = ((ti - si) * sign >= 0).astype(BF16)
    lane = lax.broadcasted_iota(jnp.int32, (c_len, PAIR), 1)
    head0 = lane < A_HEAD
    n2 = 2 * c_len
    row = lax.broadcasted_iota(jnp.int32, (n2, n2), 0)
    col = lax.broadcasted_iota(jnp.int32, (n2, n2), 1)
    same = (row // c_len) == (col // c_len)
    order = ((row % c_len) - (col % c_len)) * sign
    strict = same & (order > 0)
    incl = same & (order >= 0)
    eye = row == col

    def stack(x):
        return jnp.concatenate([jnp.where(head0, x, 0.0), jnp.where(head0, 0.0, x)], axis=0)

    def dup(x):
        return jnp.concatenate([x, x], axis=0)

    def bdot(a, b):
        return _dot(a.astype(BF16), b.astype(BF16))

    P = range(pairs)
    lanes = [slice(pi * PAIR, (pi + 1) * PAIR) for pi in P]
    lw = [lw_ref[:, ls] for ls in lanes]
    parts = [_split3(x) for x in lw]
    cl_in = [_dot(upto, hi) + (_dot(upto, mid) + _dot(upto, lo)) for hi, mid, lo in parts]
    cl_tot = [jnp.sum(x, axis=0, keepdims=True) for x in lw]
    g_in = [jnp.exp(c) for c in cl_in]
    g_ex = [jnp.exp(c - x) for c, x in zip(cl_in, lw)]
    g_inv = [jnp.exp(-c) for c in cl_in]
    g_end = [jnp.exp(t - c) for t, c in zip(cl_tot, cl_in)]

    ks = [stack(kap_ref[:, ls] * g).astype(BF16) for ls, g in zip(lanes, g_ex)]
    rs = [stack(r_ref[:, ls] * g) for ls, g in zip(lanes, g_in)]
    vs = [stack(v_ref[:, ls]).astype(BF16) for ls in lanes]
    bk = [jnp.concatenate([dup(b_ref[:, ls] * g), dup(k_ref[:, ls] * g)], axis=0).astype(BF16)
          for ls, g in zip(lanes, g_inv)]
    gram = [_dot_nt(jnp.concatenate([ks[i], rs[i].astype(BF16)], axis=0), bk[i]) for i in P]
    bhs_t = [stack(b_ref[:, ls] * g).T.astype(BF16) for ls, g in zip(lanes, g_end)]
    khs_t = [stack(k_ref[:, ls] * g).T.astype(BF16) for ls, g in zip(lanes, g_end)]
    n_ = [jnp.where(strict, gm[:n2, n2:], 0.0).astype(BF16) for gm in gram]
    lb = [jnp.where(incl, gm[n2:, :n2], 0.0).astype(BF16) for gm in gram]
    lk = [jnp.where(incl, gm[n2:, n2:], 0.0).astype(BF16) for gm in gram]

    pw = [jnp.where(strict, -gm[:n2, :n2], 0.0) for gm in gram]
    tinv = [jnp.where(eye, 1.0, 0.0) + x for x in pw]
    span = 2
    while span < c_len:
        pw = [bdot(x, x) for x in pw]
        tinv = [t + bdot(t, x) for t, x in zip(tinv, pw)]
        span *= 2

    nv = [_dot(n_[i], vs[i]) for i in P]
    kq_u = [bdot(tinv[i], jnp.concatenate([ks[i], nv[i].astype(BF16)], axis=1)).astype(BF16) for i in P]
    low = [_dot(bhs_t[i], kq_u[i]) for i in P]
    lbq = [_dot(lb[i], kq_u[i]) for i in P]
    kv = [_dot(khs_t[i], vs[i]) for i in P]
    lkv = [_dot(lk[i], vs[i]) for i in P]
    for i in P:
        a_ = jnp.where(eye, jnp.exp(cl_tot[i]), 0.0) - low[i][:, :PAIR]
        g_ = kv[i] - low[i][:, PAIR:]
        rq = rs[i] - lbq[i][:, :PAIR]
        yi = lkv[i] - lbq[i][:, PAIR:]
        prop = _dot_x3(jnp.concatenate([rq, a_], axis=0), s_ref[i])
        ys = prop[:n2] + yi
        y_ref[:, lanes[i]] = ys[:c_len] + ys[c_len:]
        s_ref[i] = prop[n2:] + g_


def _rwkv_scan(r, kap, v, lw, b, kd, n_ctx):
    n, w = r.shape
    c_len = RWKV_CHUNK
    n_chunks = n // c_len
    n_ctx_chunks = n_ctx // c_len
    pairs = _pick(w // PAIR, (16, 4, 2, 1))
    bw = pairs * PAIR

    def tok(d, p, c):
        return (_chunk_order(d, c, n_ctx_chunks, n_chunks), p)

    def tok2(d, p, c):
        return (d, _chunk_order(d, c, n_ctx_chunks, n_chunks), p)

    one = pl.BlockSpec((c_len, bw), tok)
    two = pl.BlockSpec((None, c_len, bw), tok2)
    return pl.pallas_call(
        functools.partial(_rwkv_scan_kernel, pairs=pairs),
        grid=(2, w // bw, n_chunks),
        in_specs=[one, one, one, two, two, two],
        out_specs=two,
        out_shape=jax.ShapeDtypeStruct((2, n, w), F32),
        scratch_shapes=[pltpu.VMEM((pairs, PAIR, PAIR), F32)],
        compiler_params=_cparams("parallel", "parallel", "arbitrary"),
        name="rwkv_scan",
    )(r, kap, v, lw, b, kd)


def _rope_tables(n_ctx, n_lat, half):
    nq = half // 2
    pos = jnp.arange(n_lat, dtype=jnp.int32)
    row = (pos // GRID_W).astype(F32)
    col = (pos % GRID_W).astype(F32)
    inv = ROPE_THETA ** (-jnp.arange(nq, dtype=F32) / nq)

    def part(p):
        ang = p[:, None] * inv[None, :]
        c, s = jnp.cos(ang), jnp.sin(ang)
        return jnp.concatenate([c, c], -1), jnp.concatenate([-s, s], -1)

    cr, sr = part(row)
    cc, sc = part(col)
    cos = jnp.concatenate([cr, cc], -1)
    sin = jnp.concatenate([sr, sc], -1)
    cos = jnp.concatenate([jnp.ones((n_ctx, 2 * half), F32), cos], 0)
    sin = jnp.concatenate([jnp.zeros((n_ctx, 2 * half), F32), sin], 0)
    return cos, sin


def _rope_tile(x, cos, sin, quarter):
    if 2 * quarter == LANES:
        partner = pltpu.roll(x, quarter, 1)
    else:
        lane = lax.broadcasted_iota(jnp.int32, x.shape, 1)
        lower = (lane % (2 * quarter)) < quarter
        partner = jnp.where(lower, pltpu.roll(x, LANES - quarter, 1), pltpu.roll(x, quarter, 1))
    return x * cos + partner * sin


def _retention_kernel(q_ref, k_ref, v_ref, cos_ref, sin_ref, lg_ref, y_ref, s_ref, *, heads):
    c_len = q_ref.shape[0]
    d = pl.program_id(0)
    rev = d == 1

    @pl.when(pl.program_id(1) == 0)
    def _():
        s_ref[...] = jnp.zeros_like(s_ref)

    cos = cos_ref[...]
    sin = sin_ref[...]

    def rope(x):
        tiles = [_rope_tile(x[:, j * LANES:(j + 1) * LANES], cos[:, j * LANES:(j + 1) * LANES],
                            sin[:, j * LANES:(j + 1) * LANES], LANES // 2) for j in range(B_HEAD // LANES)]
        return jnp.concatenate(tiles, axis=-1)

    n_i = lax.broadcasted_iota(jnp.int32, (c_len, c_len), 0)
    m_i = lax.broadcasted_iota(jnp.int32, (c_len, c_len), 1)
    dist = (n_i - m_i) * jnp.where(rev, -1, 1)
    seen = dist >= jnp.where(rev, 1, 0)
    dist_f = jnp.maximum(dist, 0).astype(F32)
    idx = lax.broadcasted_iota(jnp.int32, (c_len, B_HEAD), 0)
    order = jnp.where(rev, c_len - 1 - idx, idx).astype(F32)

    H = range(heads)
    cols = [slice(h * B_HEAD, (h + 1) * B_HEAD) for h in H]
    lg = [jnp.concatenate([lg_ref[h, 0:1, :]] * (B_HEAD // LANES), axis=-1) for h in H]
    lg_c = [jnp.concatenate([lg_ref[h, 0:1, :]] * (c_len // LANES), axis=-1) for h in H]
    q = [rope(q_ref[:, cs]) for cs in cols]
    k = [rope(k_ref[:, cs]) * (B_HEAD ** -0.5) for cs in cols]
    v = [v_ref[:, cs].astype(BF16) for cs in cols]
    inner = [_dot_nt(q[h].astype(BF16), k[h].astype(BF16)) for h in H]
    cross = [_dot((q[h] * jnp.exp(lg[h] * (order + 1.0))).astype(BF16), s_ref[h].astype(BF16)) for h in H]
    kt = [(k[h] * jnp.exp(lg[h] * (c_len - 1.0 - order))).T.astype(BF16) for h in H]
    upd = [_dot(kt[h], v[h]) for h in H]
    for h in H:
        decay = jnp.where(seen, jnp.exp(lg_c[h] * dist_f), 0.0)
        y_ref[:, cols[h]] = _dot((inner[h] * decay).astype(BF16), v[h]) + cross[h]
        s_ref[h] = s_ref[h] * jnp.exp(lg[h] * float(c_len)) + upd[h]


def _retention(pb, n_ctx, cos, sin):
    n = pb.shape[0]
    width = pb.shape[1] // 4
    heads = width // B_HEAD
    c_len = RET_CHUNK
    n_chunks = n // c_len
    n_ctx_chunks = n_ctx // c_len
    log_g = jnp.log1p(-jnp.exp2(-5.0 - jnp.arange(heads, dtype=F32)))
    lg = jnp.stack([log_g, log_g[::-1]])
    lg = jnp.broadcast_to(lg[:, :, None, None], (2, heads, SUBLANES, LANES))

    def part(which):
        return pl.BlockSpec((c_len, width), lambda d, c: (_chunk_order(d, c, n_ctx_chunks, n_chunks), which))

    tab = pl.BlockSpec((c_len, B_HEAD), lambda d, c: (_chunk_order(d, c, n_ctx_chunks, n_chunks), 0))
    return pl.pallas_call(
        functools.partial(_retention_kernel, heads=heads),
        grid=(2, n_chunks),
        in_specs=[part(0), part(1), part(2), tab, tab,
                  pl.BlockSpec((None, heads, SUBLANES, LANES), lambda d, c: (d, 0, 0, 0))],
        out_specs=pl.BlockSpec((None, c_len, width),
                               lambda d, c: (d, _chunk_order(d, c, n_ctx_chunks, n_chunks), 0)),
        out_shape=jax.ShapeDtypeStruct((2, n, width), F32),
        scratch_shapes=[pltpu.VMEM((heads, B_HEAD, B_HEAD), F32)],
        compiler_params=_cparams("parallel", "arbitrary"),
        name="retention",
    )(pb, pb, pb, cos, sin, lg)


def _mix_finish_kernel(ya_ref, bonus_ref, g_ref, yb_ref, gate_ref, lg_ref, lb_ref, gg_ref, gb_ref, o_ref):
    wa = bonus_ref.shape[1]
    ya = ya_ref[0] + ya_ref[1]
    seg = _head_seg_matrix(A_HEAD)
    inv = 1.0 / A_HEAD
    for j in range(wa // LANES):
        sl = slice(j * LANES, (j + 1) * LANES)
        t = ya[:, sl]
        mu = _segsum_tile(t, seg) * inv
        tc = t - mu
        var = _segsum_tile(tc * tc, seg) * inv
        o = tc * lax.rsqrt(var + A_GN_EPS) * lg_ref[:, sl] + lb_ref[:, sl] + bonus_ref[:, sl]
        o_ref[:, sl] = (o * g_ref[:, sl]).astype(o_ref.dtype)
    yb = yb_ref[0] + yb_ref[1]
    wb = yb.shape[1]
    for j in range(wb // B_HEAD):
        sl = slice(j * B_HEAD, (j + 1) * B_HEAD)
        t = yb[:, sl]
        mu = jnp.mean(t, axis=-1, keepdims=True)
        tc = t - mu
        var = jnp.mean(tc * tc, axis=-1, keepdims=True)
        o = tc * lax.rsqrt(var + B_GN_EPS) * gg_ref[:, sl] + gb_ref[:, sl]
        o_ref[:, wa + j * B_HEAD:wa + (j + 1) * B_HEAD] = (o * _silu(gate_ref[:, sl])).astype(o_ref.dtype)


def _mix_finish(ya, bonus, g, yb, pb, lnx_g, lnx_b, gn_g, gn_b):
    n, wa = bonus.shape
    wb = yb.shape[2]
    tb = _pick(n, (128, 64, 32, 16, 8))
    ra = pl.BlockSpec((tb, wa), lambda i: (i, 0))
    rb = pl.BlockSpec((tb, wb), lambda i: (i, 3))
    va = pl.BlockSpec((1, wa), lambda i: (0, 0))
    vb = pl.BlockSpec((1, wb), lambda i: (0, 0))
    return pl.pallas_call(
        _mix_finish_kernel,
        grid=(n // tb,),
        in_specs=[pl.BlockSpec((2, tb, wa), lambda i: (0, i, 0)), ra, ra,
                  pl.BlockSpec((2, tb, wb), lambda i: (0, i, 0)), rb, va, va, vb, vb],
        out_specs=pl.BlockSpec((tb, wa + wb), lambda i: (i, 0)),
        out_shape=jax.ShapeDtypeStruct((n, wa + wb), BF16),
        compiler_params=_cparams("parallel"),
        name="mix_finish",
    )(ya, bonus, g, yb, pb, lnx_g.reshape(1, wa), lnx_b.reshape(1, wa), gn_g.reshape(1, wb), gn_b.reshape(1, wb))


def _qkv_prep_kernel(x_ref, cos_ref, sin_ref, qg_ref, kg_ref, q_ref, k_ref, v_ref, *, q_heads, kv_heads):
    cos = cos_ref[...]
    sin = sin_ref[...]
    quarter = C_HEAD // 4
    lane = lax.broadcasted_iota(jnp.int32, cos.shape, 1)
    lower = (lane % (2 * quarter)) < quarter
    scale = C_HEAD ** -0.5 * LOG2_E
    n_heads = q_heads + kv_heads
    tiles = [x_ref[:, h * C_HEAD:(h + 1) * C_HEAD] for h in range(n_heads)]
    ms = [jnp.mean(t * t, axis=-1, keepdims=True) for t in tiles]
    gq = qg_ref[...] * scale
    normed = [t * lax.rsqrt(m + QK_EPS) * (gq if h < q_heads else kg_ref[...])
              for h, (t, m) in enumerate(zip(tiles, ms))]
    up = [pltpu.roll(t, LANES - quarter, 1) for t in normed]
    down = [pltpu.roll(t, quarter, 1) for t in normed]
    for h in range(n_heads):
        y = (normed[h] * cos + jnp.where(lower, up[h], down[h]) * sin).astype(BF16)
        if h < q_heads:
            q_ref[:, h * C_HEAD:(h + 1) * C_HEAD] = y
        else:
            k_ref[:, (h - q_heads) * C_HEAD:(h - q_heads + 1) * C_HEAD] = y
    v0 = n_heads * C_HEAD
    v_ref[...] = x_ref[:, v0:v0 + kv_heads * C_HEAD].astype(BF16)


def _qkv_prep(qkv, cos, sin, qn_g, kn_g, q_heads, kv_heads):
    n, c_in = qkv.shape
    tb = _pick(n, (128, 64, 32, 16, 8))
    kvw = kv_heads * C_HEAD
    return pl.pallas_call(
        functools.partial(_qkv_prep_kernel, q_heads=q_heads, kv_heads=kv_heads),
        grid=(n // tb,),
        in_specs=[pl.BlockSpec((tb, c_in), lambda i: (i, 0)),
                  pl.BlockSpec((tb, C_HEAD), lambda i: (i, 0)),
                  pl.BlockSpec((tb, C_HEAD), lambda i: (i, 0)),
                  pl.BlockSpec((1, C_HEAD), lambda i: (0, 0)),
                  pl.BlockSpec((1, C_HEAD), lambda i: (0, 0))],
        out_specs=[pl.BlockSpec((tb, q_heads * C_HEAD), lambda i: (i, 0)),
                   pl.BlockSpec((tb, kvw), lambda i: (i, 0)),
                   pl.BlockSpec((tb, kvw), lambda i: (i, 0))],
        out_shape=[jax.ShapeDtypeStruct((n, q_heads * C_HEAD), BF16),
                   jax.ShapeDtypeStruct((n, kvw), BF16),
                   jax.ShapeDtypeStruct((n, kvw), BF16)],
        compiler_params=_cparams("parallel"),
        name="qkv_prep",
    )(qkv, cos, sin, qn_g.reshape(1, C_HEAD), kn_g.reshape(1, C_HEAD))


def _flash_kernel(*refs, tq, tk, n_kv, n_cast):
    q_ref, k_ref, v_ref = refs[:3]
    cast_in = refs[3:3 + n_cast]
    o_ref = refs[3 + n_cast]
    cast_out = refs[4 + n_cast:4 + 2 * n_cast]
    q_scr, s_scr, p_scr, m_scr, l_scr, c_scr, acc_scr = refs[4 + 2 * n_cast:]
    for src, dst in zip(cast_in, cast_out):
        dst[...] = src[...].astype(BF16)
    nkv = n_kv // tk
    for h in range(C_GROUP):
        q_scr[h * tq:(h + 1) * tq, :] = q_ref[:, h * C_HEAD:(h + 1) * C_HEAD]
    m_scr[...] = jnp.full(m_scr.shape, -jnp.inf, F32)
    l_scr[...] = jnp.zeros(l_scr.shape, F32)
    acc_scr[...] = jnp.zeros(acc_scr.shape, F32)
    n_col = tk // LANES

    n_part = 4 if s_scr.shape[1] % (4 * 2 * SUBLANES) == 0 else 1
    half = s_scr.shape[1] // n_part
    halves = tuple(slice(i * half, (i + 1) * half) for i in range(n_part))

    def scores(j, slot):
        kj = k_ref[pl.ds(pl.multiple_of(j * tk, tk), tk), :]
        for rs in halves:
            s_scr[slot, rs, :] = _dot_nt(q_scr[rs, :], kj)

    def values(j, slot):
        vj = v_ref[pl.ds(pl.multiple_of(j * tk, tk), tk), :]
        return jnp.concatenate([_dot(p_scr[slot, rs, :], vj) for rs in halves], axis=0)

    def normalise(slot):
        for rs in halves:
            tile_max = s_scr[slot, rs, 0:LANES]
            for c in range(1, n_col):
                tile_max = jnp.maximum(tile_max, s_scr[slot, rs, c * LANES:(c + 1) * LANES])
            row_max = jnp.broadcast_to(jnp.max(tile_max, axis=-1, keepdims=True), tile_max.shape)
            m_old = m_scr[rs, :]
            m_new = jnp.maximum(m_old, row_max)
            corr = jnp.exp2(m_old - m_new)
            part = None
            for c in range(n_col):
                p = jnp.exp2(s_scr[slot, rs, c * LANES:(c + 1) * LANES] - m_new)
                part = p if part is None else part + p
                p_scr[slot, rs, c * LANES:(c + 1) * LANES] = p.astype(BF16)
            l_scr[rs, :] = corr * l_scr[rs, :] + part
            m_scr[rs, :] = m_new
            c_scr[rs, :] = corr

    scores(0, 0)
    if nkv > 1:
        scores(1, 1)
    normalise(0)

    def step(j, cur):
        scores(j + 1, 1 - cur)
        normalise(cur)
        acc_scr[...] = (acc_scr[...] + values(j - 1, 1 - cur)) * c_scr[...]

    def pair(t, carry):
        step(2 * t + 1, 1)
        step(2 * t + 2, 0)
        return carry

    n_mid = max(nkv - 2, 0)
    if n_mid >= 2:
        lax.fori_loop(0, n_mid // 2, pair, 0)
    if n_mid % 2:
        step(nkv - 2, (nkv - 2) % 2)
    if nkv > 1:
        last = (nkv - 1) % 2
        normalise(last)
        acc_scr[...] = (acc_scr[...] + values(nkv - 2, 1 - last)) * c_scr[...]
    acc = acc_scr[...] + values(nkv - 1, (nkv - 1) % 2)
    out = acc / jnp.sum(l_scr[...], axis=-1, keepdims=True)
    for h in range(C_GROUP):
        o_ref[:, h * C_HEAD:(h + 1) * C_HEAD] = out[h * tq:(h + 1) * tq].astype(o_ref.dtype)


def _flash(q, k, v, n_ctx, cast=()):
    n = q.shape[0]
    kv_heads = k.shape[1] // C_HEAD
    n_lat = n - n_ctx
    tq = _pick(n_ctx, (256, 128, 64, 32, 16, 8))
    tk = _pick(n, (1280, 1024, 640, 512, 256, 128, 64, 32, 16, 8))
    gw = C_GROUP * C_HEAD
    q0 = n_ctx // tq
    rows = C_GROUP * tq
    n_qb = n_lat // tq
    steps = kv_heads * n_qb
    slabs = [a.shape[0] * a.shape[1] // steps for a in cast]
    if not all(a.ndim == 3 and a.shape[0] * a.shape[1] == sl * steps and a.shape[1] % sl == 0
               and sl % (2 * SUBLANES) == 0 for a, sl in zip(cast, slabs)):
        return _flash(q, k, v, n_ctx)[0], [a.astype(BF16) for a in cast]

    def slab_spec(a, sl):
        per = a.shape[1] // sl
        return pl.BlockSpec((None, sl, a.shape[2]), lambda g, i: ((g * n_qb + i) // per, (g * n_qb + i) % per, 0))

    cast_specs = [slab_spec(a, sl) for a, sl in zip(cast, slabs)]
    outs = pl.pallas_call(
        functools.partial(_flash_kernel, tq=tq, tk=tk, n_kv=n, n_cast=len(cast)),
        grid=(kv_heads, n_qb),
        in_specs=[pl.BlockSpec((tq, gw), lambda g, i: (q0 + i, g)),
                  pl.BlockSpec((n, C_HEAD), lambda g, i: (0, g)),
                  pl.BlockSpec((n, C_HEAD), lambda g, i: (0, g))] + cast_specs,
        out_specs=[pl.BlockSpec((tq, gw), lambda g, i: (i, g))] + cast_specs,
        out_shape=[jax.ShapeDtypeStruct((n_lat, kv_heads * gw), BF16)]
        + [jax.ShapeDtypeStruct(a.shape, BF16) for a in cast],
        scratch_shapes=[pltpu.VMEM((rows, C_HEAD), BF16),
                        pltpu.VMEM((2, rows, tk), F32),
                        pltpu.VMEM((2, rows, tk), BF16),
                        pltpu.VMEM((rows, LANES), F32),
                        pltpu.VMEM((rows, LANES), F32),
                        pltpu.VMEM((rows, LANES), F32),
                        pltpu.VMEM((rows, C_HEAD), F32)],
        compiler_params=_cparams("parallel", "parallel"),
        name="flash_gqa",
    )(q, k, v, *cast)
    return outs[0], list(outs[1:])


def _pad_cols(w, n):
    return jnp.pad(w, ((0, 0), (0, n - w.shape[1])))


def kernel(x, c, ctx, c_ctx, l0_mod_w, l0_mod_b, l0_w_in, l0_a_mu, l0_a_w0, l0_a_w2, l0_a_a0, l0_a_a2, l0_a_g2, l0_a_k_k, l0_a_k_a, l0_a_r_k, l0_a_lnx_g, l0_a_lnx_b, l0_b_gn_g, l0_b_gn_b, l0_w_out, l0_ln1_g, l0_ln1_b, l0_ffn_w1, l0_ffn_w3, l0_ffn_w2, l0_ln2_g, l0_ln2_b, l1_mod_w, l1_mod_b, l1_w_qkv, l1_q_norm_g, l1_k_norm_g, l1_w_out, l1_ln1_g, l1_ln1_b, l1_router, l1_moe_w1, l1_moe_w3, l1_moe_w2, l1_ln2_g, l1_ln2_b):
    batch, n_lat, d = x.shape
    n_ctx = ctx.shape[1]
    assert batch == 1 and c.shape[0] == 1
    a_width = l0_a_g2.shape[1]
    a_in = l0_a_mu.shape[0]
    b_width = l0_b_gn_g.shape[0]

    c_rows = jnp.zeros((SUBLANES, d), F32).at[0].set(c[0]).at[1].set(c_ctx)
    mod0 = _mod_vectors(c_rows, l0_mod_w, l0_mod_b)
    mod1 = _mod_vectors(c_rows, l1_mod_w, l1_mod_b)

    h = jnp.concatenate([ctx[0], x[0]], axis=0)

    _, u = _ln_mod(h, None, mod0, 0, n_ctx)
    a_pad = _round_up(a_in, 1024) if a_in > 1024 else a_in
    w_in_a = _pad_cols(l0_w_in[:, :a_in], a_pad).astype(BF16)
    w_in_b = l0_w_in[:, a_in:].astype(BF16)
    pa = _mm_plain(u, w_in_a, F32)
    pb = _mm_plain(u, w_in_b, F32)
    r, v, kap, g, bonus, lw, b, kd = _rwkv_prepare(pa, n_ctx, a_in, l0_a_mu, l0_a_w0, l0_a_w2, l0_a_a0, l0_a_a2,
                                                   l0_a_g2, l0_a_k_k, l0_a_k_a, l0_a_r_k)
    ya = _rwkv_scan(r, kap, v, lw, b, kd, n_ctx)
    cos_b, sin_b = _rope_tables(n_ctx, n_lat, B_HEAD // 2)
    yb = _retention(pb, n_ctx, cos_b, sin_b)
    mix = _mix_finish(ya, bonus, g, yb, pb, l0_a_lnx_g, l0_a_lnx_b, l0_b_gn_g, l0_b_gn_b)
    res = _mm_res(mix, l0_w_out.astype(BF16), h, mod0, 2, n_ctx)
    h, u = _ln_mod(res, (l0_ln1_g, l0_ln1_b), mod0, 3, n_ctx)
    d_ff = l0_ffn_w1.shape[1]
    ff_pad = _round_up(d_ff, 1024) if d_ff > 1024 else d_ff
    hid = _mm_swiglu(u, _pad_cols(l0_ffn_w1, ff_pad).astype(BF16), _pad_cols(l0_ffn_w3, ff_pad).astype(BF16))
    w2 = jnp.pad(l0_ffn_w2, ((0, ff_pad - d_ff), (0, 0))).astype(BF16)
    res = _mm_res(hid, w2, h, mod0, 5, n_ctx)
    h, u = _ln_mod(res, (l0_ln2_g, l0_ln2_b), mod1, 0, n_ctx)

    q_heads = l1_w_out.shape[0] // C_HEAD
    kv_heads = q_heads // C_GROUP
    qkv = _mm_plain(u, l1_w_qkv.astype(BF16), F32)
    cos_c, sin_c = _rope_tables(n_ctx, n_lat, C_HEAD // 2)
    q, k, vv = _qkv_prep(qkv, cos_c, sin_c, l1_q_norm_g, l1_k_norm_g, q_heads, kv_heads)
    att, (moe_w1, moe_w3, moe_w2) = _flash(q, k, vv, n_ctx, cast=(l1_moe_w1, l1_moe_w3, l1_moe_w2))
    h_lat = h[n_ctx:]
    res = _mm_res(att, l1_w_out.astype(BF16), h_lat, mod1, 2, 0)
    h_lat, u, comb = _ln_mod(res, (l1_ln1_g, l1_ln1_b), mod1, 3, 0, router=l1_router)
    n_exp = l1_router.shape[1]
    cexp = jnp.broadcast_to(comb[:, :n_exp].T[:, :, None], (n_exp, n_lat, LANES))
    hid = _mm_swiglu(u, moe_w1, moe_w3, cexp)
    w2 = moe_w2.reshape(n_exp * moe_w2.shape[1], d)
    res = _mm_res(hid, w2, h_lat, mod1, 5, 0)
    out, _ = _ln_mod(res, (l1_ln2_g, l1_ln2_b), None, 0, 0)
    return out[None]
```

```python
import functools

import jax
import jax.numpy as jnp
from jax import lax
from jax.experimental import pallas as pl
from jax.experimental.pallas import tpu as pltpu

F32 = jnp.float32
BF16 = jnp.bfloat16
HIGHEST = lax.Precision.HIGHEST

GRID_W = 64
DEPTH = 2
ALPHA = (2 * DEPTH) ** 0.25
LN_EPS = 1e-6
ROPE_THETA = 10000.0
A_HEAD = 64
A_DECAY_LORA = 96
A_AAA_LORA = 96
A_GATE_LORA = 256
A_GN_EPS = 64e-5
B_HEAD = 256
RET_CHUNK = 128
B_GN_EPS = 1e-5
C_HEAD = 128
C_GROUP = 4
QK_EPS = 1e-6
LOG2_E = 1.4426950408889634

LANES = 128
SUBLANES = 8
VMEM_LIMIT_BYTES = 60 * 1024 * 1024

RWKV_CHUNK = 64
PAIR = 2 * A_HEAD


def _cparams(*sem):
    return pltpu.CompilerParams(dimension_semantics=sem, vmem_limit_bytes=VMEM_LIMIT_BYTES)


def _pick(n, candidates):
    for c in candidates:
        if c <= n and n % c == 0:
            return c
    return n


def _round_up(n, m):
    return (n + m - 1) // m * m


def _silu(x):
    return x * (1.0 / (1.0 + jnp.exp(-x)))


def _sigmoid(x):
    return 1.0 / (1.0 + jnp.exp(-x))


def _dot(a, b):
    return jnp.dot(a, b, preferred_element_type=F32)


def _dot_nt(a, b, precision=None):
    return lax.dot_general(a, b, (((1,), (1,)), ((), ())), preferred_element_type=F32, precision=precision)


def _row_is_ctx(shape, row0, n_ctx):
    rows = row0 + lax.broadcasted_iota(jnp.int32, shape, 0)
    return rows < n_ctx


def _select_stream(mod_ref, shape, row0, n_ctx):
    lat = mod_ref[0:1, :]
    if n_ctx == 0:
        return jnp.broadcast_to(lat, shape)
    return jnp.where(_row_is_ctx(shape, row0, n_ctx), mod_ref[1:2, :], lat)


def _modvec_kernel(c_ref, w_ref, b_ref, o_ref):
    s = _silu(c_ref[...])
    o_ref[...] = jnp.dot(s, w_ref[...], preferred_element_type=F32, precision=HIGHEST) + b_ref[...]


def _mod_vectors(c_rows, w, b):
    d, n = w.shape
    tn = _pick(n, (512, 256, 128))
    return pl.pallas_call(
        _modvec_kernel,
        grid=(n // tn,),
        in_specs=[pl.BlockSpec((SUBLANES, d), lambda j: (0, 0)),
                  pl.BlockSpec((d, tn), lambda j: (0, j)),
                  pl.BlockSpec((1, tn), lambda j: (0, j))],
        out_specs=pl.BlockSpec((SUBLANES, tn), lambda j: (0, j)),
        out_shape=jax.ShapeDtypeStruct((SUBLANES, n), F32),
        compiler_params=_cparams("parallel"),
        name="mod_vectors",
    )(c_rows, w, b.reshape(1, n))


def _top2_combine(u, w_ref, n_exp):
    logits = jnp.dot(u, w_ref[...], preferred_element_type=F32, precision=HIGHEST)
    lane = lax.broadcasted_iota(jnp.int32, logits.shape, 1)
    neg = jnp.float32(-jnp.inf)
    l1 = jnp.where(lane < n_exp, logits, neg)
    m1 = jnp.max(l1, axis=-1, keepdims=True)
    i1 = jnp.min(jnp.where(l1 == m1, lane, LANES), axis=-1, keepdims=True)
    l2 = jnp.where(lane == i1, neg, l1)
    m2 = jnp.max(l2, axis=-1, keepdims=True)
    i2 = jnp.min(jnp.where(l2 == m2, lane, LANES), axis=-1, keepdims=True)
    e2 = jnp.exp(m2 - m1)
    return jnp.where(lane == i1, 1.0 / (1.0 + e2), 0.0) + jnp.where(lane == i2, e2 / (1.0 + e2), 0.0)


def _ln_mod_kernel(*refs, tb, n_ctx, do_ln, do_mod, n_exp):
    refs = list(refs)
    r_ref = refs.pop(0)
    x = r_ref[...]
    if do_ln:
        g_ref, b_ref = refs.pop(0), refs.pop(0)
        mu = jnp.mean(x, axis=-1, keepdims=True)
        xc = x - mu
        var = jnp.mean(xc * xc, axis=-1, keepdims=True)
        x = xc * lax.rsqrt(var + LN_EPS) * g_ref[...] + b_ref[...]
    if do_mod:
        sh_ref, sc_ref = refs.pop(0), refs.pop(0)
    if n_exp:
        w_ref = refs.pop(0)
    if do_ln:
        refs.pop(0)[...] = x
    if do_mod:
        row0 = pl.program_id(0) * tb
        sh = _select_stream(sh_ref, x.shape, row0, n_ctx)
        sc = _select_stream(sc_ref, x.shape, row0, n_ctx)
        u = x * (1.0 + sc) + sh
        refs.pop(0)[...] = u.astype(BF16)
        if n_exp:
            refs.pop(0)[...] = _top2_combine(u, w_ref, n_exp)


def _ln_mod(r, ln, mod, k_shift, n_ctx, router=None, h_from=0):
    n, d = r.shape
    do_ln, do_mod = ln is not None, mod is not None
    n_exp = 0 if router is None else router.shape[1]
    tb = _pick(h_from if h_from else n, (256, 128, 64, 32, 16, 8))
    row = pl.BlockSpec((tb, d), lambda i: (i, 0))
    vec = pl.BlockSpec((1, d), lambda i: (0, 0))
    args, in_specs, out_specs, out_shape = [r], [row], [], []
    if do_ln:
        args += [ln[0].reshape(1, d), ln[1].reshape(1, d)]
        in_specs += [vec, vec]
        skip = h_from // tb
        out_specs.append(pl.BlockSpec((tb, d), lambda i: (jnp.maximum(i - skip, 0), 0)))
        out_shape.append(jax.ShapeDtypeStruct((n - h_from, d), F32))
    if do_mod:
        args += [mod, mod]
        in_specs += [pl.BlockSpec((SUBLANES, d), lambda i: (0, k_shift)),
                     pl.BlockSpec((SUBLANES, d), lambda i: (0, k_shift + 1))]
        out_specs.append(row)
        out_shape.append(jax.ShapeDtypeStruct((n, d), BF16))
    if n_exp:
        args.append(jnp.zeros((d, LANES), F32).at[:, :n_exp].set(router))
        in_specs.append(pl.BlockSpec((d, LANES), lambda i: (0, 0)))
        out_specs.append(pl.BlockSpec((tb, LANES), lambda i: (i, 0)))
        out_shape.append(jax.ShapeDtypeStruct((n, LANES), F32))
    outs = pl.pallas_call(
        functools.partial(_ln_mod_kernel, tb=tb, n_ctx=n_ctx, do_ln=do_ln, do_mod=do_mod, n_exp=n_exp),
        grid=(n // tb,),
        in_specs=in_specs,
        out_specs=out_specs,
        out_shape=out_shape,
        compiler_params=_cparams("arbitrary"),
        name="ln_mod",
    )(*args)
    outs = list(outs)
    h = outs.pop(0) if do_ln else r
    u = outs.pop(0) if do_mod else None
    return (h, u, outs.pop(0)) if n_exp else (h, u)


def _mm_kernel(*refs, kind, tm, n_ctx, nk):
    n_w = 2 if kind == "swiglu" else 1
    n_acc = n_w if nk > 1 else 0
    accs = refs[len(refs) - n_acc:]
    refs = refs[:len(refs) - n_acc]
    x_ref, w_refs, rest = refs[0], refs[1:1 + n_w], refs[1 + n_w:]
    o_ref = rest[-1]
    k = pl.program_id(2)

    def products():
        x = x_ref[...]
        return [_dot(x, w[...]) for w in w_refs]

    def finish(tot):
        if kind == "swiglu":
            c_ref = rest[0] if len(rest) == 2 else None
            for j in range(o_ref.shape[1] // LANES):
                sl = slice(j * LANES, (j + 1) * LANES)
                y = _silu(tot[0][:, sl]) * tot[1][:, sl]
                if c_ref is not None:
                    y = y * c_ref[...]
                o_ref[:, sl] = y.astype(o_ref.dtype)
        elif kind == "res":
            h_ref, gt_ref = rest[0], rest[1]
            gt = _select_stream(gt_ref, tot[0].shape, pl.program_id(0) * tm, n_ctx)
            o_ref[...] = ALPHA * h_ref[...] + gt * tot[0]
        else:
            o_ref[...] = tot[0].astype(o_ref.dtype)

    if nk == 1:
        finish(products())
        return

    @pl.when(k == 0)
    def _():
        for acc, y in zip(accs, products()):
            acc[...] = y

    @pl.when((k > 0) & (k < nk - 1))
    def _():
        for acc, y in zip(accs, products()):
            acc[...] += y

    @pl.when(k == nk - 1)
    def _():
        finish([acc[...] + y for acc, y in zip(accs, products())])


def _mm_tiles(m, n, k):
    tm = _pick(m, (1280, 1024, 512, 256, 128, 64, 32, 16, 8))
    if k > 2 * 4096:
        return tm, _pick(n, (1024, 512, 256, 128)), _pick(k, (2048, 1024, 512, 256, 128))
    tn = _pick(n, (512, 256, 128))
    tk = _pick(k, (4096, 2816, 2048, 1024, 512, 256, 128))
    return tm, tn, tk


def _mm_scratch(tm, tn, nk, n_w):
    return [pltpu.VMEM((tm, tn), F32)] * n_w if nk > 1 else []


def _mm_plain(x, w, out_dtype):
    m, kd = x.shape
    n = w.shape[1]
    tm, tn, tk = _mm_tiles(m, n, kd)
    nk = kd // tk
    return pl.pallas_call(
        functools.partial(_mm_kernel, kind="plain", tm=tm, n_ctx=0, nk=nk),
        grid=(m // tm, n // tn, nk),
        in_specs=[pl.BlockSpec((tm, tk), lambda i, j, k: (i, k)),
                  pl.BlockSpec((tk, tn), lambda i, j, k: (k, j))],
        out_specs=pl.BlockSpec((tm, tn), lambda i, j, k: (i, j)),
        out_shape=jax.ShapeDtypeStruct((m, n), out_dtype),
        scratch_shapes=_mm_scratch(tm, tn, nk, 1),
        compiler_params=_cparams("parallel", "parallel", "arbitrary"),
        name="mm_plain",
    )(x, w)


def _mm_res(x, w, h, mod, k_gate, n_ctx):
    m, kd = x.shape
    n = w.shape[1]
    tm, tn, tk = _mm_tiles(m, n, kd)
    nk = kd // tk
    gate_block0 = k_gate * (n // tn)
    return pl.pallas_call(
        functools.partial(_mm_kernel, kind="res", tm=tm, n_ctx=n_ctx, nk=nk),
        grid=(m // tm, n // tn, nk),
        in_specs=[pl.BlockSpec((tm, tk), lambda i, j, k: (i, k)),
                  pl.BlockSpec((tk, tn), lambda i, j, k: (k, j)),
                  pl.BlockSpec((tm, tn), lambda i, j, k: (i, j)),
                  pl.BlockSpec((SUBLANES, tn), lambda i, j, k: (0, gate_block0 + j))],
        out_specs=pl.BlockSpec((tm, tn), lambda i, j, k: (i, j)),
        out_shape=jax.ShapeDtypeStruct((m, n), F32),
        scratch_shapes=_mm_scratch(tm, tn, nk, 1),
        compiler_params=_cparams("parallel", "parallel", "arbitrary"),
        name="mm_res",
    )(x, w, h, mod)


def _mm_swiglu(x, w1, w3, cexp=None):
    m, kd = x.shape
    stacked = w1.ndim == 3
    f = w1.shape[-1]
    n_exp = w1.shape[0] if stacked else 1
    tm, tn, tk = _mm_tiles(m, f, kd)
    nk = kd // tk
    nf = f // tn
    if stacked:
        w_spec = pl.BlockSpec((None, tk, tn), lambda i, j, k: (j // nf, k, j % nf))
    else:
        w_spec = pl.BlockSpec((tk, tn), lambda i, j, k: (k, j))
    in_specs = [pl.BlockSpec((tm, tk), lambda i, j, k: (i, k)), w_spec, w_spec]
    args = [x, w1, w3]
    if cexp is not None:
        in_specs.append(pl.BlockSpec((None, tm, LANES), lambda i, j, k: (j // nf, i, 0)))
        args.append(cexp)

    return pl.pallas_call(
        functools.partial(_mm_kernel, kind="swiglu", tm=tm, n_ctx=0, nk=nk),
        grid=(m // tm, n_exp * nf, nk),
        in_specs=in_specs,
        out_specs=pl.BlockSpec((tm, tn), lambda i, j, k: (i, j)),
        out_shape=jax.ShapeDtypeStruct((m, n_exp * f), BF16),
        scratch_shapes=_mm_scratch(tm, tn, nk, 2),
        compiler_params=_cparams("parallel", "parallel", "arbitrary"),
        name="mm_swiglu",
    )(*args)


def _split3(x):
    hi = x.astype(BF16)
    r1 = x - hi.astype(F32)
    mid = r1.astype(BF16)
    lo = (r1 - mid.astype(F32)).astype(BF16)
    return hi, mid, lo


def _head_seg_matrix(width):
    r = lax.broadcasted_iota(jnp.int32, (LANES, LANES), 0) // width
    c = lax.broadcasted_iota(jnp.int32, (LANES, LANES), 1) // width
    return (r == c).astype(BF16)


def _segsum_tile(x, seg):
    hi, mid, lo = _split3(x)
    return _dot(hi, seg) + _dot(mid, seg) + _dot(lo, seg)


def _rwkv_prep_kernel(p_ref, prev_ref, next_ref, mu_ref, w0_ref, w2_ref, a0_ref, a2_ref, g2_ref,
                      kk_ref, ka_ref, rk_ref,
                      r_out, v_out, kap_out, g_out, bonus_out, lw_out, b_out, kd_out,
                      *, tb, n_ctx, n_rows, width, win_w, win_a, o_gl):
    x = p_ref[...]
    row = pl.program_id(0) * tb + lax.broadcasted_iota(jnp.int32, (tb, 1), 0)
    local = lax.broadcasted_iota(jnp.int32, (tb, 1), 0)
    prev = jnp.where(local == 0, prev_ref[SUBLANES - 1:SUBLANES, :], pltpu.roll(x, 1, 0))
    nxt = jnp.where(local == tb - 1, next_ref[0:1, :], pltpu.roll(x, tb - 1, 0))
    prev = jnp.where((row == 0) | (row == n_ctx), 0.0, prev)
    nxt = jnp.where((row == n_ctx - 1) | (row == n_rows - 1), 0.0, nxt)
    x = x + mu_ref[...] * (0.5 * (prev + nxt) - x)

    w = width
    r = x[:, 0:w]
    k = x[:, w:2 * w]
    v = x[:, 2 * w:3 * w]
    gl = x[:, o_gl:o_gl + A_GATE_LORA]
    wl = x[:, win_w[0]:win_w[0] + win_w[1]]
    al = x[:, win_a[0]:win_a[0] + win_a[1]]

    g_out[...] = _dot(_sigmoid(gl).astype(BF16), g2_ref[...])
    lw_lora = _dot(jnp.tanh(wl).astype(BF16), w2_ref[...])
    a_lora = _dot(al.astype(BF16), a2_ref[...])

    kk = k * kk_ref[...]
    seg = _head_seg_matrix(A_HEAD)
    rk = rk_ref[...]
    ka = ka_ref[...]
    kd_sum = jnp.zeros_like(k)
    kaps = []
    for j in range(w // LANES):
        sl = slice(j * LANES, (j + 1) * LANES)
        t = kk[:, sl]
        n2 = _segsum_tile(t * t, seg)
        kaps.append(t / jnp.maximum(jnp.sqrt(n2), 1e-12))
    kap = jnp.concatenate(kaps, axis=-1)
    for d in range(2):
        z = w0_ref[d:d + 1, :] + lw_lora[:, d * w:(d + 1) * w]
        softplus = jnp.maximum(-z, 0.0) + jnp.log(1.0 + jnp.exp(-jnp.abs(z)))
        wlog = -softplus - 0.5
        lw_out[d] = -jnp.exp(wlog)
        a = _sigmoid(a0_ref[d:d + 1, :] + a_lora[:, d * w:(d + 1) * w])
        b_out[d] = kap * a
        kd = k * (1.0 + (a - 1.0) * ka)
        kd_out[d] = kd
        kd_sum = kd_sum + kd
    bon = r * kd_sum * rk
    bons = []
    for j in range(w // LANES):
        sl = slice(j * LANES, (j + 1) * LANES)
        bons.append(_segsum_tile(bon[:, sl], seg))
    bonus_out[...] = jnp.concatenate(bons, axis=-1) * v
    r_out[...] = r
    v_out[...] = v
    kap_out[...] = kap


def _aligned_window(offset, width, limit):
    start = offset // LANES * LANES
    size = min(_round_up(offset + width - start, LANES), limit - start)
    return start, size


def _window_weight(blocks, offset, window, width):
    start, size = window
    lora = blocks.shape[1]
    wt = jnp.zeros((size, 2 * width), F32)
    for d in range(2):
        r0 = offset - start + d * lora
        wt = wt.at[r0:r0 + lora, d * width:(d + 1) * width].set(blocks[d])
    return wt.astype(BF16)


def _rwkv_prepare(pa, n_ctx, a_in, mu, w0, w2, a0, a2, g2, k_k, k_a, r_k):
    n = pa.shape[0]
    w = g2.shape[1]
    tb = _pick(n_ctx if n_ctx else n, (64, 32, 16, 8))
    nblk = n // tb
    o_gl = 3 * w
    o_wl = o_gl + A_GATE_LORA
    o_al = o_wl + 2 * A_DECAY_LORA
    win_w = _aligned_window(o_wl, 2 * A_DECAY_LORA, a_in)
    win_a = _aligned_window(o_al, 2 * A_AAA_LORA, a_in)
    w2w = _window_weight(w2, o_wl, win_w, w)
    a2w = _window_weight(a2, o_al, win_a, w)
    g = tb // SUBLANES
    n_groups = n // SUBLANES
    prev_spec = pl.BlockSpec((SUBLANES, a_in), lambda i: (jnp.maximum(i * g - 1, 0), 0))
    next_spec = pl.BlockSpec((SUBLANES, a_in), lambda i: (jnp.minimum((i + 1) * g, n_groups - 1), 0))

    tok = pl.BlockSpec((tb, w), lambda i: (i, 0))
    tok2 = pl.BlockSpec((2, tb, w), lambda i: (0, i, 0))

    def full(a):
        return pl.BlockSpec(a.shape, lambda i: (0,) * a.ndim)

    params = [mu.reshape(1, a_in), w0, w2w, a0, a2w, g2.astype(BF16), k_k.reshape(1, w), k_a.reshape(1, w),
              r_k.reshape(1, w)]
    one = jax.ShapeDtypeStruct((n, w), F32)
    two = jax.ShapeDtypeStruct((2, n, w), F32)
    return pl.pallas_call(
        functools.partial(_rwkv_prep_kernel, tb=tb, n_ctx=n_ctx, n_rows=n, width=w, win_w=win_w, win_a=win_a,
                          o_gl=o_gl),
        grid=(nblk,),
        in_specs=[pl.BlockSpec((tb, a_in), lambda i: (i, 0)), prev_spec, next_spec] + [full(a) for a in params],
        out_specs=[tok, tok, tok, tok, tok, tok2, tok2, tok2],
        out_shape=[one, one, one, one, one, two, two, two],
        compiler_params=_cparams("parallel"),
        name="rwkv_prepare",
    )(pa, pa, pa, *params)


def _chunk_order(d, c, n_ctx_chunks, n_chunks):
    back = jnp.where(c < n_ctx_chunks, n_ctx_chunks - 1 - c, n_chunks - 1 - (c - n_ctx_chunks))
    return jnp.where(d == 0, c, back)


def _split2(x):
    hi = x.astype(BF16)
    return hi, (x - hi.astype(F32)).astype(BF16)


def _dot_x3(a, b):
    a_hi, a_lo = _split2(a)
    b_hi, b_lo = _split2(b)
    return _dot(a_hi, b_hi) + (_dot(a_lo, b_hi) + _dot(a_hi, b_lo))


def _rwkv_scan_kernel(*refs, pairs, casts):
    r_ref, kap_ref, v_ref, lw_ref, b_ref, k_ref = refs[:6]
    n_cast = len(casts)
    y_ref = refs[6 + n_cast]
    s_ref = refs[-1]
    c_len = r_ref.shape[0]
    d = pl.program_id(0)
    sign = jnp.where(d == 1, -1, 1)

    @pl.when(pl.program_id(2) == 0)
    def _():
        s_ref[...] = jnp.zeros_like(s_ref)

    step = (d * pl.num_programs(1) + pl.program_id(1)) * pl.num_programs(2) + pl.program_id(2)
    for src, dst, (n_src, n_dst) in zip(refs[6:6 + n_cast], refs[7 + n_cast:7 + 2 * n_cast], casts):
        cols = src.shape[1]
        is_data = jnp.minimum(step, n_dst - 1) < n_src
        dst[:, :cols] = jnp.where(is_data, src[...], 0.0).astype(BF16)
        if dst.shape[1] > cols:
            dst[:, cols:] = jnp.zeros((dst.shape[0], dst.shape[1] - cols), BF16)

    ti = lax.broadcasted_iota(jnp.int32, (c_len, c_len), 0)
    si = lax.broadcasted_iota(jnp.int32, (c_len, c_len), 1)
    upto = ((ti - si) * sign >= 0).astype(BF16)
    lane = lax.broadcasted_iota(jnp.int32, (c_len, PAIR), 1)
    head0 = lane < A_HEAD
    n2 = 2 * c_len
    row = lax.broadcasted_iota(jnp.int32, (n2, n2), 0)
    col = lax.broadcasted_iota(jnp.int32, (n2, n2), 1)
    same = (row // c_len) == (col // c_len)
    order = ((row % c_len) - (col % c_len)) * sign
    strict = same & (order > 0)
    incl = same & (order >= 0)
    eye = row == col

    def stack(x):
        return jnp.concatenate([jnp.where(head0, x, 0.0), jnp.where(head0, 0.0, x)], axis=0)

    def dup(x):
        return jnp.concatenate([x, x], axis=0)

    def bdot(a, b):
        return _dot(a.astype(BF16), b.astype(BF16))

    P = range(pairs)
    lanes = [slice(pi * PAIR, (pi + 1) * PAIR) for pi in P]
    lw = [lw_ref[:, ls] for ls in lanes]
    parts = [_split3(x) for x in lw]
    cl_in = [_dot(upto, hi) + (_dot(upto, mid) + _dot(upto, lo)) for hi, mid, lo in parts]
    cl_tot = [jnp.sum(x, axis=0, keepdims=True) for x in lw]
    g_in = [jnp.exp(c) for c in cl_in]
    g_ex = [jnp.exp(c - x) for c, x in zip(cl_in, lw)]
    g_inv = [jnp.exp(-c) for c in cl_in]
    g_end = [jnp.exp(t - c) for t, c in zip(cl_tot, cl_in)]

    ks = [stack(kap_ref[:, ls] * g).astype(BF16) for ls, g in zip(lanes, g_ex)]
    rs = [stack(r_ref[:, ls] * g) for ls, g in zip(lanes, g_in)]
    vs = [stack(v_ref[:, ls]).astype(BF16) for ls in lanes]
    bk = [jnp.concatenate([dup(b_ref[:, ls] * g), dup(k_ref[:, ls] * g)], axis=0).astype(BF16)
          for ls, g in zip(lanes, g_inv)]
    gram = [_dot_nt(jnp.concatenate([ks[i], rs[i].astype(BF16)], axis=0), bk[i]) for i in P]
    bhs_t = [stack(b_ref[:, ls] * g).T.astype(BF16) for ls, g in zip(lanes, g_end)]
    khs_t = [stack(k_ref[:, ls] * g).T.astype(BF16) for ls, g in zip(lanes, g_end)]
    n_ = [jnp.where(strict, gm[:n2, n2:], 0.0).astype(BF16) for gm in gram]
    lb = [jnp.where(incl, gm[n2:, :n2], 0.0).astype(BF16) for gm in gram]
    lk = [jnp.where(incl, gm[n2:, n2:], 0.0).astype(BF16) for gm in gram]

    pw = [jnp.where(strict, -gm[:n2, :n2], 0.0) for gm in gram]
    tinv = [jnp.where(eye, 1.0, 0.0) + x for x in pw]
    span = 2
    while span < c_len:
        pw = [bdot(x, x) for x in pw]
        tinv = [t + bdot(t, x) for t, x in zip(tinv, pw)]
        span *= 2

    nv = [_dot(n_[i], vs[i]) for i in P]
    kq_u = [bdot(tinv[i], jnp.concatenate([ks[i], nv[i].astype(BF16)], axis=1)).astype(BF16) for i in P]
    low = [_dot(bhs_t[i], kq_u[i]) for i in P]
    lbq = [_dot(lb[i], kq_u[i]) for i in P]
    kv = [_dot(khs_t[i], vs[i]) for i in P]
    lkv = [_dot(lk[i], vs[i]) for i in P]
    for i in P:
        a_ = jnp.where(eye, jnp.exp(cl_tot[i]), 0.0) - low[i][:, :PAIR]
        g_ = kv[i] - low[i][:, PAIR:]
        rq = rs[i] - lbq[i][:, :PAIR]
        yi = lkv[i] - lbq[i][:, PAIR:]
        prop = _dot_x3(jnp.concatenate([rq, a_], axis=0), s_ref[i])
        ys = prop[:n2] + yi
        y_ref[:, lanes[i]] = ys[:c_len] + ys[c_len:]
        s_ref[i] = prop[n2:] + g_


def _side_cast_plan(a, out_shape, steps):
    rows, cols = a.shape
    out_rows, out_cols = out_shape
    if cols % LANES or out_cols % LANES:
        return None
    rb = 2 * SUBLANES
    while rb <= out_rows and (out_rows % rb or rows % rb or out_rows // rb > steps):
        rb *= 2
    if rb > out_rows:
        return None
    return rb, rows // rb, out_rows // rb


def _rwkv_scan(r, kap, v, lw, b, kd, n_ctx, cast=()):
    n, w = r.shape
    c_len = RWKV_CHUNK
    n_chunks = n // c_len
    n_ctx_chunks = n_ctx // c_len
    pairs = _pick(w // PAIR, (16, 4, 2, 1))
    bw = pairs * PAIR
    steps = 2 * (w // bw) * n_chunks
    plans = [_side_cast_plan(a, shape, steps) for a, shape in cast]
    if any(p is None for p in plans):
        y, _ = _rwkv_scan(r, kap, v, lw, b, kd, n_ctx)
        return y, [jnp.pad(a, [(0, t - s) for s, t in zip(a.shape, shape)]).astype(BF16) for a, shape in cast]

    def tok(d, p, c):
        return (_chunk_order(d, c, n_ctx_chunks, n_chunks), p)

    def tok2(d, p, c):
        return (d, _chunk_order(d, c, n_ctx_chunks, n_chunks), p)

    def step_of(d, p, c):
        return (d * (w // bw) + p) * n_chunks + c

    def src_spec(a, plan):
        rb, n_src, _ = plan
        return pl.BlockSpec((rb, a.shape[1]), lambda d, p, c: (jnp.minimum(step_of(d, p, c), n_src - 1), 0))

    def dst_spec(shape, plan):
        rb, _, n_dst = plan
        return pl.BlockSpec((rb, shape[1]), lambda d, p, c: (jnp.minimum(step_of(d, p, c), n_dst - 1), 0))

    one = pl.BlockSpec((c_len, bw), tok)
    two = pl.BlockSpec((None, c_len, bw), tok2)
    outs = pl.pallas_call(
        functools.partial(_rwkv_scan_kernel, pairs=pairs, casts=tuple((p[1], p[2]) for p in plans)),
        grid=(2, w // bw, n_chunks),
        in_specs=[one, one, one, two, two, two] + [src_spec(a, p) for (a, _), p in zip(cast, plans)],
        out_specs=[two] + [dst_spec(shape, p) for (_, shape), p in zip(cast, plans)],
        out_shape=[jax.ShapeDtypeStruct((2, n, w), F32)] + [jax.ShapeDtypeStruct(shape, BF16) for _, shape in cast],
        scratch_shapes=[pltpu.VMEM((pairs, PAIR, PAIR), F32)],
        compiler_params=_cparams("arbitrary", "arbitrary", "arbitrary"),
        name="rwkv_scan",
    )(r, kap, v, lw, b, kd, *[a for a, _ in cast])
    return outs[0], list(outs[1:])


def _rope_tables(n_ctx, n_lat, half):
    nq = half // 2
    pos = jnp.arange(n_lat, dtype=jnp.int32)
    row = (pos // GRID_W).astype(F32)
    col = (pos % GRID_W).astype(F32)
    inv = ROPE_THETA ** (-jnp.arange(nq, dtype=F32) / nq)

    def part(p):
        ang = p[:, None] * inv[None, :]
        c, s = jnp.cos(ang), jnp.sin(ang)
        return jnp.concatenate([c, c], -1), jnp.concatenate([-s, s], -1)

    cr, sr = part(row)
    cc, sc = part(col)
    cos = jnp.concatenate([cr, cc], -1)
    sin = jnp.concatenate([sr, sc], -1)
    cos = jnp.concatenate([jnp.ones((n_ctx, 2 * half), F32), cos], 0)
    sin = jnp.concatenate([jnp.zeros((n_ctx, 2 * half), F32), sin], 0)
    return cos, sin


def _rope_tile(x, cos, sin, quarter):
    if 2 * quarter == LANES:
        partner = pltpu.roll(x, quarter, 1)
    else:
        lane = lax.broadcasted_iota(jnp.int32, x.shape, 1)
        lower = (lane % (2 * quarter)) < quarter
        partner = jnp.where(lower, pltpu.roll(x, LANES - quarter, 1), pltpu.roll(x, quarter, 1))
    return x * cos + partner * sin


def _retention_kernel(q_ref, k_ref, v_ref, cos_ref, sin_ref, lg_ref, y_ref, s_ref, *, heads):
    c_len = q_ref.shape[0]
    d = pl.program_id(0)
    rev = d == 1

    @pl.when(pl.program_id(1) == 0)
    def _():
        s_ref[...] = jnp.zeros_like(s_ref)

    cos = cos_ref[...]
    sin = sin_ref[...]

    def rope(x):
        tiles = [_rope_tile(x[:, j * LANES:(j + 1) * LANES], cos[:, j * LANES:(j + 1) * LANES],
                            sin[:, j * LANES:(j + 1) * LANES], LANES // 2) for j in range(B_HEAD // LANES)]
        return jnp.concatenate(tiles, axis=-1)

    n_i = lax.broadcasted_iota(jnp.int32, (c_len, c_len), 0)
    m_i = lax.broadcasted_iota(jnp.int32, (c_len, c_len), 1)
    dist = (n_i - m_i) * jnp.where(rev, -1, 1)
    seen = dist >= jnp.where(rev, 1, 0)
    dist_f = jnp.maximum(dist, 0).astype(F32)
    idx = lax.broadcasted_iota(jnp.int32, (c_len, B_HEAD), 0)
    order = jnp.where(rev, c_len - 1 - idx, idx).astype(F32)

    H = range(heads)
    cols = [slice(h * B_HEAD, (h + 1) * B_HEAD) for h in H]
    lg = [jnp.concatenate([lg_ref[h, 0:1, :]] * (B_HEAD // LANES), axis=-1) for h in H]
    lg_c = [jnp.concatenate([lg_ref[h, 0:1, :]] * (c_len // LANES), axis=-1) for h in H]
    q = [rope(q_ref[:, cs]) for cs in cols]
    k = [rope(k_ref[:, cs]) * (B_HEAD ** -0.5) for cs in cols]
    v = [v_ref[:, cs].astype(BF16) for cs in cols]
    inner = [_dot_nt(q[h].astype(BF16), k[h].astype(BF16)) for h in H]
    cross = [_dot((q[h] * jnp.exp(lg[h] * (order + 1.0))).astype(BF16), s_ref[h].astype(BF16)) for h in H]
    kt = [(k[h] * jnp.exp(lg[h] * (c_len - 1.0 - order))).T.astype(BF16) for h in H]
    upd = [_dot(kt[h], v[h]) for h in H]
    for h in H:
        decay = jnp.where(seen, jnp.exp(lg_c[h] * dist_f), 0.0)
        y_ref[:, cols[h]] = _dot((inner[h] * decay).astype(BF16), v[h]) + cross[h]
        s_ref[h] = s_ref[h] * jnp.exp(lg[h] * float(c_len)) + upd[h]


def _retention(pb, n_ctx, cos, sin):
    n = pb.shape[0]
    width = pb.shape[1] // 4
    heads = width // B_HEAD
    c_len = RET_CHUNK
    n_chunks = n // c_len
    n_ctx_chunks = n_ctx // c_len
    log_g = jnp.log1p(-jnp.exp2(-5.0 - jnp.arange(heads, dtype=F32)))
    lg = jnp.stack([log_g, log_g[::-1]])
    lg = jnp.broadcast_to(lg[:, :, None, None], (2, heads, SUBLANES, LANES))

    def part(which):
        return pl.BlockSpec((c_len, width), lambda d, c: (_chunk_order(d, c, n_ctx_chunks, n_chunks), which))

    tab = pl.BlockSpec((c_len, B_HEAD), lambda d, c: (_chunk_order(d, c, n_ctx_chunks, n_chunks), 0))
    return pl.pallas_call(
        functools.partial(_retention_kernel, heads=heads),
        grid=(2, n_chunks),
        in_specs=[part(0), part(1), part(2), tab, tab,
                  pl.BlockSpec((None, heads, SUBLANES, LANES), lambda d, c: (d, 0, 0, 0))],
        out_specs=pl.BlockSpec((None, c_len, width),
                               lambda d, c: (d, _chunk_order(d, c, n_ctx_chunks, n_chunks), 0)),
        out_shape=jax.ShapeDtypeStruct((2, n, width), F32),
        scratch_shapes=[pltpu.VMEM((heads, B_HEAD, B_HEAD), F32)],
        compiler_params=_cparams("parallel", "arbitrary"),
        name="retention",
    )(pb, pb, pb, cos, sin, lg)


def _mix_finish_kernel(ya_ref, bonus_ref, g_ref, yb_ref, gate_ref, lg_ref, lb_ref, gg_ref, gb_ref, o_ref):
    wa = bonus_ref.shape[1]
    ya = ya_ref[0] + ya_ref[1]
    seg = _head_seg_matrix(A_HEAD)
    inv = 1.0 / A_HEAD
    for j in range(wa // LANES):
        sl = slice(j * LANES, (j + 1) * LANES)
        t = ya[:, sl]
        mu = _segsum_tile(t, seg) * inv
        tc = t - mu
        var = _segsum_tile(tc * tc, seg) * inv
        o = tc * lax.rsqrt(var + A_GN_EPS) * lg_ref[:, sl] + lb_ref[:, sl] + bonus_ref[:, sl]
        o_ref[:, sl] = (o * g_ref[:, sl]).astype(o_ref.dtype)
    yb = yb_ref[0] + yb_ref[1]
    wb = yb.shape[1]
    for j in range(wb // B_HEAD):
        sl = slice(j * B_HEAD, (j + 1) * B_HEAD)
        t = yb[:, sl]
        mu = jnp.mean(t, axis=-1, keepdims=True)
        tc = t - mu
        var = jnp.mean(tc * tc, axis=-1, keepdims=True)
        o = tc * lax.rsqrt(var + B_GN_EPS) * gg_ref[:, sl] + gb_ref[:, sl]
        o_ref[:, wa + j * B_HEAD:wa + (j + 1) * B_HEAD] = (o * _silu(gate_ref[:, sl])).astype(o_ref.dtype)


def _mix_finish(ya, bonus, g, yb, pb, lnx_g, lnx_b, gn_g, gn_b):
    n, wa = bonus.shape
    wb = yb.shape[2]
    tb = _pick(n, (128, 64, 32, 16, 8))
    ra = pl.BlockSpec((tb, wa), lambda i: (i, 0))
    rb = pl.BlockSpec((tb, wb), lambda i: (i, 3))
    va = pl.BlockSpec((1, wa), lambda i: (0, 0))
    vb = pl.BlockSpec((1, wb), lambda i: (0, 0))
    return pl.pallas_call(
        _mix_finish_kernel,
        grid=(n // tb,),
        in_specs=[pl.BlockSpec((2, tb, wa), lambda i: (0, i, 0)), ra, ra,
                  pl.BlockSpec((2, tb, wb), lambda i: (0, i, 0)), rb, va, va, vb, vb],
        out_specs=pl.BlockSpec((tb, wa + wb), lambda i: (i, 0)),
        out_shape=jax.ShapeDtypeStruct((n, wa + wb), BF16),
        compiler_params=_cparams("parallel"),
        name="mix_finish",
    )(ya, bonus, g, yb, pb, lnx_g.reshape(1, wa), lnx_b.reshape(1, wa), gn_g.reshape(1, wb), gn_b.reshape(1, wb))


def _qkv_prep_kernel(x_ref, cos_ref, sin_ref, qg_ref, kg_ref, q_ref, k_ref, v_ref, *, q_heads, kv_heads):
    cos = cos_ref[...]
    sin = sin_ref[...]
    quarter = C_HEAD // 4
    lane = lax.broadcasted_iota(jnp.int32, cos.shape, 1)
    lower = (lane % (2 * quarter)) < quarter
    scale = C_HEAD ** -0.5 * LOG2_E
    n_heads = q_heads + kv_heads
    tiles = [x_ref[:, h * C_HEAD:(h + 1) * C_HEAD] for h in range(n_heads)]
    ms = [jnp.mean(t * t, axis=-1, keepdims=True) for t in tiles]
    gq = qg_ref[...] * scale
    normed = [t * lax.rsqrt(m + QK_EPS) * (gq if h < q_heads else kg_ref[...])
              for h, (t, m) in enumerate(zip(tiles, ms))]
    up = [pltpu.roll(t, LANES - quarter, 1) for t in normed]
    down = [pltpu.roll(t, quarter, 1) for t in normed]
    for h in range(n_heads):
        y = (normed[h] * cos + jnp.where(lower, up[h], down[h]) * sin).astype(BF16)
        if h < q_heads:
            q_ref[:, h * C_HEAD:(h + 1) * C_HEAD] = y
        else:
            k_ref[:, (h - q_heads) * C_HEAD:(h - q_heads + 1) * C_HEAD] = y
    v0 = n_heads * C_HEAD
    v_ref[...] = x_ref[:, v0:v0 + kv_heads * C_HEAD].astype(BF16)


def _qkv_prep(qkv, cos, sin, qn_g, kn_g, q_heads, kv_heads):
    n, c_in = qkv.shape
    tb = _pick(n, (128, 64, 32, 16, 8))
    kvw = kv_heads * C_HEAD
    return pl.pallas_call(
        functools.partial(_qkv_prep_kernel, q_heads=q_heads, kv_heads=kv_heads),
        grid=(n // tb,),
        in_specs=[pl.BlockSpec((tb, c_in), lambda i: (i, 0)),
                  pl.BlockSpec((tb, C_HEAD), lambda i: (i, 0)),
                  pl.BlockSpec((tb, C_HEAD), lambda i: (i, 0)),
                  pl.BlockSpec((1, C_HEAD), lambda i: (0, 0)),
                  pl.BlockSpec((1, C_HEAD), lambda i: (0, 0))],
        out_specs=[pl.BlockSpec((tb, q_heads * C_HEAD), lambda i: (i, 0)),
                   pl.BlockSpec((tb, kvw), lambda i: (i, 0)),
                   pl.BlockSpec((tb, kvw), lambda i: (i, 0))],
        out_shape=[jax.ShapeDtypeStruct((n, q_heads * C_HEAD), BF16),
                   jax.ShapeDtypeStruct((n, kvw), BF16),
                   jax.ShapeDtypeStruct((n, kvw), BF16)],
        compiler_params=_cparams("parallel"),
        name="qkv_prep",
    )(qkv, cos, sin, qn_g.reshape(1, C_HEAD), kn_g.reshape(1, C_HEAD))


def _flash_kernel(*refs, tq, tk, n_kv, n_cast):
    q_ref, k_ref, v_ref = refs[:3]
    cast_in = refs[3:3 + n_cast]
    o_ref = refs[3 + n_cast]
    cast_out = refs[4 + n_cast:4 + 2 * n_cast]
    q_scr, s_scr, p_scr, m_scr, l_scr, c_scr, acc_scr = refs[4 + 2 * n_cast:]
    for src, dst in zip(cast_in, cast_out):
        dst[...] = src[...].astype(BF16)
    nkv = n_kv // tk
    for h in range(C_GROUP):
        q_scr[h * tq:(h + 1) * tq, :] = q_ref[:, h * C_HEAD:(h + 1) * C_HEAD]
    m_scr[...] = jnp.full(m_scr.shape, -jnp.inf, F32)
    l_scr[...] = jnp.zeros(l_scr.shape, F32)
    acc_scr[...] = jnp.zeros(acc_scr.shape, F32)
    n_col = tk // LANES

    half = s_scr.shape[1] // 2
    halves = (slice(0, half), slice(half, 2 * half))

    def scores(j, slot):
        kj = k_ref[pl.ds(pl.multiple_of(j * tk, tk), tk), :]
        for rs in halves:
            s_scr[slot, rs, :] = _dot_nt(q_scr[rs, :], kj)

    def values(j, slot):
        vj = v_ref[pl.ds(pl.multiple_of(j * tk, tk), tk), :]
        return jnp.concatenate([_dot(p_scr[slot, rs, :], vj) for rs in halves], axis=0)

    def normalise(slot):
        for rs in halves:
            tile_max = s_scr[slot, rs, 0:LANES]
            for c in range(1, n_col):
                tile_max = jnp.maximum(tile_max, s_scr[slot, rs, c * LANES:(c + 1) * LANES])
            row_max = jnp.broadcast_to(jnp.max(tile_max, axis=-1, keepdims=True), tile_max.shape)
            m_old = m_scr[rs, :]
            m_new = jnp.maximum(m_old, row_max)
            corr = jnp.exp2(m_old - m_new)
            part = None
            for c in range(n_col):
                p = jnp.exp2(s_scr[slot, rs, c * LANES:(c + 1) * LANES] - m_new)
                part = p if part is None else part + p
                p_scr[slot, rs, c * LANES:(c + 1) * LANES] = p.astype(BF16)
            l_scr[rs, :] = corr * l_scr[rs, :] + part
            m_scr[rs, :] = m_new
            c_scr[rs, :] = corr

    scores(0, 0)
    if nkv > 1:
        scores(1, 1)
    normalise(0)

    def step(j, cur):
        scores(j + 1, 1 - cur)
        normalise(cur)
        acc_scr[...] = (acc_scr[...] + values(j - 1, 1 - cur)) * c_scr[...]

    def pair(t, carry):
        step(2 * t + 1, 1)
        step(2 * t + 2, 0)
        return carry

    n_mid = max(nkv - 2, 0)
    if n_mid >= 2:
        lax.fori_loop(0, n_mid // 2, pair, 0)
    if n_mid % 2:
        step(nkv - 2, (nkv - 2) % 2)
    if nkv > 1:
        last = (nkv - 1) % 2
        normalise(last)
        acc_scr[...] = (acc_scr[...] + values(nkv - 2, 1 - last)) * c_scr[...]
    acc = acc_scr[...] + values(nkv - 1, (nkv - 1) % 2)
    out = acc / jnp.sum(l_scr[...], axis=-1, keepdims=True)
    for h in range(C_GROUP):
        o_ref[:, h * C_HEAD:(h + 1) * C_HEAD] = out[h * tq:(h + 1) * tq].astype(o_ref.dtype)


def _flash(q, k, v, n_ctx, cast=()):
    n = q.shape[0]
    kv_heads = k.shape[1] // C_HEAD
    n_lat = n - n_ctx
    tq = _pick(n_ctx, (256, 128, 64, 32, 16, 8))
    tk = _pick(n, (1280, 1024, 640, 512, 256, 128, 64, 32, 16, 8))
    gw = C_GROUP * C_HEAD
    q0 = n_ctx // tq
    rows = C_GROUP * tq
    n_qb = n_lat // tq
    steps = kv_heads * n_qb
    slabs = [a.shape[0] * a.shape[1] // steps for a in cast]
    if not all(a.ndim == 3 and a.shape[0] * a.shape[1] == sl * steps and a.shape[1] % sl == 0
               and sl % (2 * SUBLANES) == 0 for a, sl in zip(cast, slabs)):
        return _flash(q, k, v, n_ctx)[0], [a.astype(BF16) for a in cast]

    def slab_spec(a, sl):
        per = a.shape[1] // sl
        return pl.BlockSpec((None, sl, a.shape[2]), lambda g, i: ((g * n_qb + i) // per, (g * n_qb + i) % per, 0))

    cast_specs = [slab_spec(a, sl) for a, sl in zip(cast, slabs)]
    outs = pl.pallas_call(
        functools.partial(_flash_kernel, tq=tq, tk=tk, n_kv=n, n_cast=len(cast)),
        grid=(kv_heads, n_qb),
        in_specs=[pl.BlockSpec((tq, gw), lambda g, i: (q0 + i, g)),
                  pl.BlockSpec((n, C_HEAD), lambda g, i: (0, g)),
                  pl.BlockSpec((n, C_HEAD), lambda g, i: (0, g))] + cast_specs,
        out_specs=[pl.BlockSpec((tq, gw), lambda g, i: (i, g))] + cast_specs,
        out_shape=[jax.ShapeDtypeStruct((n_lat, kv_heads * gw), BF16)]
        + [jax.ShapeDtypeStruct(a.shape, BF16) for a in cast],
        scratch_shapes=[pltpu.VMEM((rows, C_HEAD), BF16),
                        pltpu.VMEM((2, rows, tk), F32),
                        pltpu.VMEM((2, rows, tk), BF16),
                        pltpu.VMEM((rows, LANES), F32),
                        pltpu.VMEM((rows, LANES), F32),
                        pltpu.VMEM((rows, LANES), F32),
                        pltpu.VMEM((rows, C_HEAD), F32)],
        compiler_params=_cparams("parallel", "parallel"),
        name="flash_gqa",
    )(q, k, v, *cast)
    return outs[0], list(outs[1:])


def _pad_cols(w, n):
    return jnp.pad(w, ((0, 0), (0, n - w.shape[1])))


def kernel(x, c, ctx, c_ctx, l0_mod_w, l0_mod_b, l0_w_in, l0_a_mu, l0_a_w0, l0_a_w2, l0_a_a0, l0_a_a2, l0_a_g2, l0_a_k_k, l0_a_k_a, l0_a_r_k, l0_a_lnx_g, l0_a_lnx_b, l0_b_gn_g, l0_b_gn_b, l0_w_out, l0_ln1_g, l0_ln1_b, l0_ffn_w1, l0_ffn_w3, l0_ffn_w2, l0_ln2_g, l0_ln2_b, l1_mod_w, l1_mod_b, l1_w_qkv, l1_q_norm_g, l1_k_norm_g, l1_w_out, l1_ln1_g, l1_ln1_b, l1_router, l1_moe_w1, l1_moe_w3, l1_moe_w2, l1_ln2_g, l1_ln2_b):
    batch, n_lat, d = x.shape
    n_ctx = ctx.shape[1]
    assert batch == 1 and c.shape[0] == 1
    a_width = l0_a_g2.shape[1]
    a_in = l0_a_mu.shape[0]
    b_width = l0_b_gn_g.shape[0]

    c_rows = jnp.zeros((SUBLANES, d), F32).at[0].set(c[0]).at[1].set(c_ctx)
    mod0 = _mod_vectors(c_rows, l0_mod_w, l0_mod_b)
    mod1 = _mod_vectors(c_rows, l1_mod_w, l1_mod_b)

    h = jnp.concatenate([ctx[0], x[0]], axis=0)

    _, u = _ln_mod(h, None, mod0, 0, n_ctx)
    a_pad = _round_up(a_in, 1024) if a_in > 1024 else a_in
    w_in_a = _pad_cols(l0_w_in[:, :a_in], a_pad).astype(BF16)
    w_in_b = l0_w_in[:, a_in:].astype(BF16)
    pa = _mm_plain(u, w_in_a, F32)
    pb = _mm_plain(u, w_in_b, F32)
    r, v, kap, g, bonus, lw, b, kd = _rwkv_prepare(pa, n_ctx, a_in, l0_a_mu, l0_a_w0, l0_a_w2, l0_a_a0, l0_a_a2,
                                                   l0_a_g2, l0_a_k_k, l0_a_k_a, l0_a_r_k)
    d_ff = l0_ffn_w1.shape[1]
    ff_pad = _round_up(d_ff, 1024) if d_ff > 1024 else d_ff
    ya, (ffn_w1, ffn_w3, ffn_w2) = _rwkv_scan(
        r, kap, v, lw, b, kd, n_ctx,
        cast=((l0_ffn_w1, (d, ff_pad)), (l0_ffn_w3, (d, ff_pad)), (l0_ffn_w2, (ff_pad, d))))
    cos_b, sin_b = _rope_tables(n_ctx, n_lat, B_HEAD // 2)
    yb = _retention(pb, n_ctx, cos_b, sin_b)
    mix = _mix_finish(ya, bonus, g, yb, pb, l0_a_lnx_g, l0_a_lnx_b, l0_b_gn_g, l0_b_gn_b)
    res = _mm_res(mix, l0_w_out.astype(BF16), h, mod0, 2, n_ctx)
    h, u = _ln_mod(res, (l0_ln1_g, l0_ln1_b), mod0, 3, n_ctx)
    hid = _mm_swiglu(u, ffn_w1, ffn_w3)
    res = _mm_res(hid, ffn_w2, h, mod0, 5, n_ctx)
    h_lat, u = _ln_mod(res, (l0_ln2_g, l0_ln2_b), mod1, 0, n_ctx, h_from=n_ctx)

    q_heads = l1_w_out.shape[0] // C_HEAD
    kv_heads = q_heads // C_GROUP
    qkv = _mm_plain(u, l1_w_qkv.astype(BF16), F32)
    cos_c, sin_c = _rope_tables(n_ctx, n_lat, C_HEAD // 2)
    q, k, vv = _qkv_prep(qkv, cos_c, sin_c, l1_q_norm_g, l1_k_norm_g, q_heads, kv_heads)
    att, (moe_w1, moe_w3, moe_w2) = _flash(q, k, vv, n_ctx, cast=(l1_moe_w1, l1_moe_w3, l1_moe_w2))
    res = _mm_res(att, l1_w_out.astype(BF16), h_lat, mod1, 2, 0)
    h_lat, u, comb = _ln_mod(res, (l1_ln1_g, l1_ln1_b), mod1, 3, 0, router=l1_router)
    n_exp = l1_router.shape[1]
    cexp = jnp.broadcast_to(comb[:, :n_exp].T[:, :, None], (n_exp, n_lat, LANES))
    hid = _mm_swiglu(u, moe_w1, moe_w3, cexp)
    w2 = moe_w2.reshape(n_exp * moe_w2.shape[1], d)
    res = _mm_res(hid, w2, h_lat, mod1, 5, 0)
    out, _ = _ln_mod(res, (l1_ln2_g, l1_ln2_b), None, 0, 0)
    return out[None]
```

```python
import functools

import jax
import jax.numpy as jnp
from jax import lax
from jax.experimental import pallas as pl
from jax.experimental.pallas import tpu as pltpu

F32 = jnp.float32
BF16 = jnp.bfloat16
HIGHEST = lax.Precision.HIGHEST

GRID_W = 64
DEPTH = 2
ALPHA = (2 * DEPTH) ** 0.25
LN_EPS = 1e-6
ROPE_THETA = 10000.0
A_HEAD = 64
A_DECAY_LORA = 96
A_AAA_LORA = 96
A_GATE_LORA = 256
A_GN_EPS = 64e-5
B_HEAD = 256
RET_CHUNK = 128
B_GN_EPS = 1e-5
C_HEAD = 128
C_GROUP = 4
QK_EPS = 1e-6
LOG2_E = 1.4426950408889634

LANES = 128
SUBLANES = 8
VMEM_LIMIT_BYTES = 60 * 1024 * 1024

RWKV_CHUNK = 64
PAIR = 2 * A_HEAD


def _cparams(*sem):
    return pltpu.CompilerParams(dimension_semantics=sem, vmem_limit_bytes=VMEM_LIMIT_BYTES)


def _pick(n, candidates):
    for c in candidates:
        if c <= n and n % c == 0:
            return c
    return n


def _round_up(n, m):
    return (n + m - 1) // m * m


def _silu(x):
    return x * (1.0 / (1.0 + jnp.exp(-x)))


def _sigmoid(x):
    return 1.0 / (1.0 + jnp.exp(-x))


def _dot(a, b):
    return jnp.dot(a, b, preferred_element_type=F32)


def _dot_nt(a, b, precision=None):
    return lax.dot_general(a, b, (((1,), (1,)), ((), ())), preferred_element_type=F32, precision=precision)


def _row_is_ctx(shape, row0, n_ctx):
    rows = row0 + lax.broadcasted_iota(jnp.int32, shape, 0)
    return rows < n_ctx


def _select_stream(mod_ref, shape, row0, n_ctx):
    lat = mod_ref[0:1, :]
    if n_ctx == 0:
        return jnp.broadcast_to(lat, shape)
    return jnp.where(_row_is_ctx(shape, row0, n_ctx), mod_ref[1:2, :], lat)


def _modvec_kernel(c_ref, w_ref, b_ref, o_ref):
    s = _silu(c_ref[...])
    o_ref[...] = jnp.dot(s, w_ref[...], preferred_element_type=F32, precision=HIGHEST) + b_ref[...]


def _mod_vectors(c_rows, w, b):
    d, n = w.shape
    tn = _pick(n, (512, 256, 128))
    return pl.pallas_call(
        _modvec_kernel,
        grid=(n // tn,),
        in_specs=[pl.BlockSpec((SUBLANES, d), lambda j: (0, 0)),
                  pl.BlockSpec((d, tn), lambda j: (0, j)),
                  pl.BlockSpec((1, tn), lambda j: (0, j))],
        out_specs=pl.BlockSpec((SUBLANES, tn), lambda j: (0, j)),
        out_shape=jax.ShapeDtypeStruct((SUBLANES, n), F32),
        compiler_params=_cparams("parallel"),
        name="mod_vectors",
    )(c_rows, w, b.reshape(1, n))


def _top2_combine(u, w_ref, n_exp):
    logits = jnp.dot(u, w_ref[...], preferred_element_type=F32, precision=HIGHEST)
    lane = lax.broadcasted_iota(jnp.int32, logits.shape, 1)
    neg = jnp.float32(-jnp.inf)
    l1 = jnp.where(lane < n_exp, logits, neg)
    m1 = jnp.max(l1, axis=-1, keepdims=True)
    i1 = jnp.min(jnp.where(l1 == m1, lane, LANES), axis=-1, keepdims=True)
    l2 = jnp.where(lane == i1, neg, l1)
    m2 = jnp.max(l2, axis=-1, keepdims=True)
    i2 = jnp.min(jnp.where(l2 == m2, lane, LANES), axis=-1, keepdims=True)
    e2 = jnp.exp(m2 - m1)
    return jnp.where(lane == i1, 1.0 / (1.0 + e2), 0.0) + jnp.where(lane == i2, e2 / (1.0 + e2), 0.0)


def _ln_mod_kernel(*refs, tb, n_ctx, do_ln, do_mod, n_exp):
    refs = list(refs)
    r_ref = refs.pop(0)
    x = r_ref[...]
    if do_ln:
        g_ref, b_ref = refs.pop(0), refs.pop(0)
        mu = jnp.mean(x, axis=-1, keepdims=True)
        xc = x - mu
        var = jnp.mean(xc * xc, axis=-1, keepdims=True)
        x = xc * lax.rsqrt(var + LN_EPS) * g_ref[...] + b_ref[...]
    if do_mod:
        sh_ref, sc_ref = refs.pop(0), refs.pop(0)
    if n_exp:
        w_ref = refs.pop(0)
    if do_ln:
        refs.pop(0)[...] = x
    if do_mod:
        row0 = pl.program_id(0) * tb
        sh = _select_stream(sh_ref, x.shape, row0, n_ctx)
        sc = _select_stream(sc_ref, x.shape, row0, n_ctx)
        u = x * (1.0 + sc) + sh
        refs.pop(0)[...] = u.astype(BF16)
        if n_exp:
            refs.pop(0)[...] = _top2_combine(u, w_ref, n_exp)


def _ln_mod(r, ln, mod, k_shift, n_ctx, router=None, h_from=0):
    n, d = r.shape
    do_ln, do_mod = ln is not None, mod is not None
    n_exp = 0 if router is None else router.shape[1]
    tb = _pick(h_from if h_from else n, (256, 128, 64, 32, 16, 8))
    row = pl.BlockSpec((tb, d), lambda i: (i, 0))
    vec = pl.BlockSpec((1, d), lambda i: (0, 0))
    args, in_specs, out_specs, out_shape = [r], [row], [], []
    if do_ln:
        args += [ln[0].reshape(1, d), ln[1].reshape(1, d)]
        in_specs += [vec, vec]
        skip = h_from // tb
        out_specs.append(pl.BlockSpec((tb, d), lambda i: (jnp.maximum(i - skip, 0), 0)))
        out_shape.append(jax.ShapeDtypeStruct((n - h_from, d), F32))
    if do_mod:
        args += [mod, mod]
        in_specs += [pl.BlockSpec((SUBLANES, d), lambda i: (0, k_shift)),
                     pl.BlockSpec((SUBLANES, d), lambda i: (0, k_shift + 1))]
        out_specs.append(row)
        out_shape.append(jax.ShapeDtypeStruct((n, d), BF16))
    if n_exp:
        args.append(jnp.zeros((d, LANES), F32).at[:, :n_exp].set(router))
        in_specs.append(pl.BlockSpec((d, LANES), lambda i: (0, 0)))
        out_specs.append(pl.BlockSpec((tb, LANES), lambda i: (i, 0)))
        out_shape.append(jax.ShapeDtypeStruct((n, LANES), F32))
    outs = pl.pallas_call(
        functools.partial(_ln_mod_kernel, tb=tb, n_ctx=n_ctx, do_ln=do_ln, do_mod=do_mod, n_exp=n_exp),
        grid=(n // tb,),
        in_specs=in_specs,
        out_specs=out_specs,
        out_shape=out_shape,
        compiler_params=_cparams("arbitrary"),
        name="ln_mod",
    )(*args)
    outs = list(outs)
    h = outs.pop(0) if do_ln else r
    u = outs.pop(0) if do_mod else None
    return (h, u, outs.pop(0)) if n_exp else (h, u)


def _mm_kernel(*refs, kind, tm, n_ctx, nk):
    n_w = 2 if kind == "swiglu" else 1
    n_acc = n_w if nk > 1 else 0
    accs = refs[len(refs) - n_acc:]
    refs = refs[:len(refs) - n_acc]
    x_ref, w_refs, rest = refs[0], refs[1:1 + n_w], refs[1 + n_w:]
    o_ref = rest[-1]
    k = pl.program_id(2)

    def products():
        x = x_ref[...]
        return [_dot(x, w[...]) for w in w_refs]

    def finish(tot):
        if kind == "swiglu":
            c_ref = rest[0] if len(rest) == 2 else None
            for j in range(o_ref.shape[1] // LANES):
                sl = slice(j * LANES, (j + 1) * LANES)
                y = _silu(tot[0][:, sl]) * tot[1][:, sl]
                if c_ref is not None:
                    y = y * c_ref[...]
                o_ref[:, sl] = y.astype(o_ref.dtype)
        elif kind == "res":
            h_ref, gt_ref = rest[0], rest[1]
            gt = _select_stream(gt_ref, tot[0].shape, pl.program_id(0) * tm, n_ctx)
            o_ref[...] = ALPHA * h_ref[...] + gt * tot[0]
        else:
            o_ref[...] = tot[0].astype(o_ref.dtype)

    if nk == 1:
        finish(products())
        return

    @pl.when(k == 0)
    def _():
        for acc, y in zip(accs, products()):
            acc[...] = y

    @pl.when((k > 0) & (k < nk - 1))
    def _():
        for acc, y in zip(accs, products()):
            acc[...] += y

    @pl.when(k == nk - 1)
    def _():
        finish([acc[...] + y for acc, y in zip(accs, products())])


def _mm_tiles(m, n, k):
    tm = _pick(m, (1280, 1024, 512, 256, 128, 64, 32, 16, 8))
    if k > 2 * 4096:
        return tm, _pick(n, (1024, 512, 256, 128)), _pick(k, (2048, 1024, 512, 256, 128))
    tn = _pick(n, (512, 256, 128))
    tk = _pick(k, (4096, 2816, 2048, 1024, 512, 256, 128))
    return tm, tn, tk


def _mm_scratch(tm, tn, nk, n_w):
    return [pltpu.VMEM((tm, tn), F32)] * n_w if nk > 1 else []


def _mm_plain(x, w, out_dtype):
    m, kd = x.shape
    n = w.shape[1]
    tm, tn, tk = _mm_tiles(m, n, kd)
    nk = kd // tk
    return pl.pallas_call(
        functools.partial(_mm_kernel, kind="plain", tm=tm, n_ctx=0, nk=nk),
        grid=(m // tm, n // tn, nk),
        in_specs=[pl.BlockSpec((tm, tk), lambda i, j, k: (i, k)),
                  pl.BlockSpec((tk, tn), lambda i, j, k: (k, j))],
        out_specs=pl.BlockSpec((tm, tn), lambda i, j, k: (i, j)),
        out_shape=jax.ShapeDtypeStruct((m, n), out_dtype),
        scratch_shapes=_mm_scratch(tm, tn, nk, 1),
        compiler_params=_cparams("parallel", "parallel", "arbitrary"),
        name="mm_plain",
    )(x, w)


def _mm_res(x, w, h, mod, k_gate, n_ctx):
    m, kd = x.shape
    n = w.shape[1]
    tm, tn, tk = _mm_tiles(m, n, kd)
    nk = kd // tk
    gate_block0 = k_gate * (n // tn)
    return pl.pallas_call(
        functools.partial(_mm_kernel, kind="res", tm=tm, n_ctx=n_ctx, nk=nk),
        grid=(m // tm, n // tn, nk),
        in_specs=[pl.BlockSpec((tm, tk), lambda i, j, k: (i, k)),
                  pl.BlockSpec((tk, tn), lambda i, j, k: (k, j)),
                  pl.BlockSpec((tm, tn), lambda i, j, k: (i, j)),
                  pl.BlockSpec((SUBLANES, tn), lambda i, j, k: (0, gate_block0 + j))],
        out_specs=pl.BlockSpec((tm, tn), lambda i, j, k: (i, j)),
        out_shape=jax.ShapeDtypeStruct((m, n), F32),
        scratch_shapes=_mm_scratch(tm, tn, nk, 1),
        compiler_params=_cparams("parallel", "parallel", "arbitrary"),
        name="mm_res",
    )(x, w, h, mod)


def _mm_swiglu(x, w1, w3, cexp=None):
    m, kd = x.shape
    stacked = w1.ndim == 3
    f = w1.shape[-1]
    n_exp = w1.shape[0] if stacked else 1
    tm, tn, tk = _mm_tiles(m, f, kd)
    nk = kd // tk
    nf = f // tn
    if stacked:
        w_spec = pl.BlockSpec((None, tk, tn), lambda i, j, k: (j // nf, k, j % nf))
    else:
        w_spec = pl.BlockSpec((tk, tn), lambda i, j, k: (k, j))
    in_specs = [pl.BlockSpec((tm, tk), lambda i, j, k: (i, k)), w_spec, w_spec]
    args = [x, w1, w3]
    if cexp is not None:
        in_specs.append(pl.BlockSpec((None, tm, LANES), lambda i, j, k: (j // nf, i, 0)))
        args.append(cexp)

    return pl.pallas_call(
        functools.partial(_mm_kernel, kind="swiglu", tm=tm, n_ctx=0, nk=nk),
        grid=(m // tm, n_exp * nf, nk),
        in_specs=in_specs,
        out_specs=pl.BlockSpec((tm, tn), lambda i, j, k: (i, j)),
        out_shape=jax.ShapeDtypeStruct((m, n_exp * f), BF16),
        scratch_shapes=_mm_scratch(tm, tn, nk, 2),
        compiler_params=_cparams("parallel", "parallel", "arbitrary"),
        name="mm_swiglu",
    )(*args)


def _split3(x):
    hi = x.astype(BF16)
    r1 = x - hi.astype(F32)
    mid = r1.astype(BF16)
    lo = (r1 - mid.astype(F32)).astype(BF16)
    return hi, mid, lo


def _head_seg_matrix(width):
    r = lax.broadcasted_iota(jnp.int32, (LANES, LANES), 0) // width
    c = lax.broadcasted_iota(jnp.int32, (LANES, LANES), 1) // width
    return (r == c).astype(BF16)


def _segsum_tile(x, seg):
    hi, mid, lo = _split3(x)
    return _dot(hi, seg) + _dot(mid, seg) + _dot(lo, seg)


def _rwkv_prep_kernel(p_ref, prev_ref, next_ref, mu_ref, w0_ref, w2_ref, a0_ref, a2_ref, g2_ref,
                      kk_ref, ka_ref, rk_ref,
                      r_out, v_out, kap_out, g_out, bonus_out, lw_out, b_out, kd_out,
                      *, tb, n_ctx, n_rows, width, win_w, win_a, o_gl):
    x = p_ref[...]
    row = pl.program_id(0) * tb + lax.broadcasted_iota(jnp.int32, (tb, 1), 0)
    local = lax.broadcasted_iota(jnp.int32, (tb, 1), 0)
    prev = jnp.where(local == 0, prev_ref[SUBLANES - 1:SUBLANES, :], pltpu.roll(x, 1, 0))
    nxt = jnp.where(local == tb - 1, next_ref[0:1, :], pltpu.roll(x, tb - 1, 0))
    prev = jnp.where((row == 0) | (row == n_ctx), 0.0, prev)
    nxt = jnp.where((row == n_ctx - 1) | (row == n_rows - 1), 0.0, nxt)
    x = x + mu_ref[...] * (0.5 * (prev + nxt) - x)

    w = width
    r = x[:, 0:w]
    k = x[:, w:2 * w]
    v = x[:, 2 * w:3 * w]
    gl = x[:, o_gl:o_gl + A_GATE_LORA]
    wl = x[:, win_w[0]:win_w[0] + win_w[1]]
    al = x[:, win_a[0]:win_a[0] + win_a[1]]

    g_out[...] = _dot(_sigmoid(gl).astype(BF16), g2_ref[...])
    lw_lora = _dot(jnp.tanh(wl).astype(BF16), w2_ref[...])
    a_lora = _dot(al.astype(BF16), a2_ref[...])

    kk = k * kk_ref[...]
    seg = _head_seg_matrix(A_HEAD)
    rk = rk_ref[...]
    ka = ka_ref[...]
    kd_sum = jnp.zeros_like(k)
    kaps = []
    for j in range(w // LANES):
        sl = slice(j * LANES, (j + 1) * LANES)
        t = kk[:, sl]
        n2 = _segsum_tile(t * t, seg)
        kaps.append(t / jnp.maximum(jnp.sqrt(n2), 1e-12))
    kap = jnp.concatenate(kaps, axis=-1)
    for d in range(2):
        z = w0_ref[d:d + 1, :] + lw_lora[:, d * w:(d + 1) * w]
        softplus = jnp.maximum(-z, 0.0) + jnp.log(1.0 + jnp.exp(-jnp.abs(z)))
        wlog = -softplus - 0.5
        lw_out[d] = -jnp.exp(wlog)
        a = _sigmoid(a0_ref[d:d + 1, :] + a_lora[:, d * w:(d + 1) * w])
        b_out[d] = kap * a
        kd = k * (1.0 + (a - 1.0) * ka)
        kd_out[d] = kd
        kd_sum = kd_sum + kd
    bon = r * kd_sum * rk
    bons = []
    for j in range(w // LANES):
        sl = slice(j * LANES, (j + 1) * LANES)
        bons.append(_segsum_tile(bon[:, sl], seg))
    bonus_out[...] = jnp.concatenate(bons, axis=-1) * v
    r_out[...] = r
    v_out[...] = v
    kap_out[...] = kap


def _aligned_window(offset, width, limit):
    start = offset // LANES * LANES
    size = min(_round_up(offset + width - start, LANES), limit - start)
    return start, size


def _window_weight(blocks, offset, window, width):
    start, size = window
    lora = blocks.shape[1]
    wt = jnp.zeros((size, 2 * width), F32)
    for d in range(2):
        r0 = offset - start + d * lora
        wt = wt.at[r0:r0 + lora, d * width:(d + 1) * width].set(blocks[d])
    return wt.astype(BF16)


def _rwkv_prepare(pa, n_ctx, a_in, mu, w0, w2, a0, a2, g2, k_k, k_a, r_k):
    n = pa.shape[0]
    w = g2.shape[1]
    tb = _pick(n_ctx if n_ctx else n, (64, 32, 16, 8))
    nblk = n // tb
    o_gl = 3 * w
    o_wl = o_gl + A_GATE_LORA
    o_al = o_wl + 2 * A_DECAY_LORA
    win_w = _aligned_window(o_wl, 2 * A_DECAY_LORA, a_in)
    win_a = _aligned_window(o_al, 2 * A_AAA_LORA, a_in)
    w2w = _window_weight(w2, o_wl, win_w, w)
    a2w = _window_weight(a2, o_al, win_a, w)
    g = tb // SUBLANES
    n_groups = n // SUBLANES
    prev_spec = pl.BlockSpec((SUBLANES, a_in), lambda i: (jnp.maximum(i * g - 1, 0), 0))
    next_spec = pl.BlockSpec((SUBLANES, a_in), lambda i: (jnp.minimum((i + 1) * g, n_groups - 1), 0))

    tok = pl.BlockSpec((tb, w), lambda i: (i, 0))
    tok2 = pl.BlockSpec((2, tb, w), lambda i: (0, i, 0))

    def full(a):
        return pl.BlockSpec(a.shape, lambda i: (0,) * a.ndim)

    params = [mu.reshape(1, a_in), w0, w2w, a0, a2w, g2.astype(BF16), k_k.reshape(1, w), k_a.reshape(1, w),
              r_k.reshape(1, w)]
    one = jax.ShapeDtypeStruct((n, w), F32)
    two = jax.ShapeDtypeStruct((2, n, w), F32)
    return pl.pallas_call(
        functools.partial(_rwkv_prep_kernel, tb=tb, n_ctx=n_ctx, n_rows=n, width=w, win_w=win_w, win_a=win_a,
                          o_gl=o_gl),
        grid=(nblk,),
        in_specs=[pl.BlockSpec((tb, a_in), lambda i: (i, 0)), prev_spec, next_spec] + [full(a) for a in params],
        out_specs=[tok, tok, tok, tok, tok, tok2, tok2, tok2],
        out_shape=[one, one, one, one, one, two, two, two],
        compiler_params=_cparams("parallel"),
        name="rwkv_prepare",
    )(pa, pa, pa, *params)


def _chunk_order(d, c, n_ctx_chunks, n_chunks):
    back = jnp.where(c < n_ctx_chunks, n_ctx_chunks - 1 - c, n_chunks - 1 - (c - n_ctx_chunks))
    return jnp.where(d == 0, c, back)


def _split2(x):
    hi = x.astype(BF16)
    return hi, (x - hi.astype(F32)).astype(BF16)


def _dot_x3(a, b):
    a_hi, a_lo = _split2(a)
    b_hi, b_lo = _split2(b)
    return _dot(a_hi, b_hi) + (_dot(a_lo, b_hi) + _dot(a_hi, b_lo))


def _rwkv_scan_kernel(*refs, pairs, casts):
    r_ref, kap_ref, v_ref, lw_ref, b_ref, k_ref = refs[:6]
    n_cast = len(casts)
    y_ref = refs[6 + n_cast]
    s_ref = refs[-1]
    c_len = r_ref.shape[0]
    d = pl.program_id(0)
    sign = jnp.where(d == 1, -1, 1)

    @pl.when(pl.program_id(2) == 0)
    def _():
        s_ref[...] = jnp.zeros_like(s_ref)

    step = (d * pl.num_programs(1) + pl.program_id(1)) * pl.num_programs(2) + pl.program_id(2)
    for src, dst, (n_src, n_dst) in zip(refs[6:6 + n_cast], refs[7 + n_cast:7 + 2 * n_cast], casts):
        cols = src.shape[1]
        is_data = jnp.minimum(step, n_dst - 1) < n_src
        dst[:, :cols] = jnp.where(is_data, src[...], 0.0).astype(BF16)
        if dst.shape[1] > cols:
            dst[:, cols:] = jnp.zeros((dst.shape[0], dst.shape[1] - cols), BF16)

    ti = lax.broadcasted_iota(jnp.int32, (c_len, c_len), 0)
    si = lax.broadcasted_iota(jnp.int32, (c_len, c_len), 1)
    upto = ((ti - si) * sign >= 0).astype(BF16)
    lane = lax.broadcasted_iota(jnp.int32, (c_len, PAIR), 1)
    head0 = lane < A_HEAD
    n2 = 2 * c_len
    row = lax.broadcasted_iota(jnp.int32, (n2, n2), 0)
    col = lax.broadcasted_iota(jnp.int32, (n2, n2), 1)
    same = (row // c_len) == (col // c_len)
    order = ((row % c_len) - (col % c_len)) * sign
    strict = same & (order > 0)
    incl = same & (order >= 0)
    eye = row == col

    def stack(x):
        return jnp.concatenate([jnp.where(head0, x, 0.0), jnp.where(head0, 0.0, x)], axis=0)

    def dup(x):
        return jnp.concatenate([x, x], axis=0)

    def bdot(a, b):
        return _dot(a.astype(BF16), b.astype(BF16))

    P = range(pairs)
    lanes = [slice(pi * PAIR, (pi + 1) * PAIR) for pi in P]
    lw = [lw_ref[:, ls] for ls in lanes]
    parts = [_split3(x) for x in lw]
    cl_in = [_dot(upto, hi) + (_dot(upto, mid) + _dot(upto, lo)) for hi, mid, lo in parts]
    cl_tot = [jnp.sum(x, axis=0, keepdims=True) for x in lw]
    g_in = [jnp.exp(c) for c in cl_in]
    g_ex = [jnp.exp(c - x) for c, x in zip(cl_in, lw)]
    g_inv = [jnp.exp(-c) for c in cl_in]
    g_end = [jnp.exp(t - c) for t, c in zip(cl_tot, cl_in)]

    ks = [stack(kap_ref[:, ls] * g).astype(BF16) for ls, g in zip(lanes, g_ex)]
    rs = [stack(r_ref[:, ls] * g) for ls, g in zip(lanes, g_in)]
    vs = [stack(v_ref[:, ls]).astype(BF16) for ls in lanes]
    bk = [jnp.concatenate([dup(b_ref[:, ls] * g), dup(k_ref[:, ls] * g)], axis=0).astype(BF16)
          for ls, g in zip(lanes, g_inv)]
    gram = [_dot_nt(jnp.concatenate([ks[i], rs[i].astype(BF16)], axis=0), bk[i]) for i in P]
    bhs_t = [stack(b_ref[:, ls] * g).T.astype(BF16) for ls, g in zip(lanes, g_end)]
    khs_t = [stack(k_ref[:, ls] * g).T.astype(BF16) for ls, g in zip(lanes, g_end)]
    n_ = [jnp.where(strict, gm[:n2, n2:], 0.0).astype(BF16) for gm in gram]
    lb = [jnp.where(incl, gm[n2:, :n2], 0.0).astype(BF16) for gm in gram]
    lk = [jnp.where(incl, gm[n2:, n2:], 0.0).astype(BF16) for gm in gram]

    pw = [jnp.where(strict, -gm[:n2, :n2], 0.0) for gm in gram]
    tinv = [jnp.where(eye, 1.0, 0.0) + x for x in pw]
    span = 2
    while span < c_len:
        pw = [bdot(x, x) for x in pw]
        tinv = [t + bdot(t, x) for t, x in zip(tinv, pw)]
        span *= 2

    nv = [_dot(n_[i], vs[i]) for i in P]
    kq_u = [bdot(tinv[i], jnp.concatenate([ks[i], nv[i].astype(BF16)], axis=1)).astype(BF16) for i in P]
    low = [_dot(bhs_t[i], kq_u[i]) for i in P]
    lbq = [_dot(lb[i], kq_u[i]) for i in P]
    kv = [_dot(khs_t[i], vs[i]) for i in P]
    lkv = [_dot(lk[i], vs[i]) for i in P]
    for i in P:
        a_ = jnp.where(eye, jnp.exp(cl_tot[i]), 0.0) - low[i][:, :PAIR]
        g_ = kv[i] - low[i][:, PAIR:]
        rq = rs[i] - lbq[i][:, :PAIR]
        yi = lkv[i] - lbq[i][:, PAIR:]
        prop = _dot_x3(jnp.concatenate([rq, a_], axis=0), s_ref[i])
        ys = prop[:n2] + yi
        y_ref[:, lanes[i]] = ys[:c_len] + ys[c_len:]
        s_ref[i] = prop[n2:] + g_


def _side_cast_plan(a, out_shape, steps):
    rows, cols = a.shape
    out_rows, out_cols = out_shape
    if cols % LANES or out_cols % LANES:
        return None
    rb = 2 * SUBLANES
    while rb <= out_rows and (out_rows % rb or rows % rb or out_rows // rb > steps):
        rb *= 2
    if rb > out_rows:
        return None
    return rb, rows // rb, out_rows // rb


def _rwkv_scan(r, kap, v, lw, b, kd, n_ctx, cast=()):
    n, w = r.shape
    c_len = RWKV_CHUNK
    n_chunks = n // c_len
    n_ctx_chunks = n_ctx // c_len
    pairs = _pick(w // PAIR, (16, 4, 2, 1))
    bw = pairs * PAIR
    steps = 2 * (w // bw) * n_chunks
    plans = [_side_cast_plan(a, shape, steps) for a, shape in cast]
    if any(p is None for p in plans):
        y, _ = _rwkv_scan(r, kap, v, lw, b, kd, n_ctx)
        return y, [jnp.pad(a, [(0, t - s) for s, t in zip(a.shape, shape)]).astype(BF16) for a, shape in cast]

    def tok(d, p, c):
        return (_chunk_order(d, c, n_ctx_chunks, n_chunks), p)

    def tok2(d, p, c):
        return (d, _chunk_order(d, c, n_ctx_chunks, n_chunks), p)

    def step_of(d, p, c):
        return (d * (w // bw) + p) * n_chunks + c

    def src_spec(a, plan):
        rb, n_src, _ = plan
        return pl.BlockSpec((rb, a.shape[1]), lambda d, p, c: (jnp.minimum(step_of(d, p, c), n_src - 1), 0))

    def dst_spec(shape, plan):
        rb, _, n_dst = plan
        return pl.BlockSpec((rb, shape[1]), lambda d, p, c: (jnp.minimum(step_of(d, p, c), n_dst - 1), 0))

    one = pl.BlockSpec((c_len, bw), tok)
    two = pl.BlockSpec((None, c_len, bw), tok2)
    outs = pl.pallas_call(
        functools.partial(_rwkv_scan_kernel, pairs=pairs, casts=tuple((p[1], p[2]) for p in plans)),
        grid=(2, w // bw, n_chunks),
        in_specs=[one, one, one, two, two, two] + [src_spec(a, p) for (a, _), p in zip(cast, plans)],
        out_specs=[two] + [dst_spec(shape, p) for (_, shape), p in zip(cast, plans)],
        out_shape=[jax.ShapeDtypeStruct((2, n, w), F32)] + [jax.ShapeDtypeStruct(shape, BF16) for _, shape in cast],
        scratch_shapes=[pltpu.VMEM((pairs, PAIR, PAIR), F32)],
        compiler_params=_cparams("arbitrary", "arbitrary", "arbitrary"),
        name="rwkv_scan",
    )(r, kap, v, lw, b, kd, *[a for a, _ in cast])
    return outs[0], list(outs[1:])


def _rope_tables(n_ctx, n_lat, half):
    nq = half // 2
    pos = jnp.arange(n_lat, dtype=jnp.int32)
    row = (pos // GRID_W).astype(F32)
    col = (pos % GRID_W).astype(F32)
    inv = ROPE_THETA ** (-jnp.arange(nq, dtype=F32) / nq)

    def part(p):
        ang = p[:, None] * inv[None, :]
        c, s = jnp.cos(ang), jnp.sin(ang)
        return jnp.concatenate([c, c], -1), jnp.concatenate([-s, s], -1)

    cr, sr = part(row)
    cc, sc = part(col)
    cos = jnp.concatenate([cr, cc], -1)
    sin = jnp.concatenate([sr, sc], -1)
    cos = jnp.concatenate([jnp.ones((n_ctx, 2 * half), F32), cos], 0)
    sin = jnp.concatenate([jnp.zeros((n_ctx, 2 * half), F32), sin], 0)
    return cos, sin


def _rope_tile(x, cos, sin, quarter):
    if 2 * quarter == LANES:
        partner = pltpu.roll(x, quarter, 1)
    else:
        lane = lax.broadcasted_iota(jnp.int32, x.shape, 1)
        lower = (lane % (2 * quarter)) < quarter
        partner = jnp.where(lower, pltpu.roll(x, LANES - quarter, 1), pltpu.roll(x, quarter, 1))
    return x * cos + partner * sin


def _retention_kernel(q_ref, k_ref, v_ref, cos_ref, sin_ref, lg_ref, y_ref, s_ref, *, heads):
    c_len = q_ref.shape[0]
    d = pl.program_id(0)
    rev = d == 1

    @pl.when(pl.program_id(1) == 0)
    def _():
        s_ref[...] = jnp.zeros_like(s_ref)

    cos = cos_ref[...]
    sin = sin_ref[...]

    def rope(x):
        tiles = [_rope_tile(x[:, j * LANES:(j + 1) * LANES], cos[:, j * LANES:(j + 1) * LANES],
                            sin[:, j * LANES:(j + 1) * LANES], LANES // 2) for j in range(B_HEAD // LANES)]
        return jnp.concatenate(tiles, axis=-1)

    n_i = lax.broadcasted_iota(jnp.int32, (c_len, c_len), 0)
    m_i = lax.broadcasted_iota(jnp.int32, (c_len, c_len), 1)
    dist = (n_i - m_i) * jnp.where(rev, -1, 1)
    seen = dist >= jnp.where(rev, 1, 0)
    dist_f = jnp.maximum(dist, 0).astype(F32)
    idx = lax.broadcasted_iota(jnp.int32, (c_len, B_HEAD), 0)
    order = jnp.where(rev, c_len - 1 - idx, idx).astype(F32)

    H = range(heads)
    cols = [slice(h * B_HEAD, (h + 1) * B_HEAD) for h in H]
    lg = [jnp.concatenate([lg_ref[h, 0:1, :]] * (B_HEAD // LANES), axis=-1) for h in H]
    lg_c = [jnp.concatenate([lg_ref[h, 0:1, :]] * (c_len // LANES), axis=-1) for h in H]
    q = [rope(q_ref[:, cs]) for cs in cols]
    k = [rope(k_ref[:, cs]) * (B_HEAD ** -0.5) for cs in cols]
    v = [v_ref[:, cs].astype(BF16) for cs in cols]
    inner = [_dot_nt(q[h].astype(BF16), k[h].astype(BF16)) for h in H]
    cross = [_dot((q[h] * jnp.exp(lg[h] * (order + 1.0))).astype(BF16), s_ref[h].astype(BF16)) for h in H]
    kt = [(k[h] * jnp.exp(lg[h] * (c_len - 1.0 - order))).T.astype(BF16) for h in H]
    upd = [_dot(kt[h], v[h]) for h in H]
    for h in H:
        decay = jnp.where(seen, jnp.exp(lg_c[h] * dist_f), 0.0)
        y_ref[:, cols[h]] = _dot((inner[h] * decay).astype(BF16), v[h]) + cross[h]
        s_ref[h] = s_ref[h] * jnp.exp(lg[h] * float(c_len)) + upd[h]


def _retention(pb, n_ctx, cos, sin):
    n = pb.shape[0]
    width = pb.shape[1] // 4
    heads = width // B_HEAD
    c_len = RET_CHUNK
    n_chunks = n // c_len
    n_ctx_chunks = n_ctx // c_len
    log_g = jnp.log1p(-jnp.exp2(-5.0 - jnp.arange(heads, dtype=F32)))
    lg = jnp.stack([log_g, log_g[::-1]])
    lg = jnp.broadcast_to(lg[:, :, None, None], (2, heads, SUBLANES, LANES))

    def part(which):
        return pl.BlockSpec((c_len, width), lambda d, c: (_chunk_order(d, c, n_ctx_chunks, n_chunks), which))

    tab = pl.BlockSpec((c_len, B_HEAD), lambda d, c: (_chunk_order(d, c, n_ctx_chunks, n_chunks), 0))
    return pl.pallas_call(
        functools.partial(_retention_kernel, heads=heads),
        grid=(2, n_chunks),
        in_specs=[part(0), part(1), part(2), tab, tab,
                  pl.BlockSpec((None, heads, SUBLANES, LANES), lambda d, c: (d, 0, 0, 0))],
        out_specs=pl.BlockSpec((None, c_len, width),
                               lambda d, c: (d, _chunk_order(d, c, n_ctx_chunks, n_chunks), 0)),
        out_shape=jax.ShapeDtypeStruct((2, n, width), F32),
        scratch_shapes=[pltpu.VMEM((heads, B_HEAD, B_HEAD), F32)],
        compiler_params=_cparams("parallel", "arbitrary"),
        name="retention",
    )(pb, pb, pb, cos, sin, lg)


def _mix_finish_kernel(ya_ref, bonus_ref, g_ref, yb_ref, gate_ref, lg_ref, lb_ref, gg_ref, gb_ref, o_ref):
    wa = bonus_ref.shape[1]
    ya = ya_ref[0] + ya_ref[1]
    seg = _head_seg_matrix(A_HEAD)
    inv = 1.0 / A_HEAD
    for j in range(wa // LANES):
        sl = slice(j * LANES, (j + 1) * LANES)
        t = ya[:, sl]
        mu = _segsum_tile(t, seg) * inv
        tc = t - mu
        var = _segsum_tile(tc * tc, seg) * inv
        o = tc * lax.rsqrt(var + A_GN_EPS) * lg_ref[:, sl] + lb_ref[:, sl] + bonus_ref[:, sl]
        o_ref[:, sl] = (o * g_ref[:, sl]).astype(o_ref.dtype)
    yb = yb_ref[0] + yb_ref[1]
    wb = yb.shape[1]
    for j in range(wb // B_HEAD):
        sl = slice(j * B_HEAD, (j + 1) * B_HEAD)
        t = yb[:, sl]
        mu = jnp.mean(t, axis=-1, keepdims=True)
        tc = t - mu
        var = jnp.mean(tc * tc, axis=-1, keepdims=True)
        o = tc * lax.rsqrt(var + B_GN_EPS) * gg_ref[:, sl] + gb_ref[:, sl]
        o_ref[:, wa + j * B_HEAD:wa + (j + 1) * B_HEAD] = (o * _silu(gate_ref[:, sl])).astype(o_ref.dtype)


def _mix_finish(ya, bonus, g, yb, pb, lnx_g, lnx_b, gn_g, gn_b):
    n, wa = bonus.shape
    wb = yb.shape[2]
    tb = _pick(n, (128, 64, 32, 16, 8))
    ra = pl.BlockSpec((tb, wa), lambda i: (i, 0))
    rb = pl.BlockSpec((tb, wb), lambda i: (i, 3))
    va = pl.BlockSpec((1, wa), lambda i: (0, 0))
    vb = pl.BlockSpec((1, wb), lambda i: (0, 0))
    return pl.pallas_call(
        _mix_finish_kernel,
        grid=(n // tb,),
        in_specs=[pl.BlockSpec((2, tb, wa), lambda i: (0, i, 0)), ra, ra,
                  pl.BlockSpec((2, tb, wb), lambda i: (0, i, 0)), rb, va, va, vb, vb],
        out_specs=pl.BlockSpec((tb, wa + wb), lambda i: (i, 0)),
        out_shape=jax.ShapeDtypeStruct((n, wa + wb), BF16),
        compiler_params=_cparams("parallel"),
        name="mix_finish",
    )(ya, bonus, g, yb, pb, lnx_g.reshape(1, wa), lnx_b.reshape(1, wa), gn_g.reshape(1, wb), gn_b.reshape(1, wb))


def _qkv_prep_kernel(x_ref, cos_ref, sin_ref, qg_ref, kg_ref, q_ref, k_ref, v_ref, *, q_heads, kv_heads):
    cos = cos_ref[...]
    sin = sin_ref[...]
    quarter = C_HEAD // 4
    lane = lax.broadcasted_iota(jnp.int32, cos.shape, 1)
    lower = (lane % (2 * quarter)) < quarter
    scale = C_HEAD ** -0.5 * LOG2_E
    n_heads = q_heads + kv_heads
    tiles = [x_ref[:, h * C_HEAD:(h + 1) * C_HEAD] for h in range(n_heads)]
    ms = [jnp.mean(t * t, axis=-1, keepdims=True) for t in tiles]
    gq = qg_ref[...] * scale
    normed = [t * lax.rsqrt(m + QK_EPS) * (gq if h < q_heads else kg_ref[...])
              for h, (t, m) in enumerate(zip(tiles, ms))]
    up = [pltpu.roll(t, LANES - quarter, 1) for t in normed]
    down = [pltpu.roll(t, quarter, 1) for t in normed]
    for h in range(n_heads):
        y = (normed[h] * cos + jnp.where(lower, up[h], down[h]) * sin).astype(BF16)
        if h < q_heads:
            q_ref[:, h * C_HEAD:(h + 1) * C_HEAD] = y
        else:
            k_ref[:, (h - q_heads) * C_HEAD:(h - q_heads + 1) * C_HEAD] = y
    v0 = n_heads * C_HEAD
    v_ref[...] = x_ref[:, v0:v0 + kv_heads * C_HEAD].astype(BF16)


def _qkv_prep(qkv, cos, sin, qn_g, kn_g, q_heads, kv_heads):
    n, c_in = qkv.shape
    tb = _pick(n, (128, 64, 32, 16, 8))
    kvw = kv_heads * C_HEAD
    return pl.pallas_call(
        functools.partial(_qkv_prep_kernel, q_heads=q_heads, kv_heads=kv_heads),
        grid=(n // tb,),
        in_specs=[pl.BlockSpec((tb, c_in), lambda i: (i, 0)),
                  pl.BlockSpec((tb, C_HEAD), lambda i: (i, 0)),
                  pl.BlockSpec((tb, C_HEAD), lambda i: (i, 0)),
                  pl.BlockSpec((1, C_HEAD), lambda i: (0, 0)),
                  pl.BlockSpec((1, C_HEAD), lambda i: (0, 0))],
        out_specs=[pl.BlockSpec((tb, q_heads * C_HEAD), lambda i: (i, 0)),
                   pl.BlockSpec((tb, kvw), lambda i: (i, 0)),
                   pl.BlockSpec((tb, kvw), lambda i: (i, 0))],
        out_shape=[jax.ShapeDtypeStruct((n, q_heads * C_HEAD), BF16),
                   jax.ShapeDtypeStruct((n, kvw), BF16),
                   jax.ShapeDtypeStruct((n, kvw), BF16)],
        compiler_params=_cparams("parallel"),
        name="qkv_prep",
    )(qkv, cos, sin, qn_g.reshape(1, C_HEAD), kn_g.reshape(1, C_HEAD))


def _flash_kernel(*refs, tq, tk, n_kv, n_cast):
    q_ref, k_ref, v_ref = refs[:3]
    cast_in = refs[3:3 + n_cast]
    o_ref = refs[3 + n_cast]
    cast_out = refs[4 + n_cast:4 + 2 * n_cast]
    q_scr, s_scr, p_scr, m_scr, l_scr, c_scr, acc_scr = refs[4 + 2 * n_cast:]
    for src, dst in zip(cast_in, cast_out):
        dst[...] = src[...].astype(BF16)
    nkv = n_kv // tk
    for h in range(C_GROUP):
        q_scr[h * tq:(h + 1) * tq, :] = q_ref[:, h * C_HEAD:(h + 1) * C_HEAD]
    m_scr[...] = jnp.full(m_scr.shape, -jnp.inf, F32)
    l_scr[...] = jnp.zeros(l_scr.shape, F32)
    acc_scr[...] = jnp.zeros(acc_scr.shape, F32)
    n_col = tk // LANES

    half = s_scr.shape[1] // 2
    halves = (slice(0, half), slice(half, 2 * half))

    def scores(j, slot):
        kj = k_ref[:, pl.ds(pl.multiple_of(j * tk, tk), tk)]
        for rs in halves:
            s_scr[slot, rs, :] = _dot(q_scr[rs, :], kj)

    def values(j, slot):
        vj = v_ref[pl.ds(pl.multiple_of(j * tk, tk), tk), :]
        return jnp.concatenate([_dot(p_scr[slot, rs, :], vj) for rs in halves], axis=0)

    def normalise(slot):
        for rs in halves:
            tile_max = s_scr[slot, rs, 0:LANES]
            for c in range(1, n_col):
                tile_max = jnp.maximum(tile_max, s_scr[slot, rs, c * LANES:(c + 1) * LANES])
            row_max = jnp.broadcast_to(jnp.max(tile_max, axis=-1, keepdims=True), tile_max.shape)
            m_old = m_scr[rs, :]
            m_new = jnp.maximum(m_old, row_max)
            corr = jnp.exp2(m_old - m_new)
            part = None
            for c in range(n_col):
                p = jnp.exp2(s_scr[slot, rs, c * LANES:(c + 1) * LANES] - m_new)
                part = p if part is None else part + p
                p_scr[slot, rs, c * LANES:(c + 1) * LANES] = p.astype(BF16)
            l_scr[rs, :] = corr * l_scr[rs, :] + part
            m_scr[rs, :] = m_new
            c_scr[rs, :] = corr

    scores(0, 0)
    if nkv > 1:
        scores(1, 1)
    normalise(0)

    def step(j, cur):
        scores(j + 1, 1 - cur)
        normalise(cur)
        acc_scr[...] = (acc_scr[...] + values(j - 1, 1 - cur)) * c_scr[...]

    def pair(t, carry):
        step(2 * t + 1, 1)
        step(2 * t + 2, 0)
        return carry

    n_mid = max(nkv - 2, 0)
    if n_mid >= 2:
        lax.fori_loop(0, n_mid // 2, pair, 0)
    if n_mid % 2:
        step(nkv - 2, (nkv - 2) % 2)
    if nkv > 1:
        last = (nkv - 1) % 2
        normalise(last)
        acc_scr[...] = (acc_scr[...] + values(nkv - 2, 1 - last)) * c_scr[...]
    acc = acc_scr[...] + values(nkv - 1, (nkv - 1) % 2)
    out = acc / jnp.sum(l_scr[...], axis=-1, keepdims=True)
    for h in range(C_GROUP):
        o_ref[:, h * C_HEAD:(h + 1) * C_HEAD] = out[h * tq:(h + 1) * tq].astype(o_ref.dtype)


def _flash(q, k, v, n_ctx, cast=()):
    n = q.shape[0]
    kv_heads = k.shape[1] // C_HEAD
    n_lat = n - n_ctx
    tq = _pick(n_ctx, (256, 128, 64, 32, 16, 8))
    tk = _pick(n, (1280, 1024, 640, 512, 256, 128, 64, 32, 16, 8))
    gw = C_GROUP * C_HEAD
    q0 = n_ctx // tq
    rows = C_GROUP * tq
    n_qb = n_lat // tq
    steps = kv_heads * n_qb
    slabs = [a.shape[0] * a.shape[1] // steps for a in cast]
    if not all(a.ndim == 3 and a.shape[0] * a.shape[1] == sl * steps and a.shape[1] % sl == 0
               and sl % (2 * SUBLANES) == 0 for a, sl in zip(cast, slabs)):
        return _flash(q, k, v, n_ctx)[0], [a.astype(BF16) for a in cast]

    def slab_spec(a, sl):
        per = a.shape[1] // sl
        return pl.BlockSpec((None, sl, a.shape[2]), lambda g, i: ((g * n_qb + i) // per, (g * n_qb + i) % per, 0))

    cast_specs = [slab_spec(a, sl) for a, sl in zip(cast, slabs)]
    outs = pl.pallas_call(
        functools.partial(_flash_kernel, tq=tq, tk=tk, n_kv=n, n_cast=len(cast)),
        grid=(kv_heads, n_qb),
        in_specs=[pl.BlockSpec((tq, gw), lambda g, i: (q0 + i, g)),
                  pl.BlockSpec((C_HEAD, n), lambda g, i: (g, 0)),
                  pl.BlockSpec((n, C_HEAD), lambda g, i: (0, g))] + cast_specs,
        out_specs=[pl.BlockSpec((tq, gw), lambda g, i: (i, g))] + cast_specs,
        out_shape=[jax.ShapeDtypeStruct((n_lat, kv_heads * gw), BF16)]
        + [jax.ShapeDtypeStruct(a.shape, BF16) for a in cast],
        scratch_shapes=[pltpu.VMEM((rows, C_HEAD), BF16),
                        pltpu.VMEM((2, rows, tk), F32),
                        pltpu.VMEM((2, rows, tk), BF16),
                        pltpu.VMEM((rows, LANES), F32),
                        pltpu.VMEM((rows, LANES), F32),
                        pltpu.VMEM((rows, LANES), F32),
                        pltpu.VMEM((rows, C_HEAD), F32)],
        compiler_params=_cparams("parallel", "parallel"),
        name="flash_gqa",
    )(q, k.T, v, *cast)
    return outs[0], list(outs[1:])


def _pad_cols(w, n):
    return jnp.pad(w, ((0, 0), (0, n - w.shape[1])))


def kernel(x, c, ctx, c_ctx, l0_mod_w, l0_mod_b, l0_w_in, l0_a_mu, l0_a_w0, l0_a_w2, l0_a_a0, l0_a_a2, l0_a_g2, l0_a_k_k, l0_a_k_a, l0_a_r_k, l0_a_lnx_g, l0_a_lnx_b, l0_b_gn_g, l0_b_gn_b, l0_w_out, l0_ln1_g, l0_ln1_b, l0_ffn_w1, l0_ffn_w3, l0_ffn_w2, l0_ln2_g, l0_ln2_b, l1_mod_w, l1_mod_b, l1_w_qkv, l1_q_norm_g, l1_k_norm_g, l1_w_out, l1_ln1_g, l1_ln1_b, l1_router, l1_moe_w1, l1_moe_w3, l1_moe_w2, l1_ln2_g, l1_ln2_b):
    batch, n_lat, d = x.shape
    n_ctx = ctx.shape[1]
    assert batch == 1 and c.shape[0] == 1
    a_width = l0_a_g2.shape[1]
    a_in = l0_a_mu.shape[0]
    b_width = l0_b_gn_g.shape[0]

    c_rows = jnp.zeros((SUBLANES, d), F32).at[0].set(c[0]).at[1].set(c_ctx)
    mod0 = _mod_vectors(c_rows, l0_mod_w, l0_mod_b)
    mod1 = _mod_vectors(c_rows, l1_mod_w, l1_mod_b)

    h = jnp.concatenate([ctx[0], x[0]], axis=0)

    _, u = _ln_mod(h, None, mod0, 0, n_ctx)
    a_pad = _round_up(a_in, 1024) if a_in > 1024 else a_in
    w_in_a = _pad_cols(l0_w_in[:, :a_in], a_pad).astype(BF16)
    w_in_b = l0_w_in[:, a_in:].astype(BF16)
    pa = _mm_plain(u, w_in_a, F32)
    pb = _mm_plain(u, w_in_b, F32)
    r, v, kap, g, bonus, lw, b, kd = _rwkv_prepare(pa, n_ctx, a_in, l0_a_mu, l0_a_w0, l0_a_w2, l0_a_a0, l0_a_a2,
                                                   l0_a_g2, l0_a_k_k, l0_a_k_a, l0_a_r_k)
    d_ff = l0_ffn_w1.shape[1]
    ff_pad = _round_up(d_ff, 1024) if d_ff > 1024 else d_ff
    ya, (ffn_w1, ffn_w3, ffn_w2) = _rwkv_scan(
        r, kap, v, lw, b, kd, n_ctx,
        cast=((l0_ffn_w1, (d, ff_pad)), (l0_ffn_w3, (d, ff_pad)), (l0_ffn_w2, (ff_pad, d))))
    cos_b, sin_b = _rope_tables(n_ctx, n_lat, B_HEAD // 2)
    yb = _retention(pb, n_ctx, cos_b, sin_b)
    mix = _mix_finish(ya, bonus, g, yb, pb, l0_a_lnx_g, l0_a_lnx_b, l0_b_gn_g, l0_b_gn_b)
    res = _mm_res(mix, l0_w_out.astype(BF16), h, mod0, 2, n_ctx)
    h, u = _ln_mod(res, (l0_ln1_g, l0_ln1_b), mod0, 3, n_ctx)
    hid = _mm_swiglu(u, ffn_w1, ffn_w3)
    res = _mm_res(hid, ffn_w2, h, mod0, 5, n_ctx)
    h_lat, u = _ln_mod(res, (l0_ln2_g, l0_ln2_b), mod1, 0, n_ctx, h_from=n_ctx)

    q_heads = l1_w_out.shape[0] // C_HEAD
    kv_heads = q_heads // C_GROUP
    qkv = _mm_plain(u, l1_w_qkv.astype(BF16), F32)
    cos_c, sin_c = _rope_tables(n_ctx, n_lat, C_HEAD // 2)
    q, k, vv = _qkv_prep(qkv, cos_c, sin_c, l1_q_norm_g, l1_k_norm_g, q_heads, kv_heads)
    att, (moe_w1, moe_w3, moe_w2) = _flash(q, k, vv, n_ctx, cast=(l1_moe_w1, l1_moe_w3, l1_moe_w2))
    res = _mm_res(att, l1_w_out.astype(BF16), h_lat, mod1, 2, 0)
    h_lat, u, comb = _ln_mod(res, (l1_ln1_g, l1_ln1_b), mod1, 3, 0, router=l1_router)
    n_exp = l1_router.shape[1]
    cexp = jnp.broadcast_to(comb[:, :n_exp].T[:, :, None], (n_exp, n_lat, LANES))
    hid = _mm_swiglu(u, moe_w1, moe_w3, cexp)
    w2 = moe_w2.reshape(n_exp * moe_w2.shape[1], d)
    res = _mm_res(hid, w2, h_lat, mod1, 5, 0)
    out, _ = _ln_mod(res, (l1_ln2_g, l1_ln2_b), None, 0, 0)
    return out[None]
```

```python
import functools

import jax
import jax.numpy as jnp
from jax import lax
from jax.experimental import pallas as pl
from jax.experimental.pallas import tpu as pltpu

F32 = jnp.float32
BF16 = jnp.bfloat16
HIGHEST = lax.Precision.HIGHEST

GRID_W = 64
DEPTH = 2
ALPHA = (2 * DEPTH) ** 0.25
LN_EPS = 1e-6
ROPE_THETA = 10000.0
A_HEAD = 64
A_DECAY_LORA = 96
A_AAA_LORA = 96
A_GATE_LORA = 256
A_GN_EPS = 64e-5
B_HEAD = 256
RET_CHUNK = 128
B_GN_EPS = 1e-5
C_HEAD = 128
C_GROUP = 4
QK_EPS = 1e-6
LOG2_E = 1.4426950408889634

LANES = 128
SUBLANES = 8
VMEM_LIMIT_BYTES = 60 * 1024 * 1024

RWKV_CHUNK = 64
PAIR = 2 * A_HEAD


def _cparams(*sem):
    return pltpu.CompilerParams(dimension_semantics=sem, vmem_limit_bytes=VMEM_LIMIT_BYTES)


def _pick(n, candidates):
    for c in candidates:
        if c <= n and n % c == 0:
            return c
    return n


def _round_up(n, m):
    return (n + m - 1) // m * m


def _silu(x):
    return x * (1.0 / (1.0 + jnp.exp(-x)))


def _sigmoid(x):
    return 1.0 / (1.0 + jnp.exp(-x))


def _dot(a, b):
    return jnp.dot(a, b, preferred_element_type=F32)


def _dot_nt(a, b, precision=None):
    return lax.dot_general(a, b, (((1,), (1,)), ((), ())), preferred_element_type=F32, precision=precision)


def _row_is_ctx(shape, row0, n_ctx):
    rows = row0 + lax.broadcasted_iota(jnp.int32, shape, 0)
    return rows < n_ctx


def _select_stream(mod_ref, shape, row0, n_ctx):
    lat = mod_ref[0:1, :]
    if n_ctx == 0:
        return jnp.broadcast_to(lat, shape)
    return jnp.where(_row_is_ctx(shape, row0, n_ctx), mod_ref[1:2, :], lat)


def _modvec_kernel(c_ref, w_ref, b_ref, o_ref):
    s = _silu(c_ref[...])
    o_ref[...] = jnp.dot(s, w_ref[...], preferred_element_type=F32, precision=HIGHEST) + b_ref[...]


def _mod_vectors(c_rows, w, b):
    d, n = w.shape
    tn = _pick(n, (512, 256, 128))
    return pl.pallas_call(
        _modvec_kernel,
        grid=(n // tn,),
        in_specs=[pl.BlockSpec((SUBLANES, d), lambda j: (0, 0)),
                  pl.BlockSpec((d, tn), lambda j: (0, j)),
                  pl.BlockSpec((1, tn), lambda j: (0, j))],
        out_specs=pl.BlockSpec((SUBLANES, tn), lambda j: (0, j)),
        out_shape=jax.ShapeDtypeStruct((SUBLANES, n), F32),
        compiler_params=_cparams("parallel"),
        name="mod_vectors",
    )(c_rows, w, b.reshape(1, n))


def _top2_combine(u, w_ref, n_exp):
    logits = jnp.dot(u, w_ref[...], preferred_element_type=F32, precision=HIGHEST)
    lane = lax.broadcasted_iota(jnp.int32, logits.shape, 1)
    neg = jnp.float32(-jnp.inf)
    l1 = jnp.where(lane < n_exp, logits, neg)
    m1 = jnp.max(l1, axis=-1, keepdims=True)
    i1 = jnp.min(jnp.where(l1 == m1, lane, LANES), axis=-1, keepdims=True)
    l2 = jnp.where(lane == i1, neg, l1)
    m2 = jnp.max(l2, axis=-1, keepdims=True)
    i2 = jnp.min(jnp.where(l2 == m2, lane, LANES), axis=-1, keepdims=True)
    e2 = jnp.exp(m2 - m1)
    return jnp.where(lane == i1, 1.0 / (1.0 + e2), 0.0) + jnp.where(lane == i2, e2 / (1.0 + e2), 0.0)


def _ln_mod_kernel(*refs, tb, n_ctx, do_ln, do_mod, n_exp):
    refs = list(refs)
    r_ref = refs.pop(0)
    x = r_ref[...]
    if do_ln:
        g_ref, b_ref = refs.pop(0), refs.pop(0)
        mu = jnp.mean(x, axis=-1, keepdims=True)
        xc = x - mu
        var = jnp.mean(xc * xc, axis=-1, keepdims=True)
        x = xc * lax.rsqrt(var + LN_EPS) * g_ref[...] + b_ref[...]
    if do_mod:
        sh_ref, sc_ref = refs.pop(0), refs.pop(0)
    if n_exp:
        w_ref = refs.pop(0)
    if do_ln:
        refs.pop(0)[...] = x
    if do_mod:
        row0 = pl.program_id(0) * tb
        sh = _select_stream(sh_ref, x.shape, row0, n_ctx)
        sc = _select_stream(sc_ref, x.shape, row0, n_ctx)
        u = x * (1.0 + sc) + sh
        refs.pop(0)[...] = u.astype(BF16)
        if n_exp:
            refs.pop(0)[...] = _top2_combine(u, w_ref, n_exp)


def _ln_mod(r, ln, mod, k_shift, n_ctx, router=None, h_from=0):
    n, d = r.shape
    do_ln, do_mod = ln is not None, mod is not None
    n_exp = 0 if router is None else router.shape[1]
    tb = _pick(h_from if h_from else n, (256, 128, 64, 32, 16, 8))
    row = pl.BlockSpec((tb, d), lambda i: (i, 0))
    vec = pl.BlockSpec((1, d), lambda i: (0, 0))
    args, in_specs, out_specs, out_shape = [r], [row], [], []
    if do_ln:
        args += [ln[0].reshape(1, d), ln[1].reshape(1, d)]
        in_specs += [vec, vec]
        skip = h_from // tb
        out_specs.append(pl.BlockSpec((tb, d), lambda i: (jnp.maximum(i - skip, 0), 0)))
        out_shape.append(jax.ShapeDtypeStruct((n - h_from, d), F32))
    if do_mod:
        args += [mod, mod]
        in_specs += [pl.BlockSpec((SUBLANES, d), lambda i: (0, k_shift)),
                     pl.BlockSpec((SUBLANES, d), lambda i: (0, k_shift + 1))]
        out_specs.append(row)
        out_shape.append(jax.ShapeDtypeStruct((n, d), BF16))
    if n_exp:
        args.append(jnp.zeros((d, LANES), F32).at[:, :n_exp].set(router))
        in_specs.append(pl.BlockSpec((d, LANES), lambda i: (0, 0)))
        out_specs.append(pl.BlockSpec((tb, LANES), lambda i: (i, 0)))
        out_shape.append(jax.ShapeDtypeStruct((n, LANES), F32))
    outs = pl.pallas_call(
        functools.partial(_ln_mod_kernel, tb=tb, n_ctx=n_ctx, do_ln=do_ln, do_mod=do_mod, n_exp=n_exp),
        grid=(n // tb,),
        in_specs=in_specs,
        out_specs=out_specs,
        out_shape=out_shape,
        compiler_params=_cparams("arbitrary"),
        name="ln_mod",
    )(*args)
    outs = list(outs)
    h = outs.pop(0) if do_ln else r
    u = outs.pop(0) if do_mod else None
    return (h, u, outs.pop(0)) if n_exp else (h, u)


def _mm_kernel(*refs, kind, tm, n_ctx, nk):
    n_w = 2 if kind == "swiglu" else 1
    n_acc = n_w if nk > 1 else 0
    accs = refs[len(refs) - n_acc:]
    refs = refs[:len(refs) - n_acc]
    x_ref, w_refs, rest = refs[0], refs[1:1 + n_w], refs[1 + n_w:]
    o_ref = rest[-1]
    k = pl.program_id(2)

    def products():
        x = x_ref[...]
        return [_dot(x, w[...]) for w in w_refs]

    def finish(tot):
        if kind == "swiglu":
            c_ref = rest[0] if len(rest) == 2 else None
            for j in range(o_ref.shape[1] // LANES):
                sl = slice(j * LANES, (j + 1) * LANES)
                y = _silu(tot[0][:, sl]) * tot[1][:, sl]
                if c_ref is not None:
                    y = y * c_ref[...]
                o_ref[:, sl] = y.astype(o_ref.dtype)
        elif kind == "res":
            h_ref, gt_ref = rest[0], rest[1]
            gt = _select_stream(gt_ref, tot[0].shape, pl.program_id(0) * tm, n_ctx)
            o_ref[...] = ALPHA * h_ref[...] + gt * tot[0]
        else:
            o_ref[...] = tot[0].astype(o_ref.dtype)

    if nk == 1:
        finish(products())
        return

    @pl.when(k == 0)
    def _():
        for acc, y in zip(accs, products()):
            acc[...] = y

    @pl.when((k > 0) & (k < nk - 1))
    def _():
        for acc, y in zip(accs, products()):
            acc[...] += y

    @pl.when(k == nk - 1)
    def _():
        finish([acc[...] + y for acc, y in zip(accs, products())])


def _mm_tiles(m, n, k):
    tm = _pick(m, (1280, 1024, 512, 256, 128, 64, 32, 16, 8))
    if k > 2 * 4096:
        return tm, _pick(n, (1024, 512, 256, 128)), _pick(k, (2048, 1024, 512, 256, 128))
    tn = _pick(n, (512, 256, 128))
    tk = _pick(k, (4096, 2816, 2048, 1024, 512, 256, 128))
    return tm, tn, tk


def _mm_scratch(tm, tn, nk, n_w):
    return [pltpu.VMEM((tm, tn), F32)] * n_w if nk > 1 else []


def _mm_plain(x, w, out_dtype):
    m, kd = x.shape
    n = w.shape[1]
    tm, tn, tk = _mm_tiles(m, n, kd)
    nk = kd // tk
    return pl.pallas_call(
        functools.partial(_mm_kernel, kind="plain", tm=tm, n_ctx=0, nk=nk),
        grid=(m // tm, n // tn, nk),
        in_specs=[pl.BlockSpec((tm, tk), lambda i, j, k: (i, k)),
                  pl.BlockSpec((tk, tn), lambda i, j, k: (k, j))],
        out_specs=pl.BlockSpec((tm, tn), lambda i, j, k: (i, j)),
        out_shape=jax.ShapeDtypeStruct((m, n), out_dtype),
        scratch_shapes=_mm_scratch(tm, tn, nk, 1),
        compiler_params=_cparams("parallel", "parallel", "arbitrary"),
        name="mm_plain",
    )(x, w)


def _mm_res(x, w, h, mod, k_gate, n_ctx):
    m, kd = x.shape
    n = w.shape[1]
    tm, tn, tk = _mm_tiles(m, n, kd)
    nk = kd // tk
    gate_block0 = k_gate * (n // tn)
    return pl.pallas_call(
        functools.partial(_mm_kernel, kind="res", tm=tm, n_ctx=n_ctx, nk=nk),
        grid=(m // tm, n // tn, nk),
        in_specs=[pl.BlockSpec((tm, tk), lambda i, j, k: (i, k)),
                  pl.BlockSpec((tk, tn), lambda i, j, k: (k, j)),
                  pl.BlockSpec((tm, tn), lambda i, j, k: (i, j)),
                  pl.BlockSpec((SUBLANES, tn), lambda i, j, k: (0, gate_block0 + j))],
        out_specs=pl.BlockSpec((tm, tn), lambda i, j, k: (i, j)),
        out_shape=jax.ShapeDtypeStruct((m, n), F32),
        scratch_shapes=_mm_scratch(tm, tn, nk, 1),
        compiler_params=_cparams("parallel", "parallel", "arbitrary"),
        name="mm_res",
    )(x, w, h, mod)


def _mm_swiglu(x, w1, w3, cexp=None):
    m, kd = x.shape
    stacked = w1.ndim == 3
    f = w1.shape[-1]
    n_exp = w1.shape[0] if stacked else 1
    tm, tn, tk = _mm_tiles(m, f, kd)
    nk = kd // tk
    nf = f // tn
    if stacked:
        w_spec = pl.BlockSpec((None, tk, tn), lambda i, j, k: (j // nf, k, j % nf))
    else:
        w_spec = pl.BlockSpec((tk, tn), lambda i, j, k: (k, j))
    in_specs = [pl.BlockSpec((tm, tk), lambda i, j, k: (i, k)), w_spec, w_spec]
    args = [x, w1, w3]
    if cexp is not None:
        in_specs.append(pl.BlockSpec((None, tm, LANES), lambda i, j, k: (j // nf, i, 0)))
        args.append(cexp)

    return pl.pallas_call(
        functools.partial(_mm_kernel, kind="swiglu", tm=tm, n_ctx=0, nk=nk),
        grid=(m // tm, n_exp * nf, nk),
        in_specs=in_specs,
        out_specs=pl.BlockSpec((tm, tn), lambda i, j, k: (i, j)),
        out_shape=jax.ShapeDtypeStruct((m, n_exp * f), BF16),
        scratch_shapes=_mm_scratch(tm, tn, nk, 2),
        compiler_params=_cparams("parallel", "parallel", "arbitrary"),
        name="mm_swiglu",
    )(*args)


def _split3(x):
    hi = x.astype(BF16)
    r1 = x - hi.astype(F32)
    mid = r1.astype(BF16)
    lo = (r1 - mid.astype(F32)).astype(BF16)
    return hi, mid, lo


def _head_seg_matrix(width):
    r = lax.broadcasted_iota(jnp.int32, (LANES, LANES), 0) // width
    c = lax.broadcasted_iota(jnp.int32, (LANES, LANES), 1) // width
    return (r == c).astype(BF16)


def _segsum_tile(x, seg):
    hi, mid, lo = _split3(x)
    return _dot(hi, seg) + _dot(mid, seg) + _dot(lo, seg)


def _rwkv_prep_kernel(p_ref, prev_ref, next_ref, mu_ref, w0_ref, w2_ref, a0_ref, a2_ref, g2_ref,
                      kk_ref, ka_ref, rk_ref,
                      r_out, v_out, kap_out, g_out, bonus_out, lw_out, b_out, kd_out,
                      *, tb, n_ctx, n_rows, width, win_w, win_a, o_gl):
    x = p_ref[...]
    row = pl.program_id(0) * tb + lax.broadcasted_iota(jnp.int32, (tb, 1), 0)
    local = lax.broadcasted_iota(jnp.int32, (tb, 1), 0)
    prev = jnp.where(local == 0, prev_ref[SUBLANES - 1:SUBLANES, :], pltpu.roll(x, 1, 0))
    nxt = jnp.where(local == tb - 1, next_ref[0:1, :], pltpu.roll(x, tb - 1, 0))
    prev = jnp.where((row == 0) | (row == n_ctx), 0.0, prev)
    nxt = jnp.where((row == n_ctx - 1) | (row == n_rows - 1), 0.0, nxt)
    x = x + mu_ref[...] * (0.5 * (prev + nxt) - x)

    w = width
    r = x[:, 0:w]
    k = x[:, w:2 * w]
    v = x[:, 2 * w:3 * w]
    gl = x[:, o_gl:o_gl + A_GATE_LORA]
    wl = x[:, win_w[0]:win_w[0] + win_w[1]]
    al = x[:, win_a[0]:win_a[0] + win_a[1]]

    g_out[...] = _dot(_sigmoid(gl).astype(BF16), g2_ref[...])
    lw_lora = _dot(jnp.tanh(wl).astype(BF16), w2_ref[...])
    a_lora = _dot(al.astype(BF16), a2_ref[...])

    kk = k * kk_ref[...]
    seg = _head_seg_matrix(A_HEAD)
    rk = rk_ref[...]
    ka = ka_ref[...]
    kd_sum = jnp.zeros_like(k)
    kaps = []
    for j in range(w // LANES):
        sl = slice(j * LANES, (j + 1) * LANES)
        t = kk[:, sl]
        n2 = _segsum_tile(t * t, seg)
        kaps.append(t / jnp.maximum(jnp.sqrt(n2), 1e-12))
    kap = jnp.concatenate(kaps, axis=-1)
    for d in range(2):
        z = w0_ref[d:d + 1, :] + lw_lora[:, d * w:(d + 1) * w]
        softplus = jnp.maximum(-z, 0.0) + jnp.log(1.0 + jnp.exp(-jnp.abs(z)))
        wlog = -softplus - 0.5
        lw_out[d] = -jnp.exp(wlog)
        a = _sigmoid(a0_ref[d:d + 1, :] + a_lora[:, d * w:(d + 1) * w])
        b_out[d] = kap * a
        kd = k * (1.0 + (a - 1.0) * ka)
        kd_out[d] = kd
        kd_sum = kd_sum + kd
    bon = r * kd_sum * rk
    bons = []
    for j in range(w // LANES):
        sl = slice(j * LANES, (j + 1) * LANES)
        bons.append(_segsum_tile(bon[:, sl], seg))
    bonus_out[...] = jnp.concatenate(bons, axis=-1) * v
    r_out[...] = r
    v_out[...] = v
    kap_out[...] = kap


def _aligned_window(offset, width, limit):
    start = offset // LANES * LANES
    size = min(_round_up(offset + width - start, LANES), limit - start)
    return start, size


def _window_weight(blocks, offset, window, width):
    start, size = window
    lora = blocks.shape[1]
    wt = jnp.zeros((size, 2 * width), F32)
    for d in range(2):
        r0 = offset - start + d * lora
        wt = wt.at[r0:r0 + lora, d * width:(d + 1) * width].set(blocks[d])
    return wt.astype(BF16)


def _rwkv_prepare(pa, n_ctx, a_in, mu, w0, w2, a0, a2, g2, k_k, k_a, r_k):
    n = pa.shape[0]
    w = g2.shape[1]
    tb = _pick(n_ctx if n_ctx else n, (64, 32, 16, 8))
    nblk = n // tb
    o_gl = 3 * w
    o_wl = o_gl + A_GATE_LORA
    o_al = o_wl + 2 * A_DECAY_LORA
    win_w = _aligned_window(o_wl, 2 * A_DECAY_LORA, a_in)
    win_a = _aligned_window(o_al, 2 * A_AAA_LORA, a_in)
    w2w = _window_weight(w2, o_wl, win_w, w)
    a2w = _window_weight(a2, o_al, win_a, w)
    g = tb // SUBLANES
    n_groups = n // SUBLANES
    prev_spec = pl.BlockSpec((SUBLANES, a_in), lambda i: (jnp.maximum(i * g - 1, 0), 0))
    next_spec = pl.BlockSpec((SUBLANES, a_in), lambda i: (jnp.minimum((i + 1) * g, n_groups - 1), 0))

    tok = pl.BlockSpec((tb, w), lambda i: (i, 0))
    tok2 = pl.BlockSpec((2, tb, w), lambda i: (0, i, 0))

    def full(a):
        return pl.BlockSpec(a.shape, lambda i: (0,) * a.ndim)

    params = [mu.reshape(1, a_in), w0, w2w, a0, a2w, g2.astype(BF16), k_k.reshape(1, w), k_a.reshape(1, w),
              r_k.reshape(1, w)]
    one = jax.ShapeDtypeStruct((n, w), F32)
    two = jax.ShapeDtypeStruct((2, n, w), F32)
    return pl.pallas_call(
        functools.partial(_rwkv_prep_kernel, tb=tb, n_ctx=n_ctx, n_rows=n, width=w, win_w=win_w, win_a=win_a,
                          o_gl=o_gl),
        grid=(nblk,),
        in_specs=[pl.BlockSpec((tb, a_in), lambda i: (i, 0)), prev_spec, next_spec] + [full(a) for a in params],
        out_specs=[tok, tok, tok, tok, tok, tok2, tok2, tok2],
        out_shape=[one, one, one, one, one, two, two, two],
        compiler_params=_cparams("parallel"),
        name="rwkv_prepare",
    )(pa, pa, pa, *params)


def _chunk_order(d, c, n_ctx_chunks, n_chunks):
    back = jnp.where(c < n_ctx_chunks, n_ctx_chunks - 1 - c, n_chunks - 1 - (c - n_ctx_chunks))
    return jnp.where(d == 0, c, back)


def _split2(x):
    hi = x.astype(BF16)
    return hi, (x - hi.astype(F32)).astype(BF16)


def _dot_x3(a, b):
    a_hi, a_lo = _split2(a)
    b_hi, b_lo = _split2(b)
    return _dot(a_hi, b_hi) + (_dot(a_lo, b_hi) + _dot(a_hi, b_lo))


def _rwkv_scan_kernel(*refs, pairs, casts):
    r_ref, kap_ref, v_ref, lw_ref, b_ref, k_ref = refs[:6]
    n_cast = len(casts)
    y_ref = refs[6 + n_cast]
    s_ref = refs[-1]
    c_len = r_ref.shape[0]
    d = pl.program_id(0)
    sign = jnp.where(d == 1, -1, 1)

    @pl.when(pl.program_id(2) == 0)
    def _():
        s_ref[...] = jnp.zeros_like(s_ref)

    step = (d * pl.num_programs(1) + pl.program_id(1)) * pl.num_programs(2) + pl.program_id(2)
    for src, dst, (n_src, n_dst) in zip(refs[6:6 + n_cast], refs[7 + n_cast:7 + 2 * n_cast], casts):
        cols = src.shape[1]
        is_data = jnp.minimum(step, n_dst - 1) < n_src
        dst[:, :cols] = jnp.where(is_data, src[...], 0.0).astype(BF16)
        if dst.shape[1] > cols:
            dst[:, cols:] = jnp.zeros((dst.shape[0], dst.shape[1] - cols), BF16)

    ti = lax.broadcasted_iota(jnp.int32, (c_len, c_len), 0)
    si = lax.broadcasted_iota(jnp.int32, (c_len, c_len), 1)
    upto = ((ti - si) * sign >= 0).astype(BF16)
    lane = lax.broadcasted_iota(jnp.int32, (c_len, PAIR), 1)
    head0 = lane < A_HEAD
    n2 = 2 * c_len
    row = lax.broadcasted_iota(jnp.int32, (n2, n2), 0)
    col = lax.broadcasted_iota(jnp.int32, (n2, n2), 1)
    same = (row // c_len) == (col // c_len)
    order = ((row % c_len) - (col % c_len)) * sign
    strict = same & (order > 0)
    incl = same & (order >= 0)
    eye = row == col

    def stack(x):
        return jnp.concatenate([jnp.where(head0, x, 0.0), jnp.where(head0, 0.0, x)], axis=0)

    def dup(x):
        return jnp.concatenate([x, x], axis=0)

    def bdot(a, b):
        return _dot(a.astype(BF16), b.astype(BF16))

    P = range(pairs)
    lanes = [slice(pi * PAIR, (pi + 1) * PAIR) for pi in P]
    lw = [lw_ref[:, ls] for ls in lanes]
    parts = [_split3(x) for x in lw]
    cl_in = [_dot(upto, hi) + (_dot(upto, mid) + _dot(upto, lo)) for hi, mid, lo in parts]
    cl_tot = [jnp.sum(x, axis=0, keepdims=True) for x in lw]
    g_in = [jnp.exp(c) for c in cl_in]
    g_ex = [jnp.exp(c - x) for c, x in zip(cl_in, lw)]
    g_inv = [jnp.exp(-c) for c in cl_in]
    g_end = [jnp.exp(t - c) for t, c in zip(cl_tot, cl_in)]

    ks = [stack(kap_ref[:, ls] * g).astype(BF16) for ls, g in zip(lanes, g_ex)]
    rs = [stack(r_ref[:, ls] * g) for ls, g in zip(lanes, g_in)]
    vs = [stack(v_ref[:, ls]).astype(BF16) for ls in lanes]
    bk = [jnp.concatenate([dup(b_ref[:, ls] * g), dup(k_ref[:, ls] * g)], axis=0).astype(BF16)
          for ls, g in zip(lanes, g_inv)]
    gram = [_dot_nt(jnp.concatenate([ks[i], rs[i].astype(BF16)], axis=0), bk[i]) for i in P]
    bhs_t = [stack(b_ref[:, ls] * g).T.astype(BF16) for ls, g in zip(lanes, g_end)]
    khs_t = [stack(k_ref[:, ls] * g).T.astype(BF16) for ls, g in zip(lanes, g_end)]
    n_ = [jnp.where(strict, gm[:n2, n2:], 0.0).astype(BF16) for gm in gram]
    lb = [jnp.where(incl, gm[n2:, :n2], 0.0).astype(BF16) for gm in gram]
    lk = [jnp.where(incl, gm[n2:, n2:], 0.0).astype(BF16) for gm in gram]

    pw = [jnp.where(strict, -gm[:n2, :n2], 0.0) for gm in gram]
    tinv = [jnp.where(eye, 1.0, 0.0) + x for x in pw]
    span = 2
    while span < c_len:
        pw = [bdot(x, x) for x in pw]
        tinv = [t + bdot(t, x) for t, x in zip(tinv, pw)]
        span *= 2

    nv = [_dot(n_[i], vs[i]) for i in P]
    kq_u = [bdot(tinv[i], jnp.concatenate([ks[i], nv[i].astype(BF16)], axis=1)).astype(BF16) for i in P]
    low = [_dot(bhs_t[i], kq_u[i]) for i in P]
    lbq = [_dot(lb[i], kq_u[i]) for i in P]
    kv = [_dot(khs_t[i], vs[i]) for i in P]
    lkv = [_dot(lk[i], vs[i]) for i in P]
    for i in P:
        a_ = jnp.where(eye, jnp.exp(cl_tot[i]), 0.0) - low[i][:, :PAIR]
        g_ = kv[i] - low[i][:, PAIR:]
        rq = rs[i] - lbq[i][:, :PAIR]
        yi = lkv[i] - lbq[i][:, PAIR:]
        prop = _dot_x3(jnp.concatenate([rq, a_], axis=0), s_ref[i])
        ys = prop[:n2] + yi
        y_ref[:, lanes[i]] = ys[:c_len] + ys[c_len:]
        s_ref[i] = prop[n2:] + g_


def _side_cast_plan(a, out_shape, steps):
    rows, cols = a.shape
    out_rows, out_cols = out_shape
    if cols % LANES or out_cols % LANES:
        return None
    rb = 2 * SUBLANES
    while rb <= out_rows and (out_rows % rb or rows % rb or out_rows // rb > steps):
        rb *= 2
    if rb > out_rows:
        return None
    return rb, rows // rb, out_rows // rb


def _rwkv_scan(r, kap, v, lw, b, kd, n_ctx, cast=()):
    n, w = r.shape
    c_len = RWKV_CHUNK
    n_chunks = n // c_len
    n_ctx_chunks = n_ctx // c_len
    pairs = _pick(w // PAIR, (16, 4, 2, 1))
    bw = pairs * PAIR
    steps = 2 * (w // bw) * n_chunks
    plans = [_side_cast_plan(a, shape, steps) for a, shape in cast]
    if any(p is None for p in plans):
        y, _ = _rwkv_scan(r, kap, v, lw, b, kd, n_ctx)
        return y, [jnp.pad(a, [(0, t - s) for s, t in zip(a.shape, shape)]).astype(BF16) for a, shape in cast]

    def tok(d, p, c):
        return (_chunk_order(d, c, n_ctx_chunks, n_chunks), p)

    def tok2(d, p, c):
        return (d, _chunk_order(d, c, n_ctx_chunks, n_chunks), p)

    def step_of(d, p, c):
        return (d * (w // bw) + p) * n_chunks + c

    def src_spec(a, plan):
        rb, n_src, _ = plan
        return pl.BlockSpec((rb, a.shape[1]), lambda d, p, c: (jnp.minimum(step_of(d, p, c), n_src - 1), 0))

    def dst_spec(shape, plan):
        rb, _, n_dst = plan
        return pl.BlockSpec((rb, shape[1]), lambda d, p, c: (jnp.minimum(step_of(d, p, c), n_dst - 1), 0))

    one = pl.BlockSpec((c_len, bw), tok)
    two = pl.BlockSpec((None, c_len, bw), tok2)
    outs = pl.pallas_call(
        functools.partial(_rwkv_scan_kernel, pairs=pairs, casts=tuple((p[1], p[2]) for p in plans)),
        grid=(2, w // bw, n_chunks),
        in_specs=[one, one, one, two, two, two] + [src_spec(a, p) for (a, _), p in zip(cast, plans)],
        out_specs=[two] + [dst_spec(shape, p) for (_, shape), p in zip(cast, plans)],
        out_shape=[jax.ShapeDtypeStruct((2, n, w), F32)] + [jax.ShapeDtypeStruct(shape, BF16) for _, shape in cast],
        scratch_shapes=[pltpu.VMEM((pairs, PAIR, PAIR), F32)],
        compiler_params=_cparams("arbitrary", "arbitrary", "arbitrary"),
        name="rwkv_scan",
    )(r, kap, v, lw, b, kd, *[a for a, _ in cast])
    return outs[0], list(outs[1:])


def _rope_tables(n_ctx, n_lat, half):
    nq = half // 2
    pos = jnp.arange(n_lat, dtype=jnp.int32)
    row = (pos // GRID_W).astype(F32)
    col = (pos % GRID_W).astype(F32)
    inv = ROPE_THETA ** (-jnp.arange(nq, dtype=F32) / nq)

    def part(p):
        ang = p[:, None] * inv[None, :]
        c, s = jnp.cos(ang), jnp.sin(ang)
        return jnp.concatenate([c, c], -1), jnp.concatenate([-s, s], -1)

    cr, sr = part(row)
    cc, sc = part(col)
    cos = jnp.concatenate([cr, cc], -1)
    sin = jnp.concatenate([sr, sc], -1)
    cos = jnp.concatenate([jnp.ones((n_ctx, 2 * half), F32), cos], 0)
    sin = jnp.concatenate([jnp.zeros((n_ctx, 2 * half), F32), sin], 0)
    return cos, sin


def _rope_tile(x, cos, sin, quarter):
    if 2 * quarter == LANES:
        partner = pltpu.roll(x, quarter, 1)
    else:
        lane = lax.broadcasted_iota(jnp.int32, x.shape, 1)
        lower = (lane % (2 * quarter)) < quarter
        partner = jnp.where(lower, pltpu.roll(x, LANES - quarter, 1), pltpu.roll(x, quarter, 1))
    return x * cos + partner * sin


def _retention_kernel(q_ref, k_ref, v_ref, cos_ref, sin_ref, lg_ref, y_ref, s_ref, *, heads):
    c_len = q_ref.shape[0]
    d = pl.program_id(0)
    rev = d == 1

    @pl.when(pl.program_id(1) == 0)
    def _():
        s_ref[...] = jnp.zeros_like(s_ref)

    cos = cos_ref[...]
    sin = sin_ref[...]

    def rope(x):
        tiles = [_rope_tile(x[:, j * LANES:(j + 1) * LANES], cos[:, j * LANES:(j + 1) * LANES],
                            sin[:, j * LANES:(j + 1) * LANES], LANES // 2) for j in range(B_HEAD // LANES)]
        return jnp.concatenate(tiles, axis=-1)

    n_i = lax.broadcasted_iota(jnp.int32, (c_len, c_len), 0)
    m_i = lax.broadcasted_iota(jnp.int32, (c_len, c_len), 1)
    dist = (n_i - m_i) * jnp.where(rev, -1, 1)
    seen = dist >= jnp.where(rev, 1, 0)
    dist_f = jnp.maximum(dist, 0).astype(F32)
    idx = lax.broadcasted_iota(jnp.int32, (c_len, B_HEAD), 0)
    order = jnp.where(rev, c_len - 1 - idx, idx).astype(F32)

    H = range(heads)
    cols = [slice(h * B_HEAD, (h + 1) * B_HEAD) for h in H]
    lg = [jnp.concatenate([lg_ref[h, 0:1, :]] * (B_HEAD // LANES), axis=-1) for h in H]
    lg_c = [jnp.concatenate([lg_ref[h, 0:1, :]] * (c_len // LANES), axis=-1) for h in H]
    q = [rope(q_ref[:, cs]) for cs in cols]
    k = [rope(k_ref[:, cs]) * (B_HEAD ** -0.5) for cs in cols]
    v = [v_ref[:, cs].astype(BF16) for cs in cols]
    inner = [_dot_nt(q[h].astype(BF16), k[h].astype(BF16)) for h in H]
    cross = [_dot((q[h] * jnp.exp(lg[h] * (order + 1.0))).astype(BF16), s_ref[h].astype(BF16)) for h in H]
    kt = [(k[h] * jnp.exp(lg[h] * (c_len - 1.0 - order))).T.astype(BF16) for h in H]
    upd = [_dot(kt[h], v[h]) for h in H]
    for h in H:
        decay = jnp.where(seen, jnp.exp(lg_c[h] * dist_f), 0.0)
        y_ref[:, cols[h]] = _dot((inner[h] * decay).astype(BF16), v[h]) + cross[h]
        s_ref[h] = s_ref[h] * jnp.exp(lg[h] * float(c_len)) + upd[h]


def _retention(pb, n_ctx, cos, sin):
    n = pb.shape[0]
    width = pb.shape[1] // 4
    heads = width // B_HEAD
    c_len = RET_CHUNK
    n_chunks = n // c_len
    n_ctx_chunks = n_ctx // c_len
    log_g = jnp.log1p(-jnp.exp2(-5.0 - jnp.arange(heads, dtype=F32)))
    lg = jnp.stack([log_g, log_g[::-1]])
    lg = jnp.broadcast_to(lg[:, :, None, None], (2, heads, SUBLANES, LANES))

    def part(which):
        return pl.BlockSpec((c_len, width), lambda d, c: (_chunk_order(d, c, n_ctx_chunks, n_chunks), which))

    tab = pl.BlockSpec((c_len, B_HEAD), lambda d, c: (_chunk_order(d, c, n_ctx_chunks, n_chunks), 0))
    return pl.pallas_call(
        functools.partial(_retention_kernel, heads=heads),
        grid=(2, n_chunks),
        in_specs=[part(0), part(1), part(2), tab, tab,
                  pl.BlockSpec((None, heads, SUBLANES, LANES), lambda d, c: (d, 0, 0, 0))],
        out_specs=pl.BlockSpec((None, c_len, width),
                               lambda d, c: (d, _chunk_order(d, c, n_ctx_chunks, n_chunks), 0)),
        out_shape=jax.ShapeDtypeStruct((2, n, width), F32),
        scratch_shapes=[pltpu.VMEM((heads, B_HEAD, B_HEAD), F32)],
        compiler_params=_cparams("parallel", "arbitrary"),
        name="retention",
    )(pb, pb, pb, cos, sin, lg)


def _mix_finish_kernel(ya_ref, bonus_ref, g_ref, yb_ref, gate_ref, lg_ref, lb_ref, gg_ref, gb_ref, o_ref):
    wa = bonus_ref.shape[1]
    ya = ya_ref[0] + ya_ref[1]
    seg = _head_seg_matrix(A_HEAD)
    inv = 1.0 / A_HEAD
    for j in range(wa // LANES):
        sl = slice(j * LANES, (j + 1) * LANES)
        t = ya[:, sl]
        mu = _segsum_tile(t, seg) * inv
        tc = t - mu
        var = _segsum_tile(tc * tc, seg) * inv
        o = tc * lax.rsqrt(var + A_GN_EPS) * lg_ref[:, sl] + lb_ref[:, sl] + bonus_ref[:, sl]
        o_ref[:, sl] = (o * g_ref[:, sl]).astype(o_ref.dtype)
    yb = yb_ref[0] + yb_ref[1]
    wb = yb.shape[1]
    for j in range(wb // B_HEAD):
        sl = slice(j * B_HEAD, (j + 1) * B_HEAD)
        t = yb[:, sl]
        mu = jnp.mean(t, axis=-1, keepdims=True)
        tc = t - mu
        var = jnp.mean(tc * tc, axis=-1, keepdims=True)
        o = tc * lax.rsqrt(var + B_GN_EPS) * gg_ref[:, sl] + gb_ref[:, sl]
        o_ref[:, wa + j * B_HEAD:wa + (j + 1) * B_HEAD] = (o * _silu(gate_ref[:, sl])).astype(o_ref.dtype)


def _mix_finish(ya, bonus, g, yb, pb, lnx_g, lnx_b, gn_g, gn_b):
    n, wa = bonus.shape
    wb = yb.shape[2]
    tb = _pick(n, (128, 64, 32, 16, 8))
    ra = pl.BlockSpec((tb, wa), lambda i: (i, 0))
    rb = pl.BlockSpec((tb, wb), lambda i: (i, 3))
    va = pl.BlockSpec((1, wa), lambda i: (0, 0))
    vb = pl.BlockSpec((1, wb), lambda i: (0, 0))
    return pl.pallas_call(
        _mix_finish_kernel,
        grid=(n // tb,),
        in_specs=[pl.BlockSpec((2, tb, wa), lambda i: (0, i, 0)), ra, ra,
                  pl.BlockSpec((2, tb, wb), lambda i: (0, i, 0)), rb, va, va, vb, vb],
        out_specs=pl.BlockSpec((tb, wa + wb), lambda i: (i, 0)),
        out_shape=jax.ShapeDtypeStruct((n, wa + wb), BF16),
        compiler_params=_cparams("parallel"),
        name="mix_finish",
    )(ya, bonus, g, yb, pb, lnx_g.reshape(1, wa), lnx_b.reshape(1, wa), gn_g.reshape(1, wb), gn_b.reshape(1, wb))


def _qkv_prep_kernel(x_ref, cos_ref, sin_ref, qg_ref, kg_ref, q_ref, k_ref, v_ref, *, q_heads, kv_heads):
    cos = cos_ref[...]
    sin = sin_ref[...]
    quarter = C_HEAD // 4
    lane = lax.broadcasted_iota(jnp.int32, cos.shape, 1)
    lower = (lane % (2 * quarter)) < quarter
    scale = C_HEAD ** -0.5 * LOG2_E
    n_heads = q_heads + kv_heads
    tiles = [x_ref[:, h * C_HEAD:(h + 1) * C_HEAD] for h in range(n_heads)]
    ms = [jnp.mean(t * t, axis=-1, keepdims=True) for t in tiles]
    gq = qg_ref[...] * scale
    normed = [t * lax.rsqrt(m + QK_EPS) * (gq if h < q_heads else kg_ref[...])
              for h, (t, m) in enumerate(zip(tiles, ms))]
    up = [pltpu.roll(t, LANES - quarter, 1) for t in normed]
    down = [pltpu.roll(t, quarter, 1) for t in normed]
    for h in range(n_heads):
        y = (normed[h] * cos + jnp.where(lower, up[h], down[h]) * sin).astype(BF16)
        if h < q_heads:
            q_ref[:, h * C_HEAD:(h + 1) * C_HEAD] = y
        else:
            k_ref[:, (h - q_heads) * C_HEAD:(h - q_heads + 1) * C_HEAD] = y
    v0 = n_heads * C_HEAD
    v_ref[...] = x_ref[:, v0:v0 + kv_heads * C_HEAD].astype(BF16)


def _qkv_prep(qkv, cos, sin, qn_g, kn_g, q_heads, kv_heads):
    n, c_in = qkv.shape
    tb = _pick(n, (128, 64, 32, 16, 8))
    kvw = kv_heads * C_HEAD
    return pl.pallas_call(
        functools.partial(_qkv_prep_kernel, q_heads=q_heads, kv_heads=kv_heads),
        grid=(n // tb,),
        in_specs=[pl.BlockSpec((tb, c_in), lambda i: (i, 0)),
                  pl.BlockSpec((tb, C_HEAD), lambda i: (i, 0)),
                  pl.BlockSpec((tb, C_HEAD), lambda i: (i, 0)),
                  pl.BlockSpec((1, C_HEAD), lambda i: (0, 0)),
                  pl.BlockSpec((1, C_HEAD), lambda i: (0, 0))],
        out_specs=[pl.BlockSpec((tb, q_heads * C_HEAD), lambda i: (i, 0)),
                   pl.BlockSpec((tb, kvw), lambda i: (i, 0)),
                   pl.BlockSpec((tb, kvw), lambda i: (i, 0))],
        out_shape=[jax.ShapeDtypeStruct((n, q_heads * C_HEAD), BF16),
                   jax.ShapeDtypeStruct((n, kvw), BF16),
                   jax.ShapeDtypeStruct((n, kvw), BF16)],
        compiler_params=_cparams("parallel"),
        name="qkv_prep",
    )(qkv, cos, sin, qn_g.reshape(1, C_HEAD), kn_g.reshape(1, C_HEAD))


def _flash_kernel(*refs, tq, tk, n_kv, n_cast):
    q_ref, k_ref, v_ref = refs[:3]
    cast_in = refs[3:3 + n_cast]
    o_ref = refs[3 + n_cast]
    cast_out = refs[4 + n_cast:4 + 2 * n_cast]
    q_scr, s_scr, p_scr, m_scr, c_scr, acc_scr = refs[4 + 2 * n_cast:]
    for src, dst in zip(cast_in, cast_out):
        dst[...] = src[...].astype(BF16)
    nkv = n_kv // tk
    for h in range(C_GROUP):
        q_scr[h * tq:(h + 1) * tq, :] = q_ref[:, h * C_HEAD:(h + 1) * C_HEAD]
    m_scr[...] = jnp.full(m_scr.shape, -jnp.inf, F32)
    acc_scr[...] = jnp.zeros(acc_scr.shape, F32)
    n_col = tk // LANES

    half = s_scr.shape[1] // 2
    halves = (slice(0, half), slice(half, 2 * half))

    def scores(j, slot):
        kj = k_ref[:, pl.ds(pl.multiple_of(j * tk, tk), tk)]
        for rs in halves:
            s_scr[slot, rs, :] = _dot(q_scr[rs, :], kj)

    def values(j, slot):
        vj = jnp.concatenate([v_ref[pl.ds(pl.multiple_of(j * tk, tk), tk), :], jnp.ones((tk, C_HEAD), BF16)], axis=1)
        return jnp.concatenate([_dot(p_scr[slot, rs, :], vj) for rs in halves], axis=0)

    def normalise(slot):
        for rs in halves:
            tile_max = s_scr[slot, rs, 0:LANES]
            for c in range(1, n_col):
                tile_max = jnp.maximum(tile_max, s_scr[slot, rs, c * LANES:(c + 1) * LANES])
            row_max = jnp.broadcast_to(jnp.max(tile_max, axis=-1, keepdims=True), tile_max.shape)
            m_old = m_scr[rs, :]
            m_new = jnp.maximum(m_old, row_max)
            for c in range(n_col):
                p = jnp.exp2(s_scr[slot, rs, c * LANES:(c + 1) * LANES] - m_new)
                p_scr[slot, rs, c * LANES:(c + 1) * LANES] = p.astype(BF16)
            m_scr[rs, :] = m_new
            corr = jnp.exp2(m_old - m_new)
            c_scr[rs, :] = jnp.concatenate([corr, corr], axis=1)

    scores(0, 0)
    if nkv > 1:
        scores(1, 1)
    normalise(0)

    def step(j, cur):
        scores(j + 1, 1 - cur)
        normalise(cur)
        acc_scr[...] = (acc_scr[...] + values(j - 1, 1 - cur)) * c_scr[...]

    def pair(t, carry):
        step(2 * t + 1, 1)
        step(2 * t + 2, 0)
        return carry

    n_mid = max(nkv - 2, 0)
    if n_mid >= 2:
        lax.fori_loop(0, n_mid // 2, pair, 0)
    if n_mid % 2:
        step(nkv - 2, (nkv - 2) % 2)
    if nkv > 1:
        last = (nkv - 1) % 2
        normalise(last)
        acc_scr[...] = (acc_scr[...] + values(nkv - 2, 1 - last)) * c_scr[...]
    acc = acc_scr[...] + values(nkv - 1, (nkv - 1) % 2)
    out = acc[:, :C_HEAD] / acc[:, C_HEAD:]
    for h in range(C_GROUP):
        o_ref[:, h * C_HEAD:(h + 1) * C_HEAD] = out[h * tq:(h + 1) * tq].astype(o_ref.dtype)


def _flash(q, k, v, n_ctx, cast=()):
    n = q.shape[0]
    kv_heads = k.shape[1] // C_HEAD
    n_lat = n - n_ctx
    tq = _pick(n_ctx, (256, 128, 64, 32, 16, 8))
    tk = _pick(n, (1280, 1024, 640, 512, 256, 128, 64, 32, 16, 8))
    gw = C_GROUP * C_HEAD
    q0 = n_ctx // tq
    rows = C_GROUP * tq
    n_qb = n_lat // tq
    steps = kv_heads * n_qb
    slabs = [a.shape[0] * a.shape[1] // steps for a in cast]
    if not all(a.ndim == 3 and a.shape[0] * a.shape[1] == sl * steps and a.shape[1] % sl == 0
               and sl % (2 * SUBLANES) == 0 for a, sl in zip(cast, slabs)):
        return _flash(q, k, v, n_ctx)[0], [a.astype(BF16) for a in cast]

    def slab_spec(a, sl):
        per = a.shape[1] // sl
        return pl.BlockSpec((None, sl, a.shape[2]), lambda g, i: ((g * n_qb + i) // per, (g * n_qb + i) % per, 0))

    cast_specs = [slab_spec(a, sl) for a, sl in zip(cast, slabs)]
    outs = pl.pallas_call(
        functools.partial(_flash_kernel, tq=tq, tk=tk, n_kv=n, n_cast=len(cast)),
        grid=(kv_heads, n_qb),
        in_specs=[pl.BlockSpec((tq, gw), lambda g, i: (q0 + i, g)),
                  pl.BlockSpec((C_HEAD, n), lambda g, i: (g, 0)),
                  pl.BlockSpec((n, C_HEAD), lambda g, i: (0, g))] + cast_specs,
        out_specs=[pl.BlockSpec((tq, gw), lambda g, i: (i, g))] + cast_specs,
        out_shape=[jax.ShapeDtypeStruct((n_lat, kv_heads * gw), BF16)]
        + [jax.ShapeDtypeStruct(a.shape, BF16) for a in cast],
        scratch_shapes=[pltpu.VMEM((rows, C_HEAD), BF16),
                        pltpu.VMEM((2, rows, tk), F32),
                        pltpu.VMEM((2, rows, tk), BF16),
                        pltpu.VMEM((rows, LANES), F32),
                        pltpu.VMEM((rows, 2 * C_HEAD), F32),
                        pltpu.VMEM((rows, 2 * C_HEAD), F32)],
        compiler_params=_cparams("parallel", "parallel"),
        name="flash_gqa",
    )(q, k.T, v, *cast)
    return outs[0], list(outs[1:])


def _pad_cols(w, n):
    return jnp.pad(w, ((0, 0), (0, n - w.shape[1])))


def kernel(x, c, ctx, c_ctx, l0_mod_w, l0_mod_b, l0_w_in, l0_a_mu, l0_a_w0, l0_a_w2, l0_a_a0, l0_a_a2, l0_a_g2, l0_a_k_k, l0_a_k_a, l0_a_r_k, l0_a_lnx_g, l0_a_lnx_b, l0_b_gn_g, l0_b_gn_b, l0_w_out, l0_ln1_g, l0_ln1_b, l0_ffn_w1, l0_ffn_w3, l0_ffn_w2, l0_ln2_g, l0_ln2_b, l1_mod_w, l1_mod_b, l1_w_qkv, l1_q_norm_g, l1_k_norm_g, l1_w_out, l1_ln1_g, l1_ln1_b, l1_router, l1_moe_w1, l1_moe_w3, l1_moe_w2, l1_ln2_g, l1_ln2_b):
    batch, n_lat, d = x.shape
    n_ctx = ctx.shape[1]
    assert batch == 1 and c.shape[0] == 1
    a_width = l0_a_g2.shape[1]
    a_in = l0_a_mu.shape[0]
    b_width = l0_b_gn_g.shape[0]

    c_rows = jnp.zeros((SUBLANES, d), F32).at[0].set(c[0]).at[1].set(c_ctx)
    mod0 = _mod_vectors(c_rows, l0_mod_w, l0_mod_b)
    mod1 = _mod_vectors(c_rows, l1_mod_w, l1_mod_b)

    h = jnp.concatenate([ctx[0], x[0]], axis=0)

    _, u = _ln_mod(h, None, mod0, 0, n_ctx)
    a_pad = _round_up(a_in, 1024) if a_in > 1024 else a_in
    w_in_a = _pad_cols(l0_w_in[:, :a_in], a_pad).astype(BF16)
    w_in_b = l0_w_in[:, a_in:].astype(BF16)
    pa = _mm_plain(u, w_in_a, F32)
    pb = _mm_plain(u, w_in_b, F32)
    r, v, kap, g, bonus, lw, b, kd = _rwkv_prepare(pa, n_ctx, a_in, l0_a_mu, l0_a_w0, l0_a_w2, l0_a_a0, l0_a_a2,
                                                   l0_a_g2, l0_a_k_k, l0_a_k_a, l0_a_r_k)
    d_ff = l0_ffn_w1.shape[1]
    ff_pad = _round_up(d_ff, 1024) if d_ff > 1024 else d_ff
    ya, (ffn_w1, ffn_w3, ffn_w2) = _rwkv_scan(
        r, kap, v, lw, b, kd, n_ctx,
        cast=((l0_ffn_w1, (d, ff_pad)), (l0_ffn_w3, (d, ff_pad)), (l0_ffn_w2, (ff_pad, d))))
    cos_b, sin_b = _rope_tables(n_ctx, n_lat, B_HEAD // 2)
    yb = _retention(pb, n_ctx, cos_b, sin_b)
    mix = _mix_finish(ya, bonus, g, yb, pb, l0_a_lnx_g, l0_a_lnx_b, l0_b_gn_g, l0_b_gn_b)
    res = _mm_res(mix, l0_w_out.astype(BF16), h, mod0, 2, n_ctx)
    h, u = _ln_mod(res, (l0_ln1_g, l0_ln1_b), mod0, 3, n_ctx)
    hid = _mm_swiglu(u, ffn_w1, ffn_w3)
    res = _mm_res(hid, ffn_w2, h, mod0, 5, n_ctx)
    h_lat, u = _ln_mod(res, (l0_ln2_g, l0_ln2_b), mod1, 0, n_ctx, h_from=n_ctx)

    q_heads = l1_w_out.shape[0] // C_HEAD
    kv_heads = q_heads // C_GROUP
    qkv = _mm_plain(u, l1_w_qkv.astype(BF16), F32)
    cos_c, sin_c = _rope_tables(n_ctx, n_lat, C_HEAD // 2)
    q, k, vv = _qkv_prep(qkv, cos_c, sin_c, l1_q_norm_g, l1_k_norm_g, q_heads, kv_heads)
    att, (moe_w1, moe_w3, moe_w2) = _flash(q, k, vv, n_ctx, cast=(l1_moe_w1, l1_moe_w3, l1_moe_w2))
    res = _mm_res(att, l1_w_out.astype(BF16), h_lat, mod1, 2, 0)
    h_lat, u, comb = _ln_mod(res, (l1_ln1_g, l1_ln1_b), mod1, 3, 0, router=l1_router)
    n_exp = l1_router.shape[1]
    cexp = jnp.broadcast_to(comb[:, :n_exp].T[:, :, None], (n_exp, n_lat, LANES))
    hid = _mm_swiglu(u, moe_w1, moe_w3, cexp)
    w2 = moe_w2.reshape(n_exp * moe_w2.shape[1], d)
    res = _mm_res(hid, w2, h_lat, mod1, 5, 0)
    out, _ = _ln_mod(res, (l1_ln2_g, l1_ln2_b), None, 0, 0)
    return out[None]
```
